```python
import math
import jax, jax.numpy as jnp
from jax import lax
import numpy as np

D_MODEL = 1024
BATCH = 8
SEQ = 2048
DEPTH = 2

HEAD_DIM = 64
POOL_GROUPS = 4
POOL_WINDOWS = (2, 4, 8, 16)
POOL_GROUP_DIM = 128
POOL_WIDTH = POOL_GROUPS * POOL_GROUP_DIM
DIFF_HEADS = 4
DIFF_QK = DIFF_HEADS * 2 * HEAD_DIM
DIFF_V_DIM = 2 * HEAD_DIM
DIFF_V = DIFF_HEADS * DIFF_V_DIM
FOX_HEADS = 8
FOX_QK = FOX_HEADS * HEAD_DIM
FOX_V = FOX_HEADS * HEAD_DIM
N_BRANCH = 3
BRANCH_WIDTH = 512
ROT_DIM = HEAD_DIM // 4
ROPE_THETA = 500000.0
Q_BLOCK = 128
PLE_DIM = 256
N_GROUPS = 4
EXPERTS_PER_GROUP = 4
N_EXPERTS = N_GROUPS * EXPERTS_PER_GROUP
TOP_K = 2
D_EXPERT = 256
RMS_EPS = 1e-6
IN_SPLITS = (POOL_WIDTH, DIFF_QK, DIFF_QK, DIFF_V, FOX_QK, FOX_QK, FOX_V, FOX_HEADS, N_BRANCH * D_MODEL)
IN_WIDTH = sum(IN_SPLITS)

kernel_name = "hybrid_pool_diffattn_fox_hmoe_ple"


def rms_norm(x, g):
    x32 = x.astype(jnp.float32)
    y = x32 * lax.rsqrt(jnp.mean(x32 * x32, axis=-1, keepdims=True) + RMS_EPS)
    return (y * g.astype(jnp.float32)).astype(x.dtype)


def partial_rotary(x, cos, sin):
    half = ROT_DIM // 2
    x1 = x[..., :half]
    x2 = x[..., half:ROT_DIM]
    c = cos.astype(x.dtype)
    s = sin.astype(x.dtype)
    return jnp.concatenate([x1 * c - x2 * s, x2 * c + x1 * s, x[..., ROT_DIM:]], axis=-1)


def pool_mixer(u, w_grp, scale):
    B, S, _ = u.shape
    u = u.reshape(B, S, POOL_GROUPS, POOL_GROUP_DIM)
    u32 = u.astype(jnp.float32)
    csum = jnp.cumsum(u32, axis=1)
    t = jnp.arange(S)
    outs = []
    for g, w in enumerate(POOL_WINDOWS):
        c = csum[:, :, g]
        lag = jnp.pad(c, ((0, 0), (w, 0), (0, 0)))[:, :S]
        cnt = jnp.minimum(t + 1, w).astype(jnp.float32)[None, :, None]
        outs.append((c - lag) / cnt - u32[:, :, g])
    d = jnp.stack(outs, axis=2).astype(u.dtype)
    y = jnp.einsum('bsgc,gcd->bsgd', d, w_grp)
    return y.reshape(B, S, POOL_WIDTH) * scale


def _masked_softmax(s, mask):
    return jax.nn.softmax(jnp.where(mask, s, -jnp.inf), axis=-1)


def diff_attention(q1, q2, k1, k2, v, lam):
    S, dh = q1.shape[2], q1.shape[3]
    scale = dh ** -0.5
    outs = []
    for q0 in range(0, S, Q_BLOCK):
        kend = q0 + Q_BLOCK
        mask = jnp.arange(kend)[None, :] <= jnp.arange(q0, kend)[:, None]
        s1 = jnp.einsum('bhqd,bhkd->bhqk', q1[:, :, q0:kend], k1[:, :, :kend]).astype(jnp.float32) * scale
        s2 = jnp.einsum('bhqd,bhkd->bhqk', q2[:, :, q0:kend], k2[:, :, :kend]).astype(jnp.float32) * scale
        a = _masked_softmax(s1, mask) - lam * _masked_softmax(s2, mask)
        outs.append(jnp.einsum('bhqk,bhkd->bhqd', a.astype(v.dtype), v[:, :, :kend]))
    return jnp.concatenate(outs, axis=2)


def forgetting_attention(q, k, v, logf):
    S, dh = q.shape[2], q.shape[3]
    scale = dh ** -0.5
    F = jnp.cumsum(logf, axis=-1)
    outs = []
    for q0 in range(0, S, Q_BLOCK):
        kend = q0 + Q_BLOCK
        mask = jnp.arange(kend)[None, :] <= jnp.arange(q0, kend)[:, None]
        s = jnp.einsum('bhqd,bhkd->bhqk', q[:, :, q0:kend], k[:, :, :kend]).astype(jnp.float32) * scale
        s = s + (F[:, :, q0:kend, None] - F[:, :, None, :kend])
        a = _masked_softmax(s, mask)
        outs.append(jnp.einsum('bhqk,bhkd->bhqd', a.astype(v.dtype), v[:, :, :kend]))
    return jnp.concatenate(outs, axis=2)


def hierarchical_moe(h, w_rg, b_rg, w_re, b_re, w_gate, w_up, w_down):
    T = h.shape[0]
    g_prob = jax.nn.softmax((h @ w_rg).astype(jnp.float32) + b_rg.astype(jnp.float32), axis=-1)
    gp, gi = lax.top_k(g_prob, 1)
    e_logit = ((h @ w_re).astype(jnp.float32) + b_re.astype(jnp.float32)).reshape(T, N_GROUPS, EXPERTS_PER_GROUP)
    e_sel = jnp.take_along_axis(e_logit, gi[:, :, None], axis=1)[:, 0]
    ew, ei = lax.top_k(jax.nn.softmax(e_sel, axis=-1), TOP_K)
    weight = gp * ew / jnp.sum(ew, axis=-1, keepdims=True)
    eid = gi * EXPERTS_PER_GROUP + ei
    combine = jnp.sum(jax.nn.one_hot(eid, N_EXPERTS, dtype=jnp.float32) * weight[..., None], axis=1)
    act = jax.nn.silu(jnp.einsum('td,edf->tef', h, w_gate)) * jnp.einsum('td,edf->tef', h, w_up)
    return jnp.einsum('tef,efd->td', act * combine[:, :, None].astype(act.dtype), w_down)


def setup_inputs(seed: int = 0) -> dict:
    key = jax.random.key(seed)
    ks = jax.random.split(key, 32)
    f32 = jnp.float32
    L, D = DEPTH, D_MODEL

    def nrm(k, shape, scale):
        return jax.random.normal(k, shape, f32) * scale

    def gain(k, shape):
        return 1.0 + 0.05 * jax.random.normal(k, shape, f32)

    return {
        "x": nrm(ks[0], (BATCH, SEQ, D), 1.0),
        "p": nrm(ks[1], (L, BATCH, SEQ, PLE_DIM), 1.0),
        "positions": jnp.broadcast_to(jnp.arange(SEQ, dtype=jnp.int32), (BATCH, SEQ)),
        "attn_norm_g": gain(ks[2], (L, D)),
        "w_in": nrm(ks[3], (L, D, IN_WIDTH), D ** -0.5),
        "pool_w": nrm(ks[4], (L, POOL_GROUPS, POOL_GROUP_DIM, POOL_GROUP_DIM), POOL_GROUP_DIM ** -0.5),
        "pool_scale": gain(ks[5], (L, POOL_WIDTH)),
        "diff_qn_g": gain(ks[6], (L, HEAD_DIM)),
        "diff_kn_g": gain(ks[7], (L, HEAD_DIM)),
        "diff_lambda": nrm(ks[8], (L, 4, HEAD_DIM), 0.1),
        "diff_subln_g": gain(ks[9], (L, DIFF_V_DIM)),
        "fox_qn_g": gain(ks[10], (L, HEAD_DIM)),
        "fox_kn_g": gain(ks[11], (L, HEAD_DIM)),
        "fox_forget_b": 2.0 + 0.1 * jax.random.normal(ks[12], (L, FOX_HEADS), f32),
        "w_branch": nrm(ks[13], (L, N_BRANCH, BRANCH_WIDTH, D), BRANCH_WIDTH ** -0.5),
        "w_out": nrm(ks[14], (L, D, D), D ** -0.5),
        "ffn_norm_g": gain(ks[15], (L, D)),
        "w_route_group": nrm(ks[16], (L, D, N_GROUPS), D ** -0.5),
        "b_route_group": nrm(ks[17], (L, N_GROUPS), 0.01),
        "w_route_expert": nrm(ks[18], (L, D, N_EXPERTS), D ** -0.5),
        "b_route_expert": nrm(ks[19], (L, N_EXPERTS), 0.01),
        "moe_w_gate": nrm(ks[20], (L, N_EXPERTS, D, D_EXPERT), D ** -0.5),
        "moe_w_up": nrm(ks[21], (L, N_EXPERTS, D, D_EXPERT), D ** -0.5),
        "moe_w_down": nrm(ks[22], (L, N_EXPERTS, D_EXPERT, D), D_EXPERT ** -0.5),
        "ple_norm_g": gain(ks[23], (L, D)),
        "w_ple_gate": nrm(ks[24], (L, D, D), D ** -0.5),
        "w_ple": nrm(ks[25], (L, PLE_DIM, D), PLE_DIM ** -0.5),
    }


def reference(x, p, positions, attn_norm_g, w_in, pool_w, pool_scale, diff_qn_g, diff_kn_g, diff_lambda,
              diff_subln_g, fox_qn_g, fox_kn_g, fox_forget_b, w_branch, w_out, ffn_norm_g, w_route_group,
              b_route_group, w_route_expert, b_route_expert, moe_w_gate, moe_w_up, moe_w_down, ple_norm_g,
              w_ple_gate, w_ple):
    B, S, D = x.shape
    inv_freq = ROPE_THETA ** (-jnp.arange(0, ROT_DIM, 2, dtype=jnp.float32) / ROT_DIM)
    ang = positions.astype(jnp.float32)[:, None, :, None] * inv_freq
    cos, sin = jnp.cos(ang), jnp.sin(ang)
    split_idx = np.cumsum(IN_SPLITS)[:-1].tolist()

    for l in range(DEPTH):
        h = rms_norm(x, attn_norm_g[l])
        proj = h @ w_in[l]
        u_pool, dq, dk, dv, fq, fk, fv, fz, gz = jnp.split(proj, split_idx, axis=-1)

        y_pool = pool_mixer(u_pool, pool_w[l], pool_scale[l])

        dq = jnp.transpose(rms_norm(dq.reshape(B, S, DIFF_HEADS, 2, HEAD_DIM), diff_qn_g[l]), (0, 2, 3, 1, 4))
        dk = jnp.transpose(rms_norm(dk.reshape(B, S, DIFF_HEADS, 2, HEAD_DIM), diff_kn_g[l]), (0, 2, 3, 1, 4))
        q1, q2 = partial_rotary(dq[:, :, 0], cos, sin), partial_rotary(dq[:, :, 1], cos, sin)
        k1, k2 = partial_rotary(dk[:, :, 0], cos, sin), partial_rotary(dk[:, :, 1], cos, sin)
        dv = jnp.transpose(dv.reshape(B, S, DIFF_HEADS, DIFF_V_DIM), (0, 2, 1, 3))
        lam_init = 0.8 - 0.6 * math.exp(-0.3 * l)
        lp = diff_lambda[l].astype(jnp.float32)
        lam = jnp.exp(jnp.sum(lp[0] * lp[1])) - jnp.exp(jnp.sum(lp[2] * lp[3])) + lam_init
        o_diff = diff_attention(q1, q2, k1, k2, dv, lam)
        o_diff = rms_norm(o_diff, diff_subln_g[l]) * (1.0 - lam_init)
        y_diff = jnp.transpose(o_diff, (0, 2, 1, 3)).reshape(B, S, DIFF_V)

        fq = jnp.transpose(rms_norm(fq.reshape(B, S, FOX_HEADS, HEAD_DIM), fox_qn_g[l]), (0, 2, 1, 3))
        fk = jnp.transpose(rms_norm(fk.reshape(B, S, FOX_HEADS, HEAD_DIM), fox_kn_g[l]), (0, 2, 1, 3))
        fv = jnp.transpose(fv.reshape(B, S, FOX_HEADS, HEAD_DIM), (0, 2, 1, 3))
        logf = jnp.transpose(jax.nn.log_sigmoid(fz.astype(jnp.float32) + fox_forget_b[l].astype(jnp.float32)), (0, 2, 1))
        o_fox = forgetting_attention(fq, fk, fv, logf)
        y_fox = jnp.transpose(o_fox, (0, 2, 1, 3)).reshape(B, S, FOX_V)

        branches = jnp.stack([y_pool, y_diff, y_fox], axis=2)
        br = jnp.einsum('bsnc,ncd->bsnd', branches, w_branch[l])
        gates = jax.nn.sigmoid(gz.reshape(B, S, N_BRANCH, D))
        merged = jnp.sum(gates * br, axis=2)
        x = x + merged @ w_out[l]

        hf = rms_norm(x, ffn_norm_g[l]).reshape(B * S, D)
        y = hierarchical_moe(hf, w_route_group[l], b_route_group[l], w_route_expert[l], b_route_expert[l],
                             moe_w_gate[l], moe_w_up[l], moe_w_down[l])
        x = x + y.reshape(B, S, D)

        hp = rms_norm(x, ple_norm_g[l])
        x = x + jax.nn.sigmoid(hp @ w_ple_gate[l]) * (p[l] @ w_ple[l])
    return x
```

```python
import functools
import math

import jax
import jax.numpy as jnp
from jax import lax
from jax.experimental import pallas as pl
from jax.experimental.pallas import tpu as pltpu

F32 = jnp.float32
BF16 = jnp.bfloat16

D_MODEL = 1024
HEAD_DIM = 64
LANES = 128
POOL_WINDOWS = (2, 4, 8, 16)
POOL_HALO = 16
BRANCH_WIDTH = 512
N_BRANCH = 3
FOX_HEADS = 8
ROT_DIM = HEAD_DIM // 4
ROPE_THETA = 500000.0
PLE_DIM = 256
N_GROUPS = 4
EXPERTS_PER_GROUP = 4
N_EXPERTS = 16
D_EXPERT = 256
RMS_EPS = 1e-6
NEG = -1e30

C_POOL, C_DQ, C_DK, C_DV, C_FQ, C_FK, C_FV, C_FZ, C_GZ = (0, 512, 1024, 1536, 2048, 2560, 3072, 3584, 3712)
IN_PACKED = C_GZ + N_BRANCH * D_MODEL

TM_IN = 512
TM_MERGE = 512
TM_MOE = 256
TQ = 256
VMEM_LIMIT = 56 * 1024 * 1024


def _const_spec(shape):
    zeros = (0,) * len(shape)
    return pl.BlockSpec(shape, lambda *_: zeros, pipeline_mode=pl.Buffered(1))


def _rms(x, g):
    return x * lax.rsqrt(jnp.mean(x * x, axis=-1, keepdims=True) + RMS_EPS) * g


def _inproj_kernel(x_ref, g_ref, w_ref, cos_ref, sa_ref, sb_ref, qkg_ref, fb_ref,
                   u_ref, dq_ref, dk_ref, dv_ref, fq_ref, fk_ref, fv_ref, f_ref, ft_ref, gates_ref,
                   carry_ref):
    i = pl.program_id(1)
    tm = x_ref.shape[1]
    h = _rms(x_ref[0], g_ref[...]).astype(BF16)

    def proj(c0, width):
        return jnp.dot(h, w_ref[:, c0:c0 + width], preferred_element_type=F32)

    lo = lax.broadcasted_iota(jnp.int32, (tm, LANES), 1) < HEAD_DIM
    cos, sa, sb = cos_ref[0], sa_ref[0], sb_ref[0]
    scale = HEAD_DIM ** -0.5

    def head_norm(a, g):
        sq = a * a
        s_lo = jnp.sum(jnp.where(lo, sq, 0.0), axis=-1, keepdims=True)
        s_hi = jnp.sum(jnp.where(lo, 0.0, sq), axis=-1, keepdims=True)
        ss = jnp.where(lo, s_lo, s_hi)
        return a * lax.rsqrt(ss * (1.0 / HEAD_DIM) + RMS_EPS) * g

    def rotary(a):
        return a * cos + pltpu.roll(a, LANES - ROT_DIM // 2, 1) * sa + pltpu.roll(a, ROT_DIM // 2, 1) * sb

    def qk_chunk(c0, out_ref, g, rot, mult):
        acc = proj(c0, 512)
        for j in range(4):
            a = head_norm(acc[:, j * LANES:(j + 1) * LANES], g)
            if rot:
                a = rotary(a)
            if mult != 1.0:
                a = a * mult
            out_ref[0, :, j * LANES:(j + 1) * LANES] = a.astype(BF16)

    u_ref[0] = proj(C_POOL, 512).astype(BF16)
    qk_chunk(C_DQ, dq_ref, qkg_ref[0:1, :], True, scale)
    qk_chunk(C_DK, dk_ref, qkg_ref[1:2, :], True, 1.0)
    dv_ref[0] = proj(C_DV, 512).astype(BF16)
    qk_chunk(C_FQ, fq_ref, qkg_ref[2:3, :], False, scale)
    qk_chunk(C_FK, fk_ref, qkg_ref[3:4, :], False, 1.0)
    fv_ref[0] = proj(C_FV, 512).astype(BF16)

    z = proj(C_FZ, LANES) + fb_ref[...]
    logf = jnp.minimum(z, 0.0) - jnp.log1p(jnp.exp(-jnp.abs(z)))

    @pl.when(i == 0)
    def _():
        carry_ref[...] = jnp.zeros_like(carry_ref)

    row = lax.broadcasted_iota(jnp.int32, (tm, tm), 0)
    col = lax.broadcasted_iota(jnp.int32, (tm, tm), 1)
    tri = jnp.where(row >= col, 1.0, 0.0).astype(BF16)
    hi = logf.astype(BF16)
    lo_part = (logf - hi.astype(F32)).astype(BF16)
    fcum = (jnp.dot(tri, hi, preferred_element_type=F32)
            + jnp.dot(tri, lo_part, preferred_element_type=F32)
            + carry_ref[0:1, :])
    carry_ref[...] = jnp.broadcast_to(fcum[tm - 1:tm, :], carry_ref.shape)
    f_ref[0] = fcum
    ft_ref[0] = fcum.T[0:FOX_HEADS, :]

    for c in range(N_BRANCH * D_MODEL // 512):
        gates_ref[0, :, c * 512:(c + 1) * 512] = jax.nn.sigmoid(proj(C_GZ + c * 512, 512)).astype(BF16)


def _inproj(x, g, w, cos, sa, sb, qkg, fb):
    B, S, D = x.shape
    tm = TM_IN
    tok = lambda width: pl.BlockSpec((1, tm, width), lambda b, i: (b, i, 0))
    out_shape = (
        [jax.ShapeDtypeStruct((B, S, 512), BF16)] * 7
        + [jax.ShapeDtypeStruct((B, S, LANES), F32),
           jax.ShapeDtypeStruct((B, FOX_HEADS, S), F32),
           jax.ShapeDtypeStruct((B, S, N_BRANCH * D), BF16)]
    )
    out_specs = (
        [tok(512)] * 7
        + [tok(LANES),
           pl.BlockSpec((1, FOX_HEADS, tm), lambda b, i: (b, 0, i)),
           tok(N_BRANCH * D)]
    )
    return pl.pallas_call(
        _inproj_kernel,
        grid=(B, S // tm),
        in_specs=[tok(D), _const_spec((1, D)), _const_spec((D, IN_PACKED)),
                  tok(LANES), tok(LANES), tok(LANES),
                  _const_spec((4, LANES)), _const_spec((1, LANES))],
        out_specs=out_specs,
        out_shape=out_shape,
        scratch_shapes=[pltpu.VMEM((8, LANES), F32)],
        compiler_params=pltpu.CompilerParams(
            dimension_semantics=("parallel", "arbitrary"), vmem_limit_bytes=VMEM_LIMIT),
        name="inproj",
    )(x, g, w, cos, sa, sb, qkg, fb)


def _nt_dot(a, b):
    return lax.dot_general(a, b, (((1,), (1,)), ((), ())), preferred_element_type=F32)


def _softmax_step(s, m, l):
    m_new = jnp.maximum(m, jnp.max(s, axis=1, keepdims=True))
    alpha = jnp.exp(m - m_new)
    p = jnp.exp(s - m_new)
    l_new = alpha * l + jnp.sum(p, axis=1, keepdims=True)
    return p, m_new, l_new, alpha


def _diff_kernel(lam_init, q_ref, k_ref, v_ref, lam_ref, g_ref, o_ref):
    S = q_ref.shape[1]
    tq = TQ
    lo = lax.broadcasted_iota(jnp.int32, (tq, LANES), 1) < HEAD_DIM
    causal = (lax.broadcasted_iota(jnp.int32, (tq, tq), 1) <= lax.broadcasted_iota(jnp.int32, (tq, tq), 0))
    lp = lam_ref[...]
    lam = (jnp.exp(jnp.sum(lp[0:1, :] * lp[1:2, :], axis=1, keepdims=True))
           - jnp.exp(jnp.sum(lp[2:3, :] * lp[3:4, :], axis=1, keepdims=True)) + lam_init)

    def q_block(qi, _):
        q0 = pl.multiple_of(qi * tq, tq)
        q = q_ref[0, pl.ds(q0, tq), :]
        zero = jnp.zeros_like(q)
        q1 = jnp.where(lo, q, zero)
        q2 = jnp.where(lo, zero, q)

        def kv_step(kj, carry, diag):
            m1, l1, a1, m2, l2, a2 = carry
            k0 = pl.multiple_of(kj * tq, tq)
            k = k_ref[0, pl.ds(k0, tq), :]
            v = v_ref[0, pl.ds(k0, tq), :]
            s1 = _nt_dot(q1, k)
            s2 = _nt_dot(q2, k)
            if diag:
                s1 = jnp.where(causal, s1, NEG)
                s2 = jnp.where(causal, s2, NEG)
            p1, m1, l1, al1 = _softmax_step(s1, m1, l1)
            p2, m2, l2, al2 = _softmax_step(s2, m2, l2)
            a1 = al1 * a1 + jnp.dot(p1.astype(BF16), v, preferred_element_type=F32)
            a2 = al2 * a2 + jnp.dot(p2.astype(BF16), v, preferred_element_type=F32)
            return m1, l1, a1, m2, l2, a2

        col = lambda val: jnp.full((tq, 1), val, F32)
        acc0 = jnp.zeros((tq, LANES), F32)
        init = (col(NEG), col(0.0), acc0, col(NEG), col(0.0), acc0)
        carry = lax.fori_loop(0, qi, functools.partial(kv_step, diag=False), init)
        m1, l1, a1, m2, l2, a2 = kv_step(qi, carry, True)
        o = a1 / l1 - lam * (a2 / l2)
        o = _rms(o, g_ref[...]) * (1.0 - lam_init)
        o_ref[0, pl.ds(q0, tq), :] = o.astype(BF16)
        return 0

    lax.fori_loop(0, S // tq, q_block, 0)


def _diff_attention(dq, dk, dv, lam_p, subln_g, lam_init):
    B, S, _ = dq.shape
    blk = pl.BlockSpec((1, S, LANES), lambda b, h: (b, 0, h))
    return pl.pallas_call(
        functools.partial(_diff_kernel, lam_init),
        grid=(B, 4),
        in_specs=[blk, blk, blk, _const_spec((4, HEAD_DIM)), _const_spec((1, LANES))],
        out_specs=blk,
        out_shape=jax.ShapeDtypeStruct((B, S, 512), BF16),
        compiler_params=pltpu.CompilerParams(
            dimension_semantics=("parallel", "parallel"), vmem_limit_bytes=VMEM_LIMIT),
        name="diff_attn",
    )(dq, dk, dv, lam_p, subln_g)


def _fox_kernel(q_ref, k_ref, v_ref, f_ref, ft_ref, o_ref):
    S = q_ref.shape[1]
    tq = TQ
    j = pl.program_id(1)
    lane = lax.broadcasted_iota(jnp.int32, (tq, LANES), 1)
    lo = lane < HEAD_DIM
    causal = (lax.broadcasted_iota(jnp.int32, (tq, tq), 1) <= lax.broadcasted_iota(jnp.int32, (tq, tq), 0))

    def q_block(qi, _):
        q0 = pl.multiple_of(qi * tq, tq)
        q = q_ref[0, pl.ds(q0, tq), :]
        zero = jnp.zeros_like(q)
        qh = (jnp.where(lo, q, zero), jnp.where(lo, zero, q))
        f_tile = f_ref[0, pl.ds(q0, tq), :]
        f_t = tuple(jnp.sum(jnp.where(lane == 2 * j + hh, f_tile, 0.0), axis=1, keepdims=True)
                    for hh in range(2))

        def kv_step(kj, carry, diag):
            m, l, acc = carry
            k0 = pl.multiple_of(kj * tq, tq)
            k = k_ref[0, pl.ds(k0, tq), :]
            v = v_ref[0, pl.ds(k0, tq), :]
            vzero = jnp.zeros_like(v)
            vh = (jnp.where(lo, v, vzero), jnp.where(lo, vzero, v))
            m_out, l_out, al, pv = [], [], [], []
            for hh in range(2):
                f_s = ft_ref[0, pl.ds(2 * j + hh, 1), pl.ds(k0, tq)]
                s = _nt_dot(qh[hh], k) + (f_t[hh] - f_s)
                if diag:
                    s = jnp.where(causal, s, NEG)
                p, m_new, l_new, alpha = _softmax_step(s, m[hh], l[hh])
                m_out.append(m_new)
                l_out.append(l_new)
                al.append(alpha)
                pv.append(jnp.dot(p.astype(BF16), vh[hh], preferred_element_type=F32))
            acc = jnp.where(lo, al[0], al[1]) * acc + pv[0] + pv[1]
            return tuple(m_out), tuple(l_out), acc

        col = lambda val: jnp.full((tq, 1), val, F32)
        init = ((col(NEG), col(NEG)), (col(0.0), col(0.0)), jnp.zeros((tq, LANES), F32))
        carry = lax.fori_loop(0, qi, functools.partial(kv_step, diag=False), init)
        m, l, acc = kv_step(qi, carry, True)
        o = acc * jnp.where(lo, 1.0 / l[0], 1.0 / l[1])
        o_ref[0, pl.ds(q0, tq), :] = o.astype(BF16)
        return 0

    lax.fori_loop(0, S // tq, q_block, 0)


def _fox_attention(fq, fk, fv, f, ft):
    B, S, _ = fq.shape
    blk = pl.BlockSpec((1, S, LANES), lambda b, j: (b, 0, j))
    return pl.pallas_call(
        _fox_kernel,
        grid=(B, 4),
        in_specs=[blk, blk, blk,
                  pl.BlockSpec((1, S, LANES), lambda b, j: (b, 0, 0)),
                  pl.BlockSpec((1, FOX_HEADS, S), lambda b, j: (b, 0, 0))],
        out_specs=blk,
        out_shape=jax.ShapeDtypeStruct((B, S, 512), BF16),
        compiler_params=pltpu.CompilerParams(
            dimension_semantics=("parallel", "parallel"), vmem_limit_bytes=VMEM_LIMIT),
        name="fox_attn",
    )(fq, fk, fv, f, ft)


def _merge_kernel(x_ref, u_ref, uh_ref, yd_ref, yf_ref, gates_ref, pw_ref, ps_ref, wb_ref, wo_ref,
                  ng_ref, wrh_ref, wrl_ref, br_ref,
                  x1_ref, hf_ref, comb_ref):
    i = pl.program_id(1)
    tm = x_ref.shape[1]

    halo = uh_ref[0].astype(F32) * jnp.where(i > 0, 1.0, 0.0)
    ext = jnp.concatenate([halo, u_ref[0].astype(F32)], axis=0)
    t = i * tm + lax.broadcasted_iota(jnp.int32, (tm, 1), 0)
    ys = []
    for g, w in enumerate(POOL_WINDOWS):
        e = ext[:, g * LANES:(g + 1) * LANES]
        win = e
        step = 1
        while step < w:
            win = win + pltpu.roll(win, step, 0)
            step *= 2
        cnt = jnp.minimum(t + 1, w).astype(F32)
        d = win[POOL_HALO:, :] / cnt - e[POOL_HALO:, :]
        ys.append(jnp.dot(d.astype(BF16), pw_ref[g], preferred_element_type=F32))
    y_pool = jnp.concatenate(ys, axis=1) * ps_ref[...]

    branches = (y_pool.astype(BF16), yd_ref[0], yf_ref[0])
    merged = jnp.zeros((tm, D_MODEL), F32)
    for n in range(N_BRANCH):
        br = jnp.dot(branches[n], wb_ref[n], preferred_element_type=F32)
        merged = merged + gates_ref[0, :, n * D_MODEL:(n + 1) * D_MODEL].astype(F32) * br
    x1 = x_ref[0] + jnp.dot(merged.astype(BF16), wo_ref[...], preferred_element_type=F32)
    x1_ref[0] = x1

    hf = _rms(x1, ng_ref[...])
    hf_hi = hf.astype(BF16)
    hf_lo = (hf - hf_hi.astype(F32)).astype(BF16)
    hf_ref[0] = hf_hi
    lg = (jnp.dot(hf_hi, wrh_ref[...], preferred_element_type=F32)
          + jnp.dot(hf_lo, wrh_ref[...], preferred_element_type=F32)
          + jnp.dot(hf_hi, wrl_ref[...], preferred_element_type=F32)
          + br_ref[...])
    lane = lax.broadcasted_iota(jnp.int32, (tm, LANES), 1)
    lane_f = lane.astype(F32)
    far = float(LANES)
    rmax = lambda a: jnp.max(a, axis=1, keepdims=True)
    rmin = lambda a: jnp.min(a, axis=1, keepdims=True)

    gmask = lane < N_GROUPS
    gl = jnp.where(gmask, lg, NEG)
    gmax = rmax(gl)
    gp = 1.0 / jnp.sum(jnp.where(gmask, jnp.exp(gl - gmax), 0.0), axis=1, keepdims=True)
    gi = rmin(jnp.where(gmask & (gl == gmax), lane_f, far))

    e_idx = lane - N_GROUPS
    e_grp = jnp.right_shift(e_idx, 2).astype(F32)
    emask = (e_idx >= 0) & (e_idx < N_EXPERTS) & (e_grp == gi)
    el = jnp.where(emask, lg, NEG)
    m1 = rmax(el)
    i1 = rmin(jnp.where(emask & (el == m1), lane_f, far))
    rest = emask & (lane_f != i1)
    el2 = jnp.where(rest, lg, NEG)
    m2 = rmax(el2)
    i2 = rmin(jnp.where(rest & (el2 == m2), lane_f, far))
    r = jnp.exp(m2 - m1)
    w1 = gp / (1.0 + r)
    w2 = gp * r / (1.0 + r)
    comb_ref[0] = jnp.where(lane_f == i1, w1, jnp.where(lane_f == i2, w2, 0.0))


def _merge(x, u, yd, yf, gates, pw, ps, wb, wo, ng, wrh, wrl, brt):
    B, S, D = x.shape
    tm = TM_MERGE
    tok = lambda width: pl.BlockSpec((1, tm, width), lambda b, i: (b, i, 0))
    halo_blocks = tm // POOL_HALO
    halo = pl.BlockSpec((1, POOL_HALO, 512), lambda b, i: (b, jnp.maximum(i * halo_blocks - 1, 0), 0))
    return pl.pallas_call(
        _merge_kernel,
        grid=(B, S // tm),
        in_specs=[tok(D), tok(512), halo, tok(512), tok(512), tok(N_BRANCH * D),
                  _const_spec((4, LANES, LANES)), _const_spec((1, 512)),
                  _const_spec((N_BRANCH, BRANCH_WIDTH, D)), _const_spec((D, D)),
                  _const_spec((1, D)), _const_spec((D, LANES)), _const_spec((D, LANES)),
                  _const_spec((1, LANES))],
        out_specs=[tok(D), tok(D), tok(LANES)],
        out_shape=[jax.ShapeDtypeStruct((B, S, D), F32),
                   jax.ShapeDtypeStruct((B, S, D), BF16),
                   jax.ShapeDtypeStruct((B, S, LANES), F32)],
        compiler_params=pltpu.CompilerParams(
            dimension_semantics=("parallel", "parallel"), vmem_limit_bytes=VMEM_LIMIT),
        name="merge",
    )(x, u, u, yd, yf, gates, pw, ps, wb, wo, ng, wrh, wrl, brt)


def _moe_kernel(hf_ref, comb_ref, x1_ref, p_ref, wg_ref, wu_ref, wd_ref, pg_ref, wpg_ref, wp_ref, o_ref):
    hf = hf_ref[...]
    comb = comb_ref[...]
    y = jnp.zeros(o_ref.shape, F32)
    for e in range(N_EXPERTS):
        gate = jnp.dot(hf, wg_ref[e], preferred_element_type=F32)
        up = jnp.dot(hf, wu_ref[e], preferred_element_type=F32)
        act = gate * jax.nn.sigmoid(gate) * up * comb[:, N_GROUPS + e:N_GROUPS + e + 1]
        y = y + jnp.dot(act.astype(BF16), wd_ref[e], preferred_element_type=F32)
    x2 = x1_ref[...] + y
    hp = _rms(x2, pg_ref[...]).astype(BF16)
    gate = jax.nn.sigmoid(jnp.dot(hp, wpg_ref[...], preferred_element_type=F32))
    emb = jnp.dot(p_ref[...].astype(BF16), wp_ref[...], preferred_element_type=F32)
    o_ref[...] = x2 + gate * emb


def _moe(hf, comb, x1, p, wg, wu, wd, pg, wpg, wp):
    T, D = x1.shape
    tm = TM_MOE
    tok = lambda width: pl.BlockSpec((tm, width), lambda i: (i, 0))
    return pl.pallas_call(
        _moe_kernel,
        grid=(T // tm,),
        in_specs=[tok(D), tok(LANES), tok(D), tok(PLE_DIM),
                  _const_spec((N_EXPERTS, D, D_EXPERT)), _const_spec((N_EXPERTS, D, D_EXPERT)),
                  _const_spec((N_EXPERTS, D_EXPERT, D)),
                  _const_spec((1, D)), _const_spec((D, D)), _const_spec((PLE_DIM, D))],
        out_specs=tok(D),
        out_shape=jax.ShapeDtypeStruct((T, D), F32),
        compiler_params=pltpu.CompilerParams(
            dimension_semantics=("parallel",), vmem_limit_bytes=VMEM_LIMIT),
        name="moe",
    )(hf, comb, x1, p, wg, wu, wd, pg, wpg, wp)


def _rotary_tables(positions):
    half = ROT_DIM // 2
    inv_freq = ROPE_THETA ** (-jnp.arange(0, ROT_DIM, 2, dtype=F32) / ROT_DIM)
    ang = positions.astype(F32)[:, :, None] * inv_freq
    c, s = jnp.cos(ang), jnp.sin(ang)
    B, S = positions.shape
    pad = lambda n, val: jnp.full((B, S, n), val, F32)
    cos = jnp.concatenate([c, c, pad(HEAD_DIM - ROT_DIM, 1.0)], axis=-1)
    sa = jnp.concatenate([-s, pad(HEAD_DIM - half, 0.0)], axis=-1)
    sb = jnp.concatenate([pad(half, 0.0), s, pad(HEAD_DIM - ROT_DIM, 0.0)], axis=-1)
    two = lambda a: jnp.concatenate([a, a], axis=-1)
    return two(cos), two(sa), two(sb)


def _pack_w_in(w):
    n_main = C_FZ
    fz = jnp.pad(w[:, n_main:n_main + FOX_HEADS], ((0, 0), (0, LANES - FOX_HEADS)))
    return jnp.concatenate([w[:, :n_main], fz, w[:, n_main + FOX_HEADS:]], axis=1).astype(BF16)


def kernel(x, p, positions, attn_norm_g, w_in, pool_w, pool_scale, diff_qn_g, diff_kn_g, diff_lambda, diff_subln_g, fox_qn_g, fox_kn_g, fox_forget_b, w_branch, w_out, ffn_norm_g, w_route_group, b_route_group, w_route_expert, b_route_expert, moe_w_gate, moe_w_up, moe_w_down, ple_norm_g, w_ple_gate, w_ple):
    B, S, D = x.shape
    depth = w_in.shape[0]
    cos, sa, sb = _rotary_tables(positions)
    two = lambda a: jnp.concatenate([a, a], axis=-1)
    pad_lanes = lambda a: jnp.pad(a, ((0, 0), (0, LANES - a.shape[1])))

    for l in range(depth):
        lam_init = 0.8 - 0.6 * math.exp(-0.3 * l)
        qkg = jnp.stack([two(diff_qn_g[l]), two(diff_kn_g[l]), two(fox_qn_g[l]), two(fox_kn_g[l])])
        fb = pad_lanes(fox_forget_b[l][None, :])
        u, dq, dk, dv, fq, fk, fv, f, ft, gates = _inproj(
            x, attn_norm_g[l][None, :], _pack_w_in(w_in[l]), cos, sa, sb, qkg, fb)

        y_diff = _diff_attention(dq, dk, dv, diff_lambda[l], diff_subln_g[l][None, :], lam_init)
        y_fox = _fox_attention(fq, fk, fv, f, ft)

        w_r = jnp.concatenate([w_route_group[l], w_route_expert[l]], axis=1)
        w_r_hi = w_r.astype(BF16)
        w_r_lo = (w_r - w_r_hi.astype(F32)).astype(BF16)
        b_r = jnp.concatenate([b_route_group[l], b_route_expert[l]])[None, :]
        x1, hf, comb = _merge(
            x, u, y_diff, y_fox, gates,
            pool_w[l].astype(BF16), pool_scale[l][None, :], w_branch[l].astype(BF16), w_out[l].astype(BF16),
            ffn_norm_g[l][None, :], pad_lanes(w_r_hi), pad_lanes(w_r_lo), pad_lanes(b_r))

        x = _moe(hf.reshape(B * S, D), comb.reshape(B * S, LANES), x1.reshape(B * S, D),
                 p[l].reshape(B * S, PLE_DIM),
                 moe_w_gate[l].astype(BF16), moe_w_up[l].astype(BF16), moe_w_down[l].astype(BF16),
                 ple_norm_g[l][None, :], w_ple_gate[l].astype(BF16), w_ple[l].astype(BF16)).reshape(B, S, D)
    return x
```

```python
import functools
import math

import jax
import jax.numpy as jnp
from jax import lax
from jax.experimental import pallas as pl
from jax.experimental.pallas import tpu as pltpu

F32 = jnp.float32
BF16 = jnp.bfloat16

D_MODEL = 1024
HEAD_DIM = 64
LANES = 128
POOL_WINDOWS = (2, 4, 8, 16)
POOL_HALO = 16
BRANCH_WIDTH = 512
N_BRANCH = 3
FOX_HEADS = 8
ROT_DIM = HEAD_DIM // 4
ROPE_THETA = 500000.0
PLE_DIM = 256
N_GROUPS = 4
EXPERTS_PER_GROUP = 4
N_EXPERTS = 16
D_EXPERT = 256
RMS_EPS = 1e-6
NEG = -1e30
LOG2E = math.log2(math.e)

C_POOL, C_DQ, C_DK, C_DV, C_FQ, C_FK, C_FV, C_FZ, C_GZ = (0, 512, 1024, 1536, 2048, 2560, 3072, 3584, 3712)
IN_PACKED = C_GZ + N_BRANCH * D_MODEL

TM_IN = 512
TM_MERGE = 512
TM_MOE = 256
TQ = 256
VMEM_LIMIT = 56 * 1024 * 1024


def _const_spec(shape):
    zeros = (0,) * len(shape)
    return pl.BlockSpec(shape, lambda *_: zeros, pipeline_mode=pl.Buffered(1))


def _rms(x, g):
    return x * lax.rsqrt(jnp.mean(x * x, axis=-1, keepdims=True) + RMS_EPS) * g


def _inproj_kernel(x_ref, g_ref, w_ref, cos_ref, sa_ref, sb_ref, qkg_ref, fb_ref,
                   u_ref, dq_ref, dk_ref, dv_ref, fq_ref, fk_ref, fv_ref, f_ref, ft_ref, gates_ref,
                   carry_ref):
    i = pl.program_id(1)
    tm = x_ref.shape[1]
    h = _rms(x_ref[0], g_ref[...]).astype(BF16)

    def proj(c0, width):
        return jnp.dot(h, w_ref[:, c0:c0 + width], preferred_element_type=F32)

    lo = lax.broadcasted_iota(jnp.int32, (tm, LANES), 1) < HEAD_DIM
    cos, sa, sb = cos_ref[0], sa_ref[0], sb_ref[0]
    scale = HEAD_DIM ** -0.5 * LOG2E

    def head_norm(a, g):
        sq = a * a
        s_lo = jnp.sum(jnp.where(lo, sq, 0.0), axis=-1, keepdims=True)
        s_hi = jnp.sum(jnp.where(lo, 0.0, sq), axis=-1, keepdims=True)
        ss = jnp.where(lo, s_lo, s_hi)
        return a * lax.rsqrt(ss * (1.0 / HEAD_DIM) + RMS_EPS) * g

    def rotary(a):
        return a * cos + pltpu.roll(a, LANES - ROT_DIM // 2, 1) * sa + pltpu.roll(a, ROT_DIM // 2, 1) * sb

    def qk_chunk(c0, out_ref, g, rot, mult):
        acc = proj(c0, 512)
        for j in range(4):
            a = head_norm(acc[:, j * LANES:(j + 1) * LANES], g)
            if rot:
                a = rotary(a)
            if mult != 1.0:
                a = a * mult
            out_ref[0, :, j * LANES:(j + 1) * LANES] = a.astype(BF16)

    u_ref[0] = proj(C_POOL, 512).astype(BF16)
    qk_chunk(C_DQ, dq_ref, qkg_ref[0:1, :], True, scale)
    qk_chunk(C_DK, dk_ref, qkg_ref[1:2, :], True, 1.0)
    dv_ref[0] = proj(C_DV, 512).astype(BF16)
    qk_chunk(C_FQ, fq_ref, qkg_ref[2:3, :], False, scale)
    qk_chunk(C_FK, fk_ref, qkg_ref[3:4, :], False, 1.0)
    fv_ref[0] = proj(C_FV, 512).astype(BF16)

    z = proj(C_FZ, LANES) + fb_ref[...]
    logf = jnp.minimum(z, 0.0) - jnp.log1p(jnp.exp(-jnp.abs(z)))

    @pl.when(i == 0)
    def _():
        carry_ref[...] = jnp.zeros_like(carry_ref)

    row = lax.broadcasted_iota(jnp.int32, (tm, tm), 0)
    col = lax.broadcasted_iota(jnp.int32, (tm, tm), 1)
    tri = jnp.where(row >= col, 1.0, 0.0).astype(BF16)
    hi = logf.astype(BF16)
    lo_part = (logf - hi.astype(F32)).astype(BF16)
    fcum = (jnp.dot(tri, hi, preferred_element_type=F32)
            + jnp.dot(tri, lo_part, preferred_element_type=F32)
            + carry_ref[0:1, :])
    carry_ref[...] = jnp.broadcast_to(fcum[tm - 1:tm, :], carry_ref.shape)
    f_ref[0] = fcum
    ft_ref[0] = fcum.T[0:FOX_HEADS, :]

    for c in range(N_BRANCH * D_MODEL // 512):
        gates_ref[0, :, c * 512:(c + 1) * 512] = jax.nn.sigmoid(proj(C_GZ + c * 512, 512)).astype(BF16)


def _inproj(x, g, w, cos, sa, sb, qkg, fb):
    B, S, D = x.shape
    tm = TM_IN
    tok = lambda width: pl.BlockSpec((1, tm, width), lambda b, i: (b, i, 0))
    out_shape = (
        [jax.ShapeDtypeStruct((B, S, 512), BF16)] * 7
        + [jax.ShapeDtypeStruct((B, S, LANES), F32),
           jax.ShapeDtypeStruct((B, FOX_HEADS, S), F32),
           jax.ShapeDtypeStruct((B, S, N_BRANCH * D), BF16)]
    )
    out_specs = (
        [tok(512)] * 7
        + [tok(LANES),
           pl.BlockSpec((1, FOX_HEADS, tm), lambda b, i: (b, 0, i)),
           tok(N_BRANCH * D)]
    )
    return pl.pallas_call(
        _inproj_kernel,
        grid=(B, S // tm),
        in_specs=[tok(D), _const_spec((1, D)), _const_spec((D, IN_PACKED)),
                  tok(LANES), tok(LANES), tok(LANES),
                  _const_spec((4, LANES)), _const_spec((1, LANES))],
        out_specs=out_specs,
        out_shape=out_shape,
        scratch_shapes=[pltpu.VMEM((8, LANES), F32)],
        compiler_params=pltpu.CompilerParams(
            dimension_semantics=("parallel", "arbitrary"), vmem_limit_bytes=VMEM_LIMIT),
        name="inproj",
    )(x, g, w, cos, sa, sb, qkg, fb)


def _nt_dot(a, b):
    return lax.dot_general(a, b, (((1,), (1,)), ((), ())), preferred_element_type=F32)


def _softmax_rows(score_chunk, n_chunks, s_scr, p_scr):
    tq = TQ
    mt = jnp.full((tq, LANES), NEG, F32)
    for c in range(n_chunks):
        s = score_chunk(c)
        s_scr[:, c * tq:(c + 1) * tq] = s
        for w in range(tq // LANES):
            mt = jnp.maximum(mt, s[:, w * LANES:(w + 1) * LANES])
    m = jnp.max(mt, axis=1, keepdims=True)
    for c in range(n_chunks):
        p_scr[:, c * tq:(c + 1) * tq] = jnp.exp2(s_scr[:, c * tq:(c + 1) * tq] - m).astype(BF16)


def _diff_kernel(lam_init, q_ref, k_ref, v_ref, lam_ref, g_ref, o_ref, s_scr, p_scr, v2_scr):
    S = q_ref.shape[1]
    tq = TQ
    lo = lax.broadcasted_iota(jnp.int32, (tq, LANES), 1) < HEAD_DIM
    causal = (lax.broadcasted_iota(jnp.int32, (tq, tq), 1) <= lax.broadcasted_iota(jnp.int32, (tq, tq), 0))
    lp = lam_ref[...]
    lam = (jnp.exp(jnp.sum(lp[0:1, :] * lp[1:2, :], axis=1, keepdims=True))
           - jnp.exp(jnp.sum(lp[2:3, :] * lp[3:4, :], axis=1, keepdims=True)) + lam_init)
    v2_scr[:, 0:LANES] = v_ref[0]
    v2_scr[:, LANES:2 * LANES] = jnp.ones((S, LANES), BF16)

    for qi in range(S // tq):
        q0, kend = qi * tq, (qi + 1) * tq
        q = q_ref[0, q0:kend, :]
        zero = jnp.zeros_like(q)
        parts = []
        for comp in range(2):
            qm = jnp.where(lo, q, zero) if comp == 0 else jnp.where(lo, zero, q)

            def score_chunk(c, qm=qm, qi=qi):
                s = _nt_dot(qm, k_ref[0, c * tq:(c + 1) * tq, :])
                return jnp.where(causal, s, NEG) if c == qi else s

            _softmax_rows(score_chunk, qi + 1, s_scr.at[comp], p_scr.at[comp])
            pv = jnp.dot(p_scr[comp, :, 0:kend], v2_scr[0:kend, :], preferred_element_type=F32)
            parts.append(pv[:, 0:LANES] / pv[:, LANES:2 * LANES])
        o = parts[0] - lam * parts[1]
        o = _rms(o, g_ref[...]) * (1.0 - lam_init)
        o_ref[0, q0:kend, :] = o.astype(BF16)


def _diff_attention(dq, dk, dv, lam_p, subln_g, lam_init):
    B, S, _ = dq.shape
    blk = pl.BlockSpec((1, S, LANES), lambda b, h: (b, 0, h))
    return pl.pallas_call(
        functools.partial(_diff_kernel, lam_init),
        grid=(B, 4),
        in_specs=[blk, blk, blk, _const_spec((4, HEAD_DIM)), _const_spec((1, LANES))],
        out_specs=blk,
        out_shape=jax.ShapeDtypeStruct((B, S, 512), BF16),
        scratch_shapes=[pltpu.VMEM((2, TQ, S), F32), pltpu.VMEM((2, TQ, S), BF16),
                        pltpu.VMEM((S, 2 * LANES), BF16)],
        compiler_params=pltpu.CompilerParams(
            dimension_semantics=("parallel", "parallel"), vmem_limit_bytes=VMEM_LIMIT),
        name="diff_attn",
    )(dq, dk, dv, lam_p, subln_g)


def _fox_kernel(q_ref, k_ref, v_ref, f_ref, ft_ref, o_ref, s_scr, p_scr, v2_scr):
    S = q_ref.shape[1]
    tq = TQ
    j = pl.program_id(1)
    lane = lax.broadcasted_iota(jnp.int32, (tq, LANES), 1)
    lo = lane < HEAD_DIM
    causal = (lax.broadcasted_iota(jnp.int32, (tq, tq), 1) <= lax.broadcasted_iota(jnp.int32, (tq, tq), 0))
    v = v_ref[0]
    lo_s = lax.broadcasted_iota(jnp.int32, (S, LANES), 1) < HEAD_DIM
    one = jnp.ones_like(v)
    v2_scr[0] = jnp.where(lo_s, v, one)
    v2_scr[1] = jnp.where(lo_s, one, v)

    for qi in range(S // tq):
        q0, kend = qi * tq, (qi + 1) * tq
        q = q_ref[0, q0:kend, :]
        zero = jnp.zeros_like(q)
        f_tile = f_ref[0, q0:kend, :] * LOG2E
        pvs = []
        for hh in range(2):
            qm = jnp.where(lo, q, zero) if hh == 0 else jnp.where(lo, zero, q)
            f_t = jnp.sum(jnp.where(lane == 2 * j + hh, f_tile, 0.0), axis=1, keepdims=True)

            def score_chunk(c, qm=qm, f_t=f_t, hh=hh, qi=qi):
                f_s = ft_ref[0, pl.ds(2 * j + hh, 1), c * tq:(c + 1) * tq] * LOG2E
                s = _nt_dot(qm, k_ref[0, c * tq:(c + 1) * tq, :]) + (f_t - f_s)
                return jnp.where(causal, s, NEG) if c == qi else s

            _softmax_rows(score_chunk, qi + 1, s_scr.at[hh], p_scr.at[hh])
            pvs.append(jnp.dot(p_scr[hh, :, 0:kend], v2_scr[hh, 0:kend, :], preferred_element_type=F32))
        o = jnp.where(lo, pvs[0] / pltpu.roll(pvs[0], HEAD_DIM, 1), pvs[1] / pltpu.roll(pvs[1], HEAD_DIM, 1))
        o_ref[0, q0:kend, :] = o.astype(BF16)


def _fox_attention(fq, fk, fv, f, ft):
    B, S, _ = fq.shape
    blk = pl.BlockSpec((1, S, LANES), lambda b, j: (b, 0, j))
    return pl.pallas_call(
        _fox_kernel,
        grid=(B, 4),
        in_specs=[blk, blk, blk,
                  pl.BlockSpec((1, S, LANES), lambda b, j: (b, 0, 0)),
                  pl.BlockSpec((1, FOX_HEADS, S), lambda b, j: (b, 0, 0))],
        out_specs=blk,
        out_shape=jax.ShapeDtypeStruct((B, S, 512), BF16),
        scratch_shapes=[pltpu.VMEM((2, TQ, S), F32), pltpu.VMEM((2, TQ, S), BF16),
                        pltpu.VMEM((2, S, LANES), BF16)],
        compiler_params=pltpu.CompilerParams(
            dimension_semantics=("parallel", "parallel"), vmem_limit_bytes=VMEM_LIMIT),
        name="fox_attn",
    )(fq, fk, fv, f, ft)


def _merge_kernel(x_ref, u_ref, uh_ref, yd_ref, yf_ref, gates_ref, pw_ref, ps_ref, wb_ref, wo_ref,
                  ng_ref, wrh_ref, wrl_ref, br_ref,
                  x1_ref, hf_ref, comb_ref):
    i = pl.program_id(1)
    tm = x_ref.shape[1]

    halo = uh_ref[0].astype(F32) * jnp.where(i > 0, 1.0, 0.0)
    ext = jnp.concatenate([halo, u_ref[0].astype(F32)], axis=0)
    t = i * tm + lax.broadcasted_iota(jnp.int32, (tm, 1), 0)
    ys = []
    for g, w in enumerate(POOL_WINDOWS):
        e = ext[:, g * LANES:(g + 1) * LANES]
        win = e
        step = 1
        while step < w:
            win = win + pltpu.roll(win, step, 0)
            step *= 2
        cnt = jnp.minimum(t + 1, w).astype(F32)
        d = win[POOL_HALO:, :] / cnt - e[POOL_HALO:, :]
        ys.append(jnp.dot(d.astype(BF16), pw_ref[g], preferred_element_type=F32))
    y_pool = jnp.concatenate(ys, axis=1) * ps_ref[...]

    branches = (y_pool.astype(BF16), yd_ref[0], yf_ref[0])
    merged = jnp.zeros((tm, D_MODEL), F32)
    for n in range(N_BRANCH):
        br = jnp.dot(branches[n], wb_ref[n], preferred_element_type=F32)
        merged = merged + gates_ref[0, :, n * D_MODEL:(n + 1) * D_MODEL].astype(F32) * br
    x1 = x_ref[0] + jnp.dot(merged.astype(BF16), wo_ref[...], preferred_element_type=F32)
    x1_ref[0] = x1

    hf = _rms(x1, ng_ref[...])
    hf_hi = hf.astype(BF16)
    hf_lo = (hf - hf_hi.astype(F32)).astype(BF16)
    hf_ref[0] = hf_hi
    lg = (jnp.dot(hf_hi, wrh_ref[...], preferred_element_type=F32)
          + jnp.dot(hf_lo, wrh_ref[...], preferred_element_type=F32)
          + jnp.dot(hf_hi, wrl_ref[...], preferred_element_type=F32)
          + br_ref[...])
    lane = lax.broadcasted_iota(jnp.int32, (tm, LANES), 1)
    lane_f = lane.astype(F32)
    far = float(LANES)
    rmax = lambda a: jnp.max(a, axis=1, keepdims=True)
    rmin = lambda a: jnp.min(a, axis=1, keepdims=True)

    gmask = lane < N_GROUPS
    gl = jnp.where(gmask, lg, NEG)
    gmax = rmax(gl)
    gp = 1.0 / jnp.sum(jnp.where(gmask, jnp.exp(gl - gmax), 0.0), axis=1, keepdims=True)
    gi = rmin(jnp.where(gmask & (gl == gmax), lane_f, far))

    e_idx = lane - N_GROUPS
    e_grp = jnp.right_shift(e_idx, 2).astype(F32)
    emask = (e_idx >= 0) & (e_idx < N_EXPERTS) & (e_grp == gi)
    el = jnp.where(emask, lg, NEG)
    m1 = rmax(el)
    i1 = rmin(jnp.where(emask & (el == m1), lane_f, far))
    rest = emask & (lane_f != i1)
    el2 = jnp.where(rest, lg, NEG)
    m2 = rmax(el2)
    i2 = rmin(jnp.where(rest & (el2 == m2), lane_f, far))
    r = jnp.exp(m2 - m1)
    w1 = gp / (1.0 + r)
    w2 = gp * r / (1.0 + r)
    comb_ref[0] = jnp.where(lane_f == i1, w1, jnp.where(lane_f == i2, w2, 0.0))


def _merge(x, u, yd, yf, gates, pw, ps, wb, wo, ng, wrh, wrl, brt):
    B, S, D = x.shape
    tm = TM_MERGE
    tok = lambda width: pl.BlockSpec((1, tm, width), lambda b, i: (b, i, 0))
    halo_blocks = tm // POOL_HALO
    halo = pl.BlockSpec((1, POOL_HALO, 512), lambda b, i: (b, jnp.maximum(i * halo_blocks - 1, 0), 0))
    return pl.pallas_call(
        _merge_kernel,
        grid=(B, S // tm),
        in_specs=[tok(D), tok(512), halo, tok(512), tok(512), tok(N_BRANCH * D),
                  _const_spec((4, LANES, LANES)), _const_spec((1, 512)),
                  _const_spec((N_BRANCH, BRANCH_WIDTH, D)), _const_spec((D, D)),
                  _const_spec((1, D)), _const_spec((D, LANES)), _const_spec((D, LANES)),
                  _const_spec((1, LANES))],
        out_specs=[tok(D), tok(D), tok(LANES)],
        out_shape=[jax.ShapeDtypeStruct((B, S, D), F32),
                   jax.ShapeDtypeStruct((B, S, D), BF16),
                   jax.ShapeDtypeStruct((B, S, LANES), F32)],
        compiler_params=pltpu.CompilerParams(
            dimension_semantics=("parallel", "parallel"), vmem_limit_bytes=VMEM_LIMIT),
        name="merge",
    )(x, u, u, yd, yf, gates, pw, ps, wb, wo, ng, wrh, wrl, brt)


def _moe_kernel(hf_ref, comb_ref, x1_ref, p_ref, wg_ref, wu_ref, wd_ref, pg_ref, wpg_ref, wp_ref, o_ref):
    hf = hf_ref[...]
    comb = comb_ref[...]
    y = jnp.zeros(o_ref.shape, F32)
    for e in range(N_EXPERTS):
        gate = jnp.dot(hf, wg_ref[e], preferred_element_type=F32)
        up = jnp.dot(hf, wu_ref[e], preferred_element_type=F32)
        act = gate * jax.nn.sigmoid(gate) * up * comb[:, N_GROUPS + e:N_GROUPS + e + 1]
        y = y + jnp.dot(act.astype(BF16), wd_ref[e], preferred_element_type=F32)
    x2 = x1_ref[...] + y
    hp = _rms(x2, pg_ref[...]).astype(BF16)
    gate = jax.nn.sigmoid(jnp.dot(hp, wpg_ref[...], preferred_element_type=F32))
    emb = jnp.dot(p_ref[...].astype(BF16), wp_ref[...], preferred_element_type=F32)
    o_ref[...] = x2 + gate * emb


def _moe(hf, comb, x1, p, wg, wu, wd, pg, wpg, wp):
    T, D = x1.shape
    tm = TM_MOE
    tok = lambda width: pl.BlockSpec((tm, width), lambda i: (i, 0))
    return pl.pallas_call(
        _moe_kernel,
        grid=(T // tm,),
        in_specs=[tok(D), tok(LANES), tok(D), tok(PLE_DIM),
                  _const_spec((N_EXPERTS, D, D_EXPERT)), _const_spec((N_EXPERTS, D, D_EXPERT)),
                  _const_spec((N_EXPERTS, D_EXPERT, D)),
                  _const_spec((1, D)), _const_spec((D, D)), _const_spec((PLE_DIM, D))],
        out_specs=tok(D),
        out_shape=jax.ShapeDtypeStruct((T, D), F32),
        compiler_params=pltpu.CompilerParams(
            dimension_semantics=("parallel",), vmem_limit_bytes=VMEM_LIMIT),
        name="moe",
    )(hf, comb, x1, p, wg, wu, wd, pg, wpg, wp)


def _rotary_tables(positions):
    half = ROT_DIM // 2
    inv_freq = ROPE_THETA ** (-jnp.arange(0, ROT_DIM, 2, dtype=F32) / ROT_DIM)
    ang = positions.astype(F32)[:, :, None] * inv_freq
    c, s = jnp.cos(ang), jnp.sin(ang)
    B, S = positions.shape
    pad = lambda n, val: jnp.full((B, S, n), val, F32)
    cos = jnp.concatenate([c, c, pad(HEAD_DIM - ROT_DIM, 1.0)], axis=-1)
    sa = jnp.concatenate([-s, pad(HEAD_DIM - half, 0.0)], axis=-1)
    sb = jnp.concatenate([pad(half, 0.0), s, pad(HEAD_DIM - ROT_DIM, 0.0)], axis=-1)
    two = lambda a: jnp.concatenate([a, a], axis=-1)
    return two(cos), two(sa), two(sb)


def _pack_w_in(w):
    n_main = C_FZ
    fz = jnp.pad(w[:, n_main:n_main + FOX_HEADS], ((0, 0), (0, LANES - FOX_HEADS)))
    return jnp.concatenate([w[:, :n_main], fz, w[:, n_main + FOX_HEADS:]], axis=1).astype(BF16)


def kernel(x, p, positions, attn_norm_g, w_in, pool_w, pool_scale, diff_qn_g, diff_kn_g, diff_lambda, diff_subln_g, fox_qn_g, fox_kn_g, fox_forget_b, w_branch, w_out, ffn_norm_g, w_route_group, b_route_group, w_route_expert, b_route_expert, moe_w_gate, moe_w_up, moe_w_down, ple_norm_g, w_ple_gate, w_ple):
    B, S, D = x.shape
    depth = w_in.shape[0]
    cos, sa, sb = _rotary_tables(positions)
    two = lambda a: jnp.concatenate([a, a], axis=-1)
    pad_lanes = lambda a: jnp.pad(a, ((0, 0), (0, LANES - a.shape[1])))

    for l in range(depth):
        lam_init = 0.8 - 0.6 * math.exp(-0.3 * l)
        qkg = jnp.stack([two(diff_qn_g[l]), two(diff_kn_g[l]), two(fox_qn_g[l]), two(fox_kn_g[l])])
        fb = pad_lanes(fox_forget_b[l][None, :])
        u, dq, dk, dv, fq, fk, fv, f, ft, gates = _inproj(
            x, attn_norm_g[l][None, :], _pack_w_in(w_in[l]), cos, sa, sb, qkg, fb)

        y_diff = _diff_attention(dq, dk, dv, diff_lambda[l], diff_subln_g[l][None, :], lam_init)
        y_fox = _fox_attention(fq, fk, fv, f, ft)

        w_r = jnp.concatenate([w_route_group[l], w_route_expert[l]], axis=1)
        w_r_hi = w_r.astype(BF16)
        w_r_lo = (w_r - w_r_hi.astype(F32)).astype(BF16)
        b_r = jnp.concatenate([b_route_group[l], b_route_expert[l]])[None, :]
        x1, hf, comb = _merge(
            x, u, y_diff, y_fox, gates,
            pool_w[l].astype(BF16), pool_scale[l][None, :], w_branch[l].astype(BF16), w_out[l].astype(BF16),
            ffn_norm_g[l][None, :], pad_lanes(w_r_hi), pad_lanes(w_r_lo), pad_lanes(b_r))

        x = _moe(hf.reshape(B * S, D), comb.reshape(B * S, LANES), x1.reshape(B * S, D),
                 p[l].reshape(B * S, PLE_DIM),
                 moe_w_gate[l].astype(BF16), moe_w_up[l].astype(BF16), moe_w_down[l].astype(BF16),
                 ple_norm_g[l][None, :], w_ple_gate[l].astype(BF16), w_ple[l].astype(BF16)).reshape(B, S, D)
    return x
```

```python
import functools
import math

import jax
import jax.numpy as jnp
from jax import lax
from jax.experimental import pallas as pl
from jax.experimental.pallas import tpu as pltpu

F32 = jnp.float32
BF16 = jnp.bfloat16

D_MODEL = 1024
HEAD_DIM = 64
LANES = 128
POOL_WINDOWS = (2, 4, 8, 16)
POOL_HALO = 16
BRANCH_WIDTH = 512
N_BRANCH = 3
FOX_HEADS = 8
ROT_DIM = HEAD_DIM // 4
ROPE_THETA = 500000.0
PLE_DIM = 256
N_GROUPS = 4
EXPERTS_PER_GROUP = 4
N_EXPERTS = 16
D_EXPERT = 256
RMS_EPS = 1e-6
NEG = -1e30
LOG2E = math.log2(math.e)

C_POOL, C_DQ, C_DK, C_DV, C_FQ, C_FK, C_FV, C_FZ, C_GZ = (0, 512, 1024, 1536, 2048, 2560, 3072, 3584, 3712)
IN_PACKED = C_GZ + N_BRANCH * D_MODEL

TM_IN = 512
TM_MERGE = 512
TM_MOE = 512
TM_PLE = 512
GROUP_HIDDEN = EXPERTS_PER_GROUP * D_EXPERT
TQ = 256
VMEM_LIMIT = 56 * 1024 * 1024


def _const_spec(shape):
    zeros = (0,) * len(shape)
    return pl.BlockSpec(shape, lambda *_: zeros, pipeline_mode=pl.Buffered(1))


def _rms(x, g):
    return x * lax.rsqrt(jnp.mean(x * x, axis=-1, keepdims=True) + RMS_EPS) * g


def _router_logits(hf, wrh_ref, wrl_ref, br_ref):
    hi = hf.astype(BF16)
    lo = (hf - hi.astype(F32)).astype(BF16)
    return (jnp.dot(hi, wrh_ref[...], preferred_element_type=F32)
            + jnp.dot(lo, wrh_ref[...], preferred_element_type=F32)
            + jnp.dot(hi, wrl_ref[...], preferred_element_type=F32)
            + br_ref[...])


def _inproj_kernel(x_ref, g_ref, w_ref, cos_ref, sa_ref, sb_ref, qkg_ref, fb_ref,
                   u_ref, dq_ref, dk_ref, dv_ref, fq_ref, fk_ref, fv_ref, f_ref, ft_ref, gates_ref,
                   carry_ref):
    i = pl.program_id(1)
    tm = x_ref.shape[1]
    h = _rms(x_ref[0], g_ref[...]).astype(BF16)

    def proj(c0, width):
        return jnp.dot(h, w_ref[:, c0:c0 + width], preferred_element_type=F32)

    lo = lax.broadcasted_iota(jnp.int32, (tm, LANES), 1) < HEAD_DIM
    cos, sa, sb = cos_ref[0], sa_ref[0], sb_ref[0]
    scale = HEAD_DIM ** -0.5 * LOG2E

    def head_norm(a, g):
        sq = a * a
        s_lo = jnp.sum(jnp.where(lo, sq, 0.0), axis=-1, keepdims=True)
        s_hi = jnp.sum(jnp.where(lo, 0.0, sq), axis=-1, keepdims=True)
        ss = jnp.where(lo, s_lo, s_hi)
        return a * lax.rsqrt(ss * (1.0 / HEAD_DIM) + RMS_EPS) * g

    def rotary(a):
        return a * cos + pltpu.roll(a, LANES - ROT_DIM // 2, 1) * sa + pltpu.roll(a, ROT_DIM // 2, 1) * sb

    def qk_chunk(c0, out_ref, g, rot, mult):
        acc = proj(c0, 512)
        for j in range(4):
            a = head_norm(acc[:, j * LANES:(j + 1) * LANES], g)
            if rot:
                a = rotary(a)
            if mult != 1.0:
                a = a * mult
            out_ref[0, :, j * LANES:(j + 1) * LANES] = a.astype(BF16)

    u_ref[0] = proj(C_POOL, 512).astype(BF16)
    qk_chunk(C_DQ, dq_ref, qkg_ref[0:1, :], True, scale)
    qk_chunk(C_DK, dk_ref, qkg_ref[1:2, :], True, 1.0)
    dv_ref[0] = proj(C_DV, 512).astype(BF16)
    qk_chunk(C_FQ, fq_ref, qkg_ref[2:3, :], False, scale)
    qk_chunk(C_FK, fk_ref, qkg_ref[3:4, :], False, 1.0)
    fv_ref[0] = proj(C_FV, 512).astype(BF16)

    z = proj(C_FZ, LANES) + fb_ref[...]
    logf = jnp.minimum(z, 0.0) - jnp.log1p(jnp.exp(-jnp.abs(z)))

    @pl.when(i == 0)
    def _():
        carry_ref[...] = jnp.zeros_like(carry_ref)

    row = lax.broadcasted_iota(jnp.int32, (tm, tm), 0)
    col = lax.broadcasted_iota(jnp.int32, (tm, tm), 1)
    tri = jnp.where(row >= col, 1.0, 0.0).astype(BF16)
    hi = logf.astype(BF16)
    lo_part = (logf - hi.astype(F32)).astype(BF16)
    fcum = (jnp.dot(tri, hi, preferred_element_type=F32)
            + jnp.dot(tri, lo_part, preferred_element_type=F32)
            + carry_ref[0:1, :])
    carry_ref[...] = jnp.broadcast_to(fcum[tm - 1:tm, :], carry_ref.shape)
    f_ref[0] = fcum
    ft_ref[0] = fcum.T[0:FOX_HEADS, :]

    for c in range(N_BRANCH * D_MODEL // 512):
        gates_ref[0, :, c * 512:(c + 1) * 512] = jax.nn.sigmoid(proj(C_GZ + c * 512, 512)).astype(BF16)


def _inproj(x, g, w, cos, sa, sb, qkg, fb):
    B, S, D = x.shape
    tm = TM_IN
    tok = lambda width: pl.BlockSpec((1, tm, width), lambda b, i: (b, i, 0))
    out_shape = (
        [jax.ShapeDtypeStruct((B, S, 512), BF16)] * 7
        + [jax.ShapeDtypeStruct((B, S, LANES), F32),
           jax.ShapeDtypeStruct((B, FOX_HEADS, S), F32),
           jax.ShapeDtypeStruct((B, S, N_BRANCH * D), BF16)]
    )
    out_specs = (
        [tok(512)] * 7
        + [tok(LANES),
           pl.BlockSpec((1, FOX_HEADS, tm), lambda b, i: (b, 0, i)),
           tok(N_BRANCH * D)]
    )
    return pl.pallas_call(
        _inproj_kernel,
        grid=(B, S // tm),
        in_specs=[tok(D), _const_spec((1, D)), _const_spec((D, IN_PACKED)),
                  tok(LANES), tok(LANES), tok(LANES),
                  _const_spec((4, LANES)), _const_spec((1, LANES))],
        out_specs=out_specs,
        out_shape=out_shape,
        scratch_shapes=[pltpu.VMEM((8, LANES), F32)],
        compiler_params=pltpu.CompilerParams(
            dimension_semantics=("parallel", "arbitrary"), vmem_limit_bytes=VMEM_LIMIT),
        name="inproj",
    )(x, g, w, cos, sa, sb, qkg, fb)


def _nt_dot(a, b):
    return lax.dot_general(a, b, (((1,), (1,)), ((), ())), preferred_element_type=F32)


def _softmax_rows(score_chunk, n_chunks, s_scr, p_scr):
    tq = TQ
    mt = jnp.full((tq, LANES), NEG, F32)
    for c in range(n_chunks):
        s = score_chunk(c)
        s_scr[:, c * tq:(c + 1) * tq] = s
        for w in range(tq // LANES):
            mt = jnp.maximum(mt, s[:, w * LANES:(w + 1) * LANES])
    m = jnp.max(mt, axis=1, keepdims=True)
    for c in range(n_chunks):
        p_scr[:, c * tq:(c + 1) * tq] = jnp.exp2(s_scr[:, c * tq:(c + 1) * tq] - m).astype(BF16)


def _diff_kernel(lam_init, q_ref, k_ref, v_ref, lam_ref, g_ref, o_ref, s_scr, p_scr, v2_scr):
    S = q_ref.shape[1]
    tq = TQ
    lo = lax.broadcasted_iota(jnp.int32, (tq, LANES), 1) < HEAD_DIM
    causal = (lax.broadcasted_iota(jnp.int32, (tq, tq), 1) <= lax.broadcasted_iota(jnp.int32, (tq, tq), 0))
    lp = lam_ref[...]
    lam = (jnp.exp(jnp.sum(lp[0:1, :] * lp[1:2, :], axis=1, keepdims=True))
           - jnp.exp(jnp.sum(lp[2:3, :] * lp[3:4, :], axis=1, keepdims=True)) + lam_init)
    v2_scr[:, 0:LANES] = v_ref[0]
    v2_scr[:, LANES:2 * LANES] = jnp.ones((S, LANES), BF16)

    for qi in range(S // tq):
        q0, kend = qi * tq, (qi + 1) * tq
        q = q_ref[0, q0:kend, :]
        zero = jnp.zeros_like(q)
        parts = []
        for comp in range(2):
            qm = jnp.where(lo, q, zero) if comp == 0 else jnp.where(lo, zero, q)

            def score_chunk(c, qm=qm, qi=qi):
                s = _nt_dot(qm, k_ref[0, c * tq:(c + 1) * tq, :])
                return jnp.where(causal, s, NEG) if c == qi else s

            _softmax_rows(score_chunk, qi + 1, s_scr.at[comp], p_scr.at[comp])
            pv = jnp.dot(p_scr[comp, :, 0:kend], v2_scr[0:kend, :], preferred_element_type=F32)
            parts.append(pv[:, 0:LANES] / pv[:, LANES:2 * LANES])
        o = parts[0] - lam * parts[1]
        o = _rms(o, g_ref[...]) * (1.0 - lam_init)
        o_ref[0, q0:kend, :] = o.astype(BF16)


def _diff_attention(dq, dk, dv, lam_p, subln_g, lam_init):
    B, S, _ = dq.shape
    blk = pl.BlockSpec((1, S, LANES), lambda b, h: (b, 0, h))
    return pl.pallas_call(
        functools.partial(_diff_kernel, lam_init),
        grid=(B, 4),
        in_specs=[blk, blk, blk, _const_spec((4, HEAD_DIM)), _const_spec((1, LANES))],
        out_specs=blk,
        out_shape=jax.ShapeDtypeStruct((B, S, 512), BF16),
        scratch_shapes=[pltpu.VMEM((2, TQ, S), F32), pltpu.VMEM((2, TQ, S), BF16),
                        pltpu.VMEM((S, 2 * LANES), BF16)],
        compiler_params=pltpu.CompilerParams(
            dimension_semantics=("parallel", "parallel"), vmem_limit_bytes=VMEM_LIMIT),
        name="diff_attn",
    )(dq, dk, dv, lam_p, subln_g)


def _fox_kernel(q_ref, k_ref, v_ref, f_ref, ft_ref, o_ref, s_scr, p_scr, v2_scr):
    S = q_ref.shape[1]
    tq = TQ
    j = pl.program_id(1)
    lane = lax.broadcasted_iota(jnp.int32, (tq, LANES), 1)
    lo = lane < HEAD_DIM
    causal = (lax.broadcasted_iota(jnp.int32, (tq, tq), 1) <= lax.broadcasted_iota(jnp.int32, (tq, tq), 0))
    v = v_ref[0]
    lo_s = lax.broadcasted_iota(jnp.int32, (S, LANES), 1) < HEAD_DIM
    one = jnp.ones_like(v)
    v2_scr[0] = jnp.where(lo_s, v, one)
    v2_scr[1] = jnp.where(lo_s, one, v)

    for qi in range(S // tq):
        q0, kend = qi * tq, (qi + 1) * tq
        q = q_ref[0, q0:kend, :]
        zero = jnp.zeros_like(q)
        f_tile = f_ref[0, q0:kend, :] * LOG2E
        pvs = []
        for hh in range(2):
            qm = jnp.where(lo, q, zero) if hh == 0 else jnp.where(lo, zero, q)
            f_t = jnp.sum(jnp.where(lane == 2 * j + hh, f_tile, 0.0), axis=1, keepdims=True)

            def score_chunk(c, qm=qm, f_t=f_t, hh=hh, qi=qi):
                f_s = ft_ref[0, pl.ds(2 * j + hh, 1), c * tq:(c + 1) * tq] * LOG2E
                s = _nt_dot(qm, k_ref[0, c * tq:(c + 1) * tq, :]) + (f_t - f_s)
                return jnp.where(causal, s, NEG) if c == qi else s

            _softmax_rows(score_chunk, qi + 1, s_scr.at[hh], p_scr.at[hh])
            pvs.append(jnp.dot(p_scr[hh, :, 0:kend], v2_scr[hh, 0:kend, :], preferred_element_type=F32))
        o = jnp.where(lo, pvs[0] / pltpu.roll(pvs[0], HEAD_DIM, 1), pvs[1] / pltpu.roll(pvs[1], HEAD_DIM, 1))
        o_ref[0, q0:kend, :] = o.astype(BF16)


def _fox_attention(fq, fk, fv, f, ft):
    B, S, _ = fq.shape
    blk = pl.BlockSpec((1, S, LANES), lambda b, j: (b, 0, j))
    return pl.pallas_call(
        _fox_kernel,
        grid=(B, 4),
        in_specs=[blk, blk, blk,
                  pl.BlockSpec((1, S, LANES), lambda b, j: (b, 0, 0)),
                  pl.BlockSpec((1, FOX_HEADS, S), lambda b, j: (b, 0, 0))],
        out_specs=blk,
        out_shape=jax.ShapeDtypeStruct((B, S, 512), BF16),
        scratch_shapes=[pltpu.VMEM((2, TQ, S), F32), pltpu.VMEM((2, TQ, S), BF16),
                        pltpu.VMEM((2, S, LANES), BF16)],
        compiler_params=pltpu.CompilerParams(
            dimension_semantics=("parallel", "parallel"), vmem_limit_bytes=VMEM_LIMIT),
        name="fox_attn",
    )(fq, fk, fv, f, ft)


def _merge_kernel(x_ref, u_ref, uh_ref, yd_ref, yf_ref, gates_ref, pw_ref, ps_ref, wb_ref, wo_ref,
                  ng_ref, wrh_ref, wrl_ref, br_ref,
                  x1_ref, hf_ref, info_ref, count_ref):
    i = pl.program_id(1)
    tm = x_ref.shape[1]

    halo = uh_ref[0].astype(F32) * jnp.where(i > 0, 1.0, 0.0)
    ext = jnp.concatenate([halo, u_ref[0].astype(F32)], axis=0)
    t = i * tm + lax.broadcasted_iota(jnp.int32, (tm, 1), 0)
    ys = []
    for g, w in enumerate(POOL_WINDOWS):
        e = ext[:, g * LANES:(g + 1) * LANES]
        win = e
        step = 1
        while step < w:
            win = win + pltpu.roll(win, step, 0)
            step *= 2
        cnt = jnp.minimum(t + 1, w).astype(F32)
        d = win[POOL_HALO:, :] / cnt - e[POOL_HALO:, :]
        ys.append(jnp.dot(d.astype(BF16), pw_ref[g], preferred_element_type=F32))
    y_pool = jnp.concatenate(ys, axis=1) * ps_ref[...]

    branches = (y_pool.astype(BF16), yd_ref[0], yf_ref[0])
    merged = jnp.zeros((tm, D_MODEL), F32)
    for n in range(N_BRANCH):
        br = jnp.dot(branches[n], wb_ref[n], preferred_element_type=F32)
        merged = merged + gates_ref[0, :, n * D_MODEL:(n + 1) * D_MODEL].astype(F32) * br
    x1 = x_ref[0] + jnp.dot(merged.astype(BF16), wo_ref[...], preferred_element_type=F32)
    x1_ref[0] = x1

    hf = _rms(x1, ng_ref[...])
    hf_ref[0] = hf
    lg = _router_logits(hf, wrh_ref, wrl_ref, br_ref)
    lane = lax.broadcasted_iota(jnp.int32, (tm, LANES), 1)
    lane_f = lane.astype(F32)
    gmask = lane < N_GROUPS
    gl = jnp.where(gmask, lg, NEG)
    gmax = jnp.max(gl, axis=1, keepdims=True)
    gi = jnp.min(jnp.where(gmask & (gl == gmax), lane_f, float(LANES)), axis=1, keepdims=True)

    @pl.when((pl.program_id(0) == 0) & (i == 0))
    def _():
        count_ref[...] = jnp.zeros_like(count_ref)

    onehot = jnp.where(lane_f == gi, 1.0, 0.0)
    row = lax.broadcasted_iota(jnp.int32, (tm, tm), 0)
    col = lax.broadcasted_iota(jnp.int32, (tm, tm), 1)
    tri = jnp.where(row >= col, 1.0, 0.0).astype(BF16)
    incl = jnp.dot(tri, onehot.astype(BF16), preferred_element_type=F32)
    before = count_ref[0:1, :]
    rank = jnp.sum(onehot * (incl - onehot + before), axis=1, keepdims=True)
    count_ref[...] = jnp.broadcast_to(before + incl[tm - 1:tm, :], count_ref.shape)
    info_ref[0] = jnp.where(lane == 0, gi, jnp.where(lane == 1, rank, 0.0))


def _merge(x, u, yd, yf, gates, pw, ps, wb, wo, ng, wrh, wrl, brt):
    B, S, D = x.shape
    tm = TM_MERGE
    tok = lambda width: pl.BlockSpec((1, tm, width), lambda b, i: (b, i, 0))
    halo_blocks = tm // POOL_HALO
    halo = pl.BlockSpec((1, POOL_HALO, 512), lambda b, i: (b, jnp.maximum(i * halo_blocks - 1, 0), 0))
    return pl.pallas_call(
        _merge_kernel,
        grid=(B, S // tm),
        in_specs=[tok(D), tok(512), halo, tok(512), tok(512), tok(N_BRANCH * D),
                  _const_spec((4, LANES, LANES)), _const_spec((1, 512)),
                  _const_spec((N_BRANCH, BRANCH_WIDTH, D)), _const_spec((D, D)),
                  _const_spec((1, D)), _const_spec((D, LANES)), _const_spec((D, LANES)),
                  _const_spec((1, LANES))],
        out_specs=[tok(D), tok(D), tok(LANES), pl.BlockSpec((8, LANES), lambda b, i: (0, 0))],
        out_shape=[jax.ShapeDtypeStruct((B, S, D), F32),
                   jax.ShapeDtypeStruct((B, S, D), F32),
                   jax.ShapeDtypeStruct((B, S, LANES), F32),
                   jax.ShapeDtypeStruct((8, LANES), F32)],
        compiler_params=pltpu.CompilerParams(
            dimension_semantics=("arbitrary", "arbitrary"), vmem_limit_bytes=VMEM_LIMIT),
        name="merge",
    )(x, u, u, yd, yf, gates, pw, ps, wb, wo, ng, wrh, wrl, brt)


def _moe_kernel(gid_ref, nused_ref, src_ref, src_next_ref, dst_ref,
                hf_hbm, wg_ref, wu_ref, wd_ref, wrh_ref, wrl_ref, br_ref,
                y_hbm,
                xbuf, ybuf, wgu_bf, wd_bf, sem_g, sem_s):
    i = pl.program_id(0)
    n_used = nused_ref[0]
    tm = TM_MOE
    slot = lax.rem(i, 2)

    def start_gather(idx_ref, to_slot):
        def body(r, c):
            pltpu.make_async_copy(hf_hbm.at[pl.ds(idx_ref[0, 0, r], 1)], xbuf.at[to_slot, pl.ds(r, 1)],
                                  sem_g.at[to_slot]).start()
            return c
        lax.fori_loop(0, tm, body, 0, unroll=8)

    def wait_gather(of_slot):
        pltpu.make_async_copy(hf_hbm.at[pl.ds(0, tm)], xbuf.at[of_slot], sem_g.at[of_slot]).wait()

    def wait_scatter(of_slot):
        pltpu.make_async_copy(ybuf.at[of_slot], y_hbm.at[pl.ds(0, tm)], sem_s.at[of_slot]).wait()

    @pl.when(i < n_used)
    def _():
        g = gid_ref[i]

        @pl.when(i == 0)
        def _():
            start_gather(src_ref, slot)
            ybuf[1] = jnp.zeros((tm, D_MODEL), F32)
            n_real = hf_hbm.shape[0]
            spare = [pltpu.make_async_copy(ybuf.at[1], y_hbm.at[pl.ds(n_real + k * tm, tm)], sem_s.at[1])
                     for k in range(N_GROUPS)]
            for cp in spare:
                cp.start()
            for cp in spare:
                cp.wait()

        @pl.when(i + 1 < n_used)
        def _():
            start_gather(src_next_ref, 1 - slot)

        @pl.when((i == 0) | (g != gid_ref[jnp.maximum(i - 1, 0)]))
        def _():
            for e in range(EXPERTS_PER_GROUP):
                wgu_bf[:, e * D_EXPERT:(e + 1) * D_EXPERT] = wg_ref[e].astype(BF16)
                wgu_bf[:, GROUP_HIDDEN + e * D_EXPERT:GROUP_HIDDEN + (e + 1) * D_EXPERT] = wu_ref[e].astype(BF16)
                wd_bf[e * D_EXPERT:(e + 1) * D_EXPERT, :] = wd_ref[e].astype(BF16)

        wait_gather(slot)
        hf = xbuf[slot]

        lg = _router_logits(hf, wrh_ref, wrl_ref, br_ref)
        lane = lax.broadcasted_iota(jnp.int32, (tm, LANES), 1)
        lane_f = lane.astype(F32)
        far = float(LANES)
        rmax = lambda a: jnp.max(a, axis=1, keepdims=True)
        rmin = lambda a: jnp.min(a, axis=1, keepdims=True)
        rsum = lambda a: jnp.sum(a, axis=1, keepdims=True)
        lg_g = rsum(jnp.where(lane == g, lg, 0.0))
        gp = 1.0 / rsum(jnp.where(lane < N_GROUPS, jnp.exp(lg - lg_g), 0.0))
        e_lo = N_GROUPS + EXPERTS_PER_GROUP * g
        emask = (lane >= e_lo) & (lane < e_lo + EXPERTS_PER_GROUP)
        el = jnp.where(emask, lg, NEG)
        m1 = rmax(el)
        i1 = rmin(jnp.where(emask & (el == m1), lane_f, far))
        rest = emask & (lane_f != i1)
        el2 = jnp.where(rest, lg, NEG)
        m2 = rmax(el2)
        i2 = rmin(jnp.where(rest & (el2 == m2), lane_f, far))
        r = jnp.exp(m2 - m1)
        comb = jnp.where(lane_f == i1, gp / (1.0 + r), jnp.where(lane_f == i2, gp * r / (1.0 + r), 0.0))
        cw = jnp.concatenate(
            [jnp.broadcast_to(rsum(jnp.where(lane == e_lo + e, comb, 0.0)), (tm, D_EXPERT))
             for e in range(EXPERTS_PER_GROUP)], axis=1)

        gu = jnp.dot(hf.astype(BF16), wgu_bf[...], preferred_element_type=F32)
        gate, up = gu[:, 0:GROUP_HIDDEN], gu[:, GROUP_HIDDEN:2 * GROUP_HIDDEN]
        act = gate * jax.nn.sigmoid(gate) * up * cw
        y = jnp.dot(act.astype(BF16), wd_bf[...], preferred_element_type=F32)

        @pl.when(i >= 2)
        def _():
            wait_scatter(slot)

        ybuf[slot] = y

        def scatter_row(r, c):
            pltpu.make_async_copy(ybuf.at[slot, pl.ds(r, 1)], y_hbm.at[pl.ds(dst_ref[0, 0, r], 1)],
                                  sem_s.at[slot]).start()
            return c
        lax.fori_loop(0, tm, scatter_row, 0, unroll=8)

        @pl.when(i == n_used - 1)
        def _():
            wait_scatter(slot)

            @pl.when(i >= 1)
            def _():
                wait_scatter(1 - slot)


def _moe(gid, n_used, src, dst, hf, wg, wu, wd, wrh, wrl, brt):
    T, D = hf.shape
    tm = TM_MOE
    n_tiles = src.shape[0]
    idx_spec = lambda shift: pl.BlockSpec(
        (1, 1, tm), lambda i, gid, nu: (jnp.minimum(i + shift, n_tiles - 1), 0, 0), memory_space=pltpu.SMEM)
    grp = lambda shape: pl.BlockSpec(shape, lambda i, gid, nu: (gid[i], 0, 0))
    const = lambda shape: pl.BlockSpec(shape, lambda i, gid, nu: (0,) * len(shape), pipeline_mode=pl.Buffered(1))
    return pl.pallas_call(
        _moe_kernel,
        grid_spec=pltpu.PrefetchScalarGridSpec(
            num_scalar_prefetch=2,
            grid=(n_tiles,),
            in_specs=[idx_spec(0), idx_spec(1), idx_spec(0),
                      pl.BlockSpec(memory_space=pl.ANY),
                      grp((EXPERTS_PER_GROUP, D, D_EXPERT)), grp((EXPERTS_PER_GROUP, D, D_EXPERT)),
                      grp((EXPERTS_PER_GROUP, D_EXPERT, D)),
                      const((D, LANES)), const((D, LANES)), const((1, LANES))],
            out_specs=pl.BlockSpec(memory_space=pl.ANY),
            scratch_shapes=[pltpu.VMEM((2, tm, D), F32), pltpu.VMEM((2, tm, D), F32),
                            pltpu.VMEM((D, 2 * GROUP_HIDDEN), BF16), pltpu.VMEM((GROUP_HIDDEN, D), BF16),
                            pltpu.SemaphoreType.DMA((2,)), pltpu.SemaphoreType.DMA((2,))]),
        out_shape=jax.ShapeDtypeStruct((n_tiles * tm, D), F32),
        compiler_params=pltpu.CompilerParams(
            dimension_semantics=("arbitrary",), vmem_limit_bytes=VMEM_LIMIT),
        name="moe",
    )(gid, n_used, src, src, dst, hf, wg, wu, wd, wrh, wrl, brt)


def _ple_kernel(x1_ref, y_ref, p_ref, pg_ref, wpg_ref, wp_ref, o_ref):
    x2 = x1_ref[...] + y_ref[...]
    hp = _rms(x2, pg_ref[...]).astype(BF16)
    gate = jax.nn.sigmoid(jnp.dot(hp, wpg_ref[...], preferred_element_type=F32))
    emb = jnp.dot(p_ref[...].astype(BF16), wp_ref[...], preferred_element_type=F32)
    o_ref[...] = x2 + gate * emb


def _ple(x1, y, p, pg, wpg, wp):
    T, D = x1.shape
    tm = TM_PLE
    tok = lambda width: pl.BlockSpec((tm, width), lambda i: (i, 0))
    return pl.pallas_call(
        _ple_kernel,
        grid=(T // tm,),
        in_specs=[tok(D), tok(D), tok(PLE_DIM), _const_spec((1, D)), _const_spec((D, D)), _const_spec((PLE_DIM, D))],
        out_specs=tok(D),
        out_shape=jax.ShapeDtypeStruct((T, D), F32),
        compiler_params=pltpu.CompilerParams(
            dimension_semantics=("parallel",), vmem_limit_bytes=VMEM_LIMIT),
        name="ple",
    )(x1, y, p, pg, wpg, wp)


def _routing_tables(info, counts, T):
    tm = TM_MOE
    n_rows = T + N_GROUPS * tm
    n_tiles = n_rows // tm
    group = info[:, 0].astype(jnp.int32)
    rank = info[:, 1].astype(jnp.int32)
    cnt = counts[0, :N_GROUPS].astype(jnp.int32)
    padded = (cnt + tm - 1) // tm * tm
    ends = jnp.cumsum(padded)
    base = ends - padded
    pos = base[group] + rank
    tok_of_row = jnp.full((n_rows,), -1, jnp.int32).at[pos].set(jnp.arange(T, dtype=jnp.int32))
    real = tok_of_row >= 0
    src = jnp.where(real, tok_of_row, 0)
    pad_index = jnp.cumsum(jnp.where(real, 0, 1)) - 1
    dst = jnp.where(real, tok_of_row, T + pad_index)
    starts = jnp.arange(n_tiles, dtype=jnp.int32) * tm
    gid = jnp.minimum(jnp.sum(starts[:, None] >= ends[None, :], axis=1), N_GROUPS - 1).astype(jnp.int32)
    n_used = (ends[N_GROUPS - 1] // tm).astype(jnp.int32)[None]
    gid = jnp.where(starts < ends[N_GROUPS - 1], gid, gid[jnp.maximum(n_used[0] - 1, 0)])
    shape3 = (n_tiles, 1, tm)
    return gid, n_used, src.reshape(shape3), dst.reshape(shape3)


def _rotary_tables(positions):
    half = ROT_DIM // 2
    inv_freq = ROPE_THETA ** (-jnp.arange(0, ROT_DIM, 2, dtype=F32) / ROT_DIM)
    ang = positions.astype(F32)[:, :, None] * inv_freq
    c, s = jnp.cos(ang), jnp.sin(ang)
    B, S = positions.shape
    pad = lambda n, val: jnp.full((B, S, n), val, F32)
    cos = jnp.concatenate([c, c, pad(HEAD_DIM - ROT_DIM, 1.0)], axis=-1)
    sa = jnp.concatenate([-s, pad(HEAD_DIM - half, 0.0)], axis=-1)
    sb = jnp.concatenate([pad(half, 0.0), s, pad(HEAD_DIM - ROT_DIM, 0.0)], axis=-1)
    two = lambda a: jnp.concatenate([a, a], axis=-1)
    return two(cos), two(sa), two(sb)


def _pack_w_in(w):
    n_main = C_FZ
    fz = jnp.pad(w[:, n_main:n_main + FOX_HEADS], ((0, 0), (0, LANES - FOX_HEADS)))
    return jnp.concatenate([w[:, :n_main], fz, w[:, n_main + FOX_HEADS:]], axis=1).astype(BF16)


def kernel(x, p, positions, attn_norm_g, w_in, pool_w, pool_scale, diff_qn_g, diff_kn_g, diff_lambda, diff_subln_g, fox_qn_g, fox_kn_g, fox_forget_b, w_branch, w_out, ffn_norm_g, w_route_group, b_route_group, w_route_expert, b_route_expert, moe_w_gate, moe_w_up, moe_w_down, ple_norm_g, w_ple_gate, w_ple):
    B, S, D = x.shape
    depth = w_in.shape[0]
    cos, sa, sb = _rotary_tables(positions)
    two = lambda a: jnp.concatenate([a, a], axis=-1)
    pad_lanes = lambda a: jnp.pad(a, ((0, 0), (0, LANES - a.shape[1])))

    for l in range(depth):
        lam_init = 0.8 - 0.6 * math.exp(-0.3 * l)
        qkg = jnp.stack([two(diff_qn_g[l]), two(diff_kn_g[l]), two(fox_qn_g[l]), two(fox_kn_g[l])])
        fb = pad_lanes(fox_forget_b[l][None, :])
        u, dq, dk, dv, fq, fk, fv, f, ft, gates = _inproj(
            x, attn_norm_g[l][None, :], _pack_w_in(w_in[l]), cos, sa, sb, qkg, fb)

        y_diff = _diff_attention(dq, dk, dv, diff_lambda[l], diff_subln_g[l][None, :], lam_init)
        y_fox = _fox_attention(fq, fk, fv, f, ft)

        w_r = jnp.concatenate([w_route_group[l], w_route_expert[l]], axis=1)
        w_r_hi = w_r.astype(BF16)
        w_r_lo = (w_r - w_r_hi.astype(F32)).astype(BF16)
        b_r = jnp.concatenate([b_route_group[l], b_route_expert[l]])[None, :]
        wrh, wrl, brt = pad_lanes(w_r_hi), pad_lanes(w_r_lo), pad_lanes(b_r)
        x1, hf, info, counts = _merge(
            x, u, y_diff, y_fox, gates,
            pool_w[l].astype(BF16), pool_scale[l][None, :], w_branch[l].astype(BF16), w_out[l].astype(BF16),
            ffn_norm_g[l][None, :], wrh, wrl, brt)

        T = B * S
        gid, n_used, src, dst = _routing_tables(info.reshape(T, LANES), counts, T)
        y = _moe(gid, n_used, src, dst, hf.reshape(T, D),
                 moe_w_gate[l], moe_w_up[l], moe_w_down[l], wrh, wrl, brt)
        x = _ple(x1.reshape(T, D), y, p[l].reshape(T, PLE_DIM),
                 ple_norm_g[l][None, :], w_ple_gate[l].astype(BF16), w_ple[l].astype(BF16)).reshape(B, S, D)
    return x
```

```python
import functools
import math

import jax
import jax.numpy as jnp
import numpy as np
from jax import lax
from jax.experimental import pallas as pl
from jax.experimental.pallas import tpu as pltpu

F32 = jnp.float32
BF16 = jnp.bfloat16

D_MODEL = 1024
HEAD_DIM = 64
LANES = 128
POOL_WINDOWS = (2, 4, 8, 16)
POOL_HALO = 16
BRANCH_WIDTH = 512
N_BRANCH = 3
FOX_HEADS = 8
ROT_DIM = HEAD_DIM // 4
ROPE_THETA = 500000.0
PLE_DIM = 256
N_GROUPS = 4
EXPERTS_PER_GROUP = 4
N_EXPERTS = 16
D_EXPERT = 256
RMS_EPS = 1e-6
NEG = -1e30
LOG2E = math.log2(math.e)

C_POOL, C_DQ, C_DK, C_DV, C_FQ, C_FK, C_FV, C_FZ, C_GZ = (0, 512, 1024, 1536, 2048, 2560, 3072, 3584, 3712)
IN_PACKED = C_GZ + N_BRANCH * D_MODEL

TM_IN = 512
TM_MERGE = 512
TM_MOE = 512
TM_PLE = 512
GROUP_HIDDEN = EXPERTS_PER_GROUP * D_EXPERT
TQ = 256
VMEM_LIMIT = 56 * 1024 * 1024


def _const_spec(shape):
    zeros = (0,) * len(shape)
    return pl.BlockSpec(shape, lambda *_: zeros, pipeline_mode=pl.Buffered(1))


def _layer_spec(layer, shape):
    zeros = (0,) * len(shape)
    return pl.BlockSpec((None,) + tuple(shape), lambda *_: (layer,) + zeros, pipeline_mode=pl.Buffered(1))


def _rms(x, g):
    return x * lax.rsqrt(jnp.mean(x * x, axis=-1, keepdims=True) + RMS_EPS) * g


def _router_logits(hf, wrh_ref, wrl_ref, br_ref):
    hi = hf.astype(BF16)
    lo = (hf - hi.astype(F32)).astype(BF16)
    return (jnp.dot(hi, wrh_ref[...], preferred_element_type=F32)
            + jnp.dot(lo, wrh_ref[...], preferred_element_type=F32)
            + jnp.dot(hi, wrl_ref[...], preferred_element_type=F32)
            + br_ref[...])


def _inproj_kernel(x_ref, g_ref, w_ref, tab_ref, rot_e_ref, rot_one_ref, qkg_ref, fb_ref,
                   u_ref, dq_ref, dk_ref, dv_ref, fq_ref, fk_ref, fv_ref, f_ref, ft_ref, gates_ref,
                   carry_ref):
    i = pl.program_id(1)
    tm = x_ref.shape[1]
    h = _rms(x_ref[0], g_ref[...]).astype(BF16)

    def proj(c0, width):
        return jnp.dot(h, w_ref[:, c0:c0 + width], preferred_element_type=F32)

    lo = lax.broadcasted_iota(jnp.int32, (tm, LANES), 1) < HEAD_DIM
    tab = tab_ref[0]
    tab_hi = tab.astype(BF16)
    tab_lo = (tab - tab_hi.astype(F32)).astype(BF16)
    placed = (jnp.dot(tab_hi, rot_e_ref[...], preferred_element_type=F32)
              + jnp.dot(tab_lo, rot_e_ref[...], preferred_element_type=F32))
    cos = placed[:, 0:LANES] + rot_one_ref[...]
    sa = placed[:, LANES:2 * LANES]
    sb = placed[:, 2 * LANES:3 * LANES]
    scale = HEAD_DIM ** -0.5 * LOG2E

    def head_norm(a, g):
        sq = a * a
        s_lo = jnp.sum(jnp.where(lo, sq, 0.0), axis=-1, keepdims=True)
        s_hi = jnp.sum(jnp.where(lo, 0.0, sq), axis=-1, keepdims=True)
        ss = jnp.where(lo, s_lo, s_hi)
        return a * lax.rsqrt(ss * (1.0 / HEAD_DIM) + RMS_EPS) * g

    def rotary(a):
        return a * cos + pltpu.roll(a, LANES - ROT_DIM // 2, 1) * sa + pltpu.roll(a, ROT_DIM // 2, 1) * sb

    def qk_chunk(c0, out_ref, g, rot, mult):
        acc = proj(c0, 512)
        for j in range(4):
            a = head_norm(acc[:, j * LANES:(j + 1) * LANES], g)
            if rot:
                a = rotary(a)
            if mult != 1.0:
                a = a * mult
            out_ref[0, :, j * LANES:(j + 1) * LANES] = a.astype(BF16)

    u_ref[0] = proj(C_POOL, 512).astype(BF16)
    qk_chunk(C_DQ, dq_ref, qkg_ref[0:1, :], True, scale)
    qk_chunk(C_DK, dk_ref, qkg_ref[1:2, :], True, 1.0)
    dv_ref[0] = proj(C_DV, 512).astype(BF16)
    qk_chunk(C_FQ, fq_ref, qkg_ref[2:3, :], False, scale)
    qk_chunk(C_FK, fk_ref, qkg_ref[3:4, :], False, 1.0)
    fv_ref[0] = proj(C_FV, 512).astype(BF16)

    z = proj(C_FZ, LANES) + fb_ref[...]
    logf = jnp.minimum(z, 0.0) - jnp.log1p(jnp.exp(-jnp.abs(z)))

    @pl.when(i == 0)
    def _():
        carry_ref[...] = jnp.zeros_like(carry_ref)

    row = lax.broadcasted_iota(jnp.int32, (tm, tm), 0)
    col = lax.broadcasted_iota(jnp.int32, (tm, tm), 1)
    tri = jnp.where(row >= col, 1.0, 0.0).astype(BF16)
    hi = logf.astype(BF16)
    lo_part = (logf - hi.astype(F32)).astype(BF16)
    fcum = (jnp.dot(tri, hi, preferred_element_type=F32)
            + jnp.dot(tri, lo_part, preferred_element_type=F32)
            + carry_ref[0:1, :])
    carry_ref[...] = jnp.broadcast_to(fcum[tm - 1:tm, :], carry_ref.shape)
    f_ref[0] = fcum
    ft_ref[0] = fcum.T[0:FOX_HEADS, :]

    for c in range(N_BRANCH * D_MODEL // 512):
        gates_ref[0, :, c * 512:(c + 1) * 512] = jax.nn.sigmoid(proj(C_GZ + c * 512, 512)).astype(BF16)


def _inproj(layer, x, g, w, tab, rot_e, rot_one, qkg, fb):
    B, S, D = x.shape
    tm = TM_IN
    tok = lambda width: pl.BlockSpec((1, tm, width), lambda b, i: (b, i, 0))
    out_shape = (
        [jax.ShapeDtypeStruct((B, S, 512), BF16)] * 7
        + [jax.ShapeDtypeStruct((B, S, LANES), F32),
           jax.ShapeDtypeStruct((B, FOX_HEADS, S), F32),
           jax.ShapeDtypeStruct((B, S, N_BRANCH * D), BF16)]
    )
    out_specs = (
        [tok(512)] * 7
        + [tok(LANES),
           pl.BlockSpec((1, FOX_HEADS, tm), lambda b, i: (b, 0, i)),
           tok(N_BRANCH * D)]
    )
    return pl.pallas_call(
        _inproj_kernel,
        grid=(B, S // tm),
        in_specs=[tok(D), _layer_spec(layer, (1, D)), _layer_spec(layer, (D, IN_PACKED)),
                  tok(2 * (ROT_DIM // 2)), _const_spec((2 * (ROT_DIM // 2), 3 * LANES)), _const_spec((1, LANES)),
                  _layer_spec(layer, (4, LANES)), _layer_spec(layer, (1, LANES))],
        out_specs=out_specs,
        out_shape=out_shape,
        scratch_shapes=[pltpu.VMEM((8, LANES), F32)],
        compiler_params=pltpu.CompilerParams(
            dimension_semantics=("parallel", "arbitrary"), vmem_limit_bytes=VMEM_LIMIT),
        name="inproj",
    )(x, g, w, tab, rot_e, rot_one, qkg, fb)


def _nt_dot(a, b):
    return lax.dot_general(a, b, (((1,), (1,)), ((), ())), preferred_element_type=F32)


def _softmax_rows(score_chunk, n_chunks, s_scr, p_scr):
    tq = TQ
    mt = jnp.full((tq, LANES), NEG, F32)
    for c in range(n_chunks):
        s = score_chunk(c)
        s_scr[:, c * tq:(c + 1) * tq] = s
        for w in range(tq // LANES):
            mt = jnp.maximum(mt, s[:, w * LANES:(w + 1) * LANES])
    m = jnp.max(mt, axis=1, keepdims=True)
    for c in range(n_chunks):
        p_scr[:, c * tq:(c + 1) * tq] = jnp.exp2(s_scr[:, c * tq:(c + 1) * tq] - m).astype(BF16)


def _diff_kernel(lam_init, q_ref, k_ref, v_ref, lam_ref, g_ref, o_ref, s_scr, p_scr, v2_scr):
    S = q_ref.shape[1]
    tq = TQ
    lo = lax.broadcasted_iota(jnp.int32, (tq, LANES), 1) < HEAD_DIM
    causal = (lax.broadcasted_iota(jnp.int32, (tq, tq), 1) <= lax.broadcasted_iota(jnp.int32, (tq, tq), 0))
    lp = lam_ref[...]
    lam = (jnp.exp(jnp.sum(lp[0:1, :] * lp[1:2, :], axis=1, keepdims=True))
           - jnp.exp(jnp.sum(lp[2:3, :] * lp[3:4, :], axis=1, keepdims=True)) + lam_init)
    v2_scr[:, 0:LANES] = v_ref[0]
    v2_scr[:, LANES:2 * LANES] = jnp.ones((S, LANES), BF16)

    for qi in range(S // tq):
        q0, kend = qi * tq, (qi + 1) * tq
        q = q_ref[0, q0:kend, :]
        zero = jnp.zeros_like(q)
        parts = []
        for comp in range(2):
            qm = jnp.where(lo, q, zero) if comp == 0 else jnp.where(lo, zero, q)

            def score_chunk(c, qm=qm, qi=qi):
                s = _nt_dot(qm, k_ref[0, c * tq:(c + 1) * tq, :])
                return jnp.where(causal, s, NEG) if c == qi else s

            _softmax_rows(score_chunk, qi + 1, s_scr.at[comp], p_scr.at[comp])
            pv = jnp.dot(p_scr[comp, :, 0:kend], v2_scr[0:kend, :], preferred_element_type=F32)
            parts.append(pv[:, 0:LANES] / pv[:, LANES:2 * LANES])
        o = parts[0] - lam * parts[1]
        o = _rms(o, g_ref[...]) * (1.0 - lam_init)
        o_ref[0, q0:kend, :] = o.astype(BF16)


def _diff_attention(layer, dq, dk, dv, lam_p, subln_g, lam_init):
    B, S, _ = dq.shape
    blk = pl.BlockSpec((1, S, LANES), lambda b, h: (b, 0, h))
    return pl.pallas_call(
        functools.partial(_diff_kernel, lam_init),
        grid=(B, 4),
        in_specs=[blk, blk, blk, _layer_spec(layer, (4, HEAD_DIM)), _layer_spec(layer, (1, LANES))],
        out_specs=blk,
        out_shape=jax.ShapeDtypeStruct((B, S, 512), BF16),
        scratch_shapes=[pltpu.VMEM((2, TQ, S), F32), pltpu.VMEM((2, TQ, S), BF16),
                        pltpu.VMEM((S, 2 * LANES), BF16)],
        compiler_params=pltpu.CompilerParams(
            dimension_semantics=("parallel", "parallel"), vmem_limit_bytes=VMEM_LIMIT),
        name="diff_attn",
    )(dq, dk, dv, lam_p, subln_g)


def _fox_kernel(q_ref, k_ref, v_ref, f_ref, ft_ref, o_ref, s_scr, p_scr, v2_scr):
    S = q_ref.shape[1]
    tq = TQ
    j = pl.program_id(1)
    lane = lax.broadcasted_iota(jnp.int32, (tq, LANES), 1)
    lo = lane < HEAD_DIM
    causal = (lax.broadcasted_iota(jnp.int32, (tq, tq), 1) <= lax.broadcasted_iota(jnp.int32, (tq, tq), 0))
    v = v_ref[0]
    lo_s = lax.broadcasted_iota(jnp.int32, (S, LANES), 1) < HEAD_DIM
    one = jnp.ones_like(v)
    v2_scr[0] = jnp.where(lo_s, v, one)
    v2_scr[1] = jnp.where(lo_s, one, v)

    for qi in range(S // tq):
        q0, kend = qi * tq, (qi + 1) * tq
        q = q_ref[0, q0:kend, :]
        zero = jnp.zeros_like(q)
        f_tile = f_ref[0, q0:kend, :] * LOG2E
        pvs = []
        for hh in range(2):
            qm = jnp.where(lo, q, zero) if hh == 0 else jnp.where(lo, zero, q)
            f_t = jnp.sum(jnp.where(lane == 2 * j + hh, f_tile, 0.0), axis=1, keepdims=True)

            def score_chunk(c, qm=qm, f_t=f_t, hh=hh, qi=qi):
                f_s = ft_ref[0, pl.ds(2 * j + hh, 1), c * tq:(c + 1) * tq] * LOG2E
                s = _nt_dot(qm, k_ref[0, c * tq:(c + 1) * tq, :]) + (f_t - f_s)
                return jnp.where(causal, s, NEG) if c == qi else s

            _softmax_rows(score_chunk, qi + 1, s_scr.at[hh], p_scr.at[hh])
            pvs.append(jnp.dot(p_scr[hh, :, 0:kend], v2_scr[hh, 0:kend, :], preferred_element_type=F32))
        o = jnp.where(lo, pvs[0] / pltpu.roll(pvs[0], HEAD_DIM, 1), pvs[1] / pltpu.roll(pvs[1], HEAD_DIM, 1))
        o_ref[0, q0:kend, :] = o.astype(BF16)


def _fox_attention(fq, fk, fv, f, ft):
    B, S, _ = fq.shape
    blk = pl.BlockSpec((1, S, LANES), lambda b, j: (b, 0, j))
    return pl.pallas_call(
        _fox_kernel,
        grid=(B, 4),
        in_specs=[blk, blk, blk,
                  pl.BlockSpec((1, S, LANES), lambda b, j: (b, 0, 0)),
                  pl.BlockSpec((1, FOX_HEADS, S), lambda b, j: (b, 0, 0))],
        out_specs=blk,
        out_shape=jax.ShapeDtypeStruct((B, S, 512), BF16),
        scratch_shapes=[pltpu.VMEM((2, TQ, S), F32), pltpu.VMEM((2, TQ, S), BF16),
                        pltpu.VMEM((2, S, LANES), BF16)],
        compiler_params=pltpu.CompilerParams(
            dimension_semantics=("parallel", "parallel"), vmem_limit_bytes=VMEM_LIMIT),
        name="fox_attn",
    )(fq, fk, fv, f, ft)


def _merge_kernel(x_ref, u_ref, uh_ref, yd_ref, yf_ref, gates_ref, pw_ref, ps_ref, wb_ref, wo_ref,
                  ng_ref, wrh_ref, wrl_ref, br_ref,
                  x1_ref, hf_ref, info_ref, count_ref):
    i = pl.program_id(1)
    tm = x_ref.shape[1]

    halo = uh_ref[0].astype(F32) * jnp.where(i > 0, 1.0, 0.0)
    ext = jnp.concatenate([halo, u_ref[0].astype(F32)], axis=0)
    t = i * tm + lax.broadcasted_iota(jnp.int32, (tm, 1), 0)
    ys = []
    for g, w in enumerate(POOL_WINDOWS):
        e = ext[:, g * LANES:(g + 1) * LANES]
        win = e
        step = 1
        while step < w:
            win = win + pltpu.roll(win, step, 0)
            step *= 2
        cnt = jnp.minimum(t + 1, w).astype(F32)
        d = win[POOL_HALO:, :] / cnt - e[POOL_HALO:, :]
        ys.append(jnp.dot(d.astype(BF16), pw_ref[g], preferred_element_type=F32))
    y_pool = jnp.concatenate(ys, axis=1) * ps_ref[...]

    branches = (y_pool.astype(BF16), yd_ref[0], yf_ref[0])
    merged = jnp.zeros((tm, D_MODEL), F32)
    for n in range(N_BRANCH):
        br = jnp.dot(branches[n], wb_ref[n], preferred_element_type=F32)
        merged = merged + gates_ref[0, :, n * D_MODEL:(n + 1) * D_MODEL].astype(F32) * br
    x1 = x_ref[0] + jnp.dot(merged.astype(BF16), wo_ref[...], preferred_element_type=F32)
    x1_ref[0] = x1

    hf = _rms(x1, ng_ref[...])
    hf_ref[0] = hf
    lg = _router_logits(hf, wrh_ref, wrl_ref, br_ref)
    lane = lax.broadcasted_iota(jnp.int32, (tm, LANES), 1)
    lane_f = lane.astype(F32)
    gmask = lane < N_GROUPS
    gl = jnp.where(gmask, lg, NEG)
    gmax = jnp.max(gl, axis=1, keepdims=True)
    gi = jnp.min(jnp.where(gmask & (gl == gmax), lane_f, float(LANES)), axis=1, keepdims=True)

    @pl.when((pl.program_id(0) == 0) & (i == 0))
    def _():
        count_ref[...] = jnp.zeros_like(count_ref)

    onehot = jnp.where(lane_f == gi, 1.0, 0.0)
    row = lax.broadcasted_iota(jnp.int32, (tm, tm), 0)
    col = lax.broadcasted_iota(jnp.int32, (tm, tm), 1)
    tri = jnp.where(row >= col, 1.0, 0.0).astype(BF16)
    incl = jnp.dot(tri, onehot.astype(BF16), preferred_element_type=F32)
    before = count_ref[0:1, :]
    rank = jnp.sum(onehot * (incl - onehot + before), axis=1, keepdims=True)
    count_ref[...] = jnp.broadcast_to(before + incl[tm - 1:tm, :], count_ref.shape)
    info_ref[0] = jnp.where(lane == 0, gi, jnp.where(lane == 1, rank, 0.0)).T[0:8, :]


def _merge(layer, x, u, yd, yf, gates, pw, ps, wb, wo, ng, wrh, wrl, brt):
    B, S, D = x.shape
    tm = TM_MERGE
    tok = lambda width: pl.BlockSpec((1, tm, width), lambda b, i: (b, i, 0))
    halo_blocks = tm // POOL_HALO
    halo = pl.BlockSpec((1, POOL_HALO, 512), lambda b, i: (b, jnp.maximum(i * halo_blocks - 1, 0), 0))
    return pl.pallas_call(
        _merge_kernel,
        grid=(B, S // tm),
        in_specs=[tok(D), tok(512), halo, tok(512), tok(512), tok(N_BRANCH * D),
                  _layer_spec(layer, (4, LANES, LANES)), _layer_spec(layer, (1, 512)),
                  _layer_spec(layer, (N_BRANCH, BRANCH_WIDTH, D)), _layer_spec(layer, (D, D)),
                  _layer_spec(layer, (1, D)), _layer_spec(layer, (D, LANES)), _layer_spec(layer, (D, LANES)),
                  _layer_spec(layer, (1, LANES))],
        out_specs=[tok(D), tok(D), pl.BlockSpec((1, 8, tm), lambda b, i: (b, 0, i)),
                   pl.BlockSpec((8, LANES), lambda b, i: (0, 0))],
        out_shape=[jax.ShapeDtypeStruct((B, S, D), F32),
                   jax.ShapeDtypeStruct((B, S, D), F32),
                   jax.ShapeDtypeStruct((B, 8, S), F32),
                   jax.ShapeDtypeStruct((8, LANES), F32)],
        compiler_params=pltpu.CompilerParams(
            dimension_semantics=("arbitrary", "arbitrary"), vmem_limit_bytes=VMEM_LIMIT),
        name="merge",
    )(x, u, u, yd, yf, gates, pw, ps, wb, wo, ng, wrh, wrl, brt)


def _moe_kernel(gid_ref, nused_ref, src_ref, src_next_ref, dst_ref,
                hf_hbm, wg_ref, wu_ref, wd_ref, wrh_ref, wrl_ref, br_ref,
                y_hbm,
                xbuf, ybuf, wgu_bf, wd_bf, sem_g, sem_s):
    i = pl.program_id(0)
    n_used = nused_ref[0]
    tm = TM_MOE
    slot = lax.rem(i, 2)

    def start_gather(idx_ref, to_slot):
        def body(r, c):
            pltpu.make_async_copy(hf_hbm.at[pl.ds(idx_ref[0, 0, r], 1)], xbuf.at[to_slot, pl.ds(r, 1)],
                                  sem_g.at[to_slot]).start()
            return c
        lax.fori_loop(0, tm, body, 0, unroll=8)

    def wait_gather(of_slot):
        pltpu.make_async_copy(hf_hbm.at[pl.ds(0, tm)], xbuf.at[of_slot], sem_g.at[of_slot]).wait()

    def wait_scatter(of_slot):
        pltpu.make_async_copy(ybuf.at[of_slot], y_hbm.at[pl.ds(0, tm)], sem_s.at[of_slot]).wait()

    @pl.when(i < n_used)
    def _():
        g = gid_ref[i]

        @pl.when(i == 0)
        def _():
            start_gather(src_ref, slot)
            ybuf[1] = jnp.zeros((tm, D_MODEL), F32)
            n_real = hf_hbm.shape[0]
            spare = [pltpu.make_async_copy(ybuf.at[1], y_hbm.at[pl.ds(n_real + k * tm, tm)], sem_s.at[1])
                     for k in range(N_GROUPS)]
            for cp in spare:
                cp.start()
            for cp in spare:
                cp.wait()

        @pl.when(i + 1 < n_used)
        def _():
            start_gather(src_next_ref, 1 - slot)

        @pl.when((i == 0) | (g != gid_ref[jnp.maximum(i - 1, 0)]))
        def _():
            for e in range(EXPERTS_PER_GROUP):
                wgu_bf[:, e * D_EXPERT:(e + 1) * D_EXPERT] = wg_ref[e].astype(BF16)
                wgu_bf[:, GROUP_HIDDEN + e * D_EXPERT:GROUP_HIDDEN + (e + 1) * D_EXPERT] = wu_ref[e].astype(BF16)
                wd_bf[e * D_EXPERT:(e + 1) * D_EXPERT, :] = wd_ref[e].astype(BF16)

        wait_gather(slot)
        hf = xbuf[slot]

        lg = _router_logits(hf, wrh_ref, wrl_ref, br_ref)
        lane = lax.broadcasted_iota(jnp.int32, (tm, LANES), 1)
        lane_f = lane.astype(F32)
        far = float(LANES)
        rmax = lambda a: jnp.max(a, axis=1, keepdims=True)
        rmin = lambda a: jnp.min(a, axis=1, keepdims=True)
        rsum = lambda a: jnp.sum(a, axis=1, keepdims=True)
        lg_g = rsum(jnp.where(lane == g, lg, 0.0))
        gp = 1.0 / rsum(jnp.where(lane < N_GROUPS, jnp.exp(lg - lg_g), 0.0))
        e_lo = N_GROUPS + EXPERTS_PER_GROUP * g
        emask = (lane >= e_lo) & (lane < e_lo + EXPERTS_PER_GROUP)
        el = jnp.where(emask, lg, NEG)
        m1 = rmax(el)
        i1 = rmin(jnp.where(emask & (el == m1), lane_f, far))
        rest = emask & (lane_f != i1)
        el2 = jnp.where(rest, lg, NEG)
        m2 = rmax(el2)
        i2 = rmin(jnp.where(rest & (el2 == m2), lane_f, far))
        r = jnp.exp(m2 - m1)
        comb = jnp.where(lane_f == i1, gp / (1.0 + r), jnp.where(lane_f == i2, gp * r / (1.0 + r), 0.0))
        cw = jnp.concatenate(
            [jnp.broadcast_to(rsum(jnp.where(lane == e_lo + e, comb, 0.0)), (tm, D_EXPERT))
             for e in range(EXPERTS_PER_GROUP)], axis=1)

        gu = jnp.dot(hf.astype(BF16), wgu_bf[...], preferred_element_type=F32)
        gate, up = gu[:, 0:GROUP_HIDDEN], gu[:, GROUP_HIDDEN:2 * GROUP_HIDDEN]
        act = gate * jax.nn.sigmoid(gate) * up * cw
        y = jnp.dot(act.astype(BF16), wd_bf[...], preferred_element_type=F32)

        @pl.when(i >= 2)
        def _():
            wait_scatter(slot)

        ybuf[slot] = y

        def scatter_row(r, c):
            pltpu.make_async_copy(ybuf.at[slot, pl.ds(r, 1)], y_hbm.at[pl.ds(dst_ref[0, 0, r], 1)],
                                  sem_s.at[slot]).start()
            return c
        lax.fori_loop(0, tm, scatter_row, 0, unroll=8)

        @pl.when(i == n_used - 1)
        def _():
            wait_scatter(slot)

            @pl.when(i >= 1)
            def _():
                wait_scatter(1 - slot)


def _moe(layer, gid, n_used, src, dst, hf, wg, wu, wd, wrh, wrl, brt):
    T, D = hf.shape
    tm = TM_MOE
    n_tiles = src.shape[0]
    idx_spec = lambda shift: pl.BlockSpec(
        (1, 1, tm), lambda i, gid, nu: (jnp.minimum(i + shift, n_tiles - 1), 0, 0), memory_space=pltpu.SMEM)
    grp = lambda shape: pl.BlockSpec((None,) + shape, lambda i, gid, nu: (layer, gid[i], 0, 0))
    return pl.pallas_call(
        _moe_kernel,
        grid_spec=pltpu.PrefetchScalarGridSpec(
            num_scalar_prefetch=2,
            grid=(n_tiles,),
            in_specs=[idx_spec(0), idx_spec(1), idx_spec(0),
                      pl.BlockSpec(memory_space=pl.ANY),
                      grp((EXPERTS_PER_GROUP, D, D_EXPERT)), grp((EXPERTS_PER_GROUP, D, D_EXPERT)),
                      grp((EXPERTS_PER_GROUP, D_EXPERT, D)),
                      _layer_spec(layer, (D, LANES)), _layer_spec(layer, (D, LANES)), _layer_spec(layer, (1, LANES))],
            out_specs=pl.BlockSpec(memory_space=pl.ANY),
            scratch_shapes=[pltpu.VMEM((2, tm, D), F32), pltpu.VMEM((2, tm, D), F32),
                            pltpu.VMEM((D, 2 * GROUP_HIDDEN), BF16), pltpu.VMEM((GROUP_HIDDEN, D), BF16),
                            pltpu.SemaphoreType.DMA((2,)), pltpu.SemaphoreType.DMA((2,))]),
        out_shape=jax.ShapeDtypeStruct((n_tiles * tm, D), F32),
        compiler_params=pltpu.CompilerParams(
            dimension_semantics=("arbitrary",), vmem_limit_bytes=VMEM_LIMIT),
        name="moe",
    )(gid, n_used, src, src, dst, hf, wg, wu, wd, wrh, wrl, brt)


def _ple_kernel(x1_ref, y_ref, p_ref, pg_ref, wpg_ref, wp_ref, o_ref):
    x2 = x1_ref[...] + y_ref[...]
    hp = _rms(x2, pg_ref[...]).astype(BF16)
    gate = jax.nn.sigmoid(jnp.dot(hp, wpg_ref[...], preferred_element_type=F32))
    emb = jnp.dot(p_ref[...].astype(BF16), wp_ref[...], preferred_element_type=F32)
    o_ref[...] = x2 + gate * emb


def _ple(layer, x1, y, p, pg, wpg, wp):
    T, D = x1.shape
    tm = TM_PLE
    tok = lambda width: pl.BlockSpec((tm, width), lambda i: (i, 0))
    return pl.pallas_call(
        _ple_kernel,
        grid=(T // tm,),
        in_specs=[tok(D), tok(D), pl.BlockSpec((None, tm, PLE_DIM), lambda i: (layer, i, 0)),
                  _layer_spec(layer, (1, D)), _layer_spec(layer, (D, D)), _layer_spec(layer, (PLE_DIM, D))],
        out_specs=tok(D),
        out_shape=jax.ShapeDtypeStruct((T, D), F32),
        compiler_params=pltpu.CompilerParams(
            dimension_semantics=("parallel",), vmem_limit_bytes=VMEM_LIMIT),
        name="ple",
    )(x1, y, p, pg, wpg, wp)


def _routing_tables(info, counts):
    tm = TM_MOE
    T = info.shape[0] * info.shape[2]
    n_rows = T + N_GROUPS * tm
    n_tiles = n_rows // tm
    group = info[:, 0, :].reshape(T).astype(jnp.int32)
    rank = info[:, 1, :].reshape(T).astype(jnp.int32)
    cnt = counts[0, :N_GROUPS].astype(jnp.int32)
    padded = (cnt + tm - 1) // tm * tm
    ends = jnp.cumsum(padded)
    base = ends - padded
    pos = base[group] + rank
    tok_of_row = jnp.full((n_rows,), -1, jnp.int32).at[pos].set(jnp.arange(T, dtype=jnp.int32))
    real = tok_of_row >= 0
    src = jnp.where(real, tok_of_row, 0)
    pad_index = jnp.cumsum(jnp.where(real, 0, 1)) - 1
    dst = jnp.where(real, tok_of_row, T + pad_index)
    starts = jnp.arange(n_tiles, dtype=jnp.int32) * tm
    gid = jnp.minimum(jnp.sum(starts[:, None] >= ends[None, :], axis=1), N_GROUPS - 1).astype(jnp.int32)
    n_used = (ends[N_GROUPS - 1] // tm).astype(jnp.int32)[None]
    gid = jnp.where(starts < ends[N_GROUPS - 1], gid, gid[jnp.maximum(n_used[0] - 1, 0)])
    shape3 = (n_tiles, 1, tm)
    return gid, n_used, src.reshape(shape3), dst.reshape(shape3)


def _rotary_table(positions):
    half = ROT_DIM // 2
    per_row = LANES // half
    inv_freq = ROPE_THETA ** (-jnp.arange(0, ROT_DIM, 2, dtype=F32) / ROT_DIM)
    B, S = positions.shape
    pos = jnp.repeat(positions.astype(F32).reshape(B, S // per_row, per_row), half, axis=-1)
    ang = pos * jnp.tile(inv_freq, per_row)
    c = jnp.cos(ang).reshape(B, S, half)
    s = jnp.sin(ang).reshape(B, S, half)
    return jnp.concatenate([c, s], axis=-1)


def _rotary_placement():
    half = ROT_DIM // 2
    e = np.zeros((2 * half, 3 * LANES), np.float32)
    one = np.zeros((1, LANES), np.float32)
    for lane in range(LANES):
        d = lane % HEAD_DIM
        if d < half:
            e[d, lane] = 1.0
            e[half + d, LANES + lane] = -1.0
        elif d < ROT_DIM:
            e[d - half, lane] = 1.0
            e[d, 2 * LANES + lane] = 1.0
        else:
            one[0, lane] = 1.0
    return jnp.asarray(e, BF16), jnp.asarray(one, F32)


def _pack_w_in(w):
    n_main = C_FZ
    fz = jnp.pad(w[:, :, n_main:n_main + FOX_HEADS], ((0, 0), (0, 0), (0, LANES - FOX_HEADS)))
    return jnp.concatenate([w[:, :, :n_main], fz, w[:, :, n_main + FOX_HEADS:]], axis=2).astype(BF16)


def kernel(x, p, positions, attn_norm_g, w_in, pool_w, pool_scale, diff_qn_g, diff_kn_g, diff_lambda, diff_subln_g, fox_qn_g, fox_kn_g, fox_forget_b, w_branch, w_out, ffn_norm_g, w_route_group, b_route_group, w_route_expert, b_route_expert, moe_w_gate, moe_w_up, moe_w_down, ple_norm_g, w_ple_gate, w_ple):
    B, S, D = x.shape
    T = B * S
    depth = w_in.shape[0]
    two = lambda a: jnp.concatenate([a, a], axis=-1)
    pad_lanes = lambda a: jnp.pad(a, [(0, 0)] * (a.ndim - 1) + [(0, LANES - a.shape[-1])])
    row = lambda a: a[:, None, :]

    tab = _rotary_table(positions)
    rot_e, rot_one = _rotary_placement()
    w_in_p = _pack_w_in(w_in)
    qkg = jnp.stack([two(diff_qn_g), two(diff_kn_g), two(fox_qn_g), two(fox_kn_g)], axis=1)
    fb = row(pad_lanes(fox_forget_b))
    w_r = jnp.concatenate([w_route_group, w_route_expert], axis=2)
    w_r_hi = w_r.astype(BF16)
    wrh = pad_lanes(w_r_hi)
    wrl = pad_lanes((w_r - w_r_hi.astype(F32)).astype(BF16))
    brt = row(pad_lanes(jnp.concatenate([b_route_group, b_route_expert], axis=1)))
    pool_w_b, w_branch_b, w_out_b = pool_w.astype(BF16), w_branch.astype(BF16), w_out.astype(BF16)
    w_ple_gate_b, w_ple_b = w_ple_gate.astype(BF16), w_ple.astype(BF16)
    p_flat = p.reshape(depth, T, PLE_DIM)

    for l in range(depth):
        lam_init = 0.8 - 0.6 * math.exp(-0.3 * l)
        u, dq, dk, dv, fq, fk, fv, f, ft, gates = _inproj(
            l, x, row(attn_norm_g), w_in_p, tab, rot_e, rot_one, qkg, fb)
        y_diff = _diff_attention(l, dq, dk, dv, diff_lambda, row(diff_subln_g), lam_init)
        y_fox = _fox_attention(fq, fk, fv, f, ft)
        x1, hf, info, counts = _merge(
            l, x, u, y_diff, y_fox, gates, pool_w_b, row(pool_scale), w_branch_b, w_out_b,
            row(ffn_norm_g), wrh, wrl, brt)
        gid, n_used, src, dst = _routing_tables(info, counts)
        y = _moe(l, gid, n_used, src, dst, hf.reshape(T, D), moe_w_gate, moe_w_up, moe_w_down, wrh, wrl, brt)
        x = _ple(l, x1.reshape(T, D), y, p_flat, row(ple_norm_g), w_ple_gate_b, w_ple_b).reshape(B, S, D)
    return x
```

```python
import functools
import math

import jax
import jax.numpy as jnp
import numpy as np
from jax import lax
from jax.experimental import pallas as pl
from jax.experimental.pallas import tpu as pltpu

F32 = jnp.float32
BF16 = jnp.bfloat16

D_MODEL = 1024
HEAD_DIM = 64
LANES = 128
POOL_WINDOWS = (2, 4, 8, 16)
POOL_HALO = 16
BRANCH_WIDTH = 512
N_BRANCH = 3
FOX_HEADS = 8
ROT_DIM = HEAD_DIM // 4
ROPE_THETA = 500000.0
PLE_DIM = 256
N_GROUPS = 4
EXPERTS_PER_GROUP = 4
N_EXPERTS = 16
D_EXPERT = 256
RMS_EPS = 1e-6
NEG = -1e30
LOG2E = math.log2(math.e)

C_POOL, C_DQ, C_DK, C_DV, C_FQ, C_FK, C_FV, C_FZ, C_GZ = (0, 512, 1024, 1536, 2048, 2560, 3072, 3584, 3712)
IN_PACKED = C_GZ + N_BRANCH * D_MODEL

TM_IN = 512
TM_MERGE = 512
TM_MOE = 512
TM_PLE = 512
GROUP_HIDDEN = EXPERTS_PER_GROUP * D_EXPERT
TQ = 256
VMEM_LIMIT = 56 * 1024 * 1024


def _const_spec(shape):
    zeros = (0,) * len(shape)
    return pl.BlockSpec(shape, lambda *_: zeros, pipeline_mode=pl.Buffered(1))


def _layer_spec(layer, shape):
    zeros = (0,) * len(shape)
    return pl.BlockSpec((None,) + tuple(shape), lambda *_: (layer,) + zeros, pipeline_mode=pl.Buffered(1))


def _rms(x, g):
    return x * lax.rsqrt(jnp.mean(x * x, axis=-1, keepdims=True) + RMS_EPS) * g


def _router_logits(hf, wrh_ref, wrl_ref, br_ref):
    hi = hf.astype(BF16)
    lo = (hf - hi.astype(F32)).astype(BF16)
    return (jnp.dot(hi, wrh_ref[...], preferred_element_type=F32)
            + jnp.dot(lo, wrh_ref[...], preferred_element_type=F32)
            + jnp.dot(hi, wrl_ref[...], preferred_element_type=F32)
            + br_ref[...])


def _inproj_kernel(x_ref, g_ref, w_ref, tab_ref, rot_e_ref, rot_one_ref, qkg_ref, fb_ref,
                   u_ref, dq_ref, dk_ref, dv_ref, fq_ref, fk_ref, fv_ref, f_ref, ft_ref, gates_ref,
                   carry_ref):
    i = pl.program_id(1)
    tm = x_ref.shape[1]
    h = _rms(x_ref[0], g_ref[...]).astype(BF16)

    def proj(c0, width):
        return jnp.dot(h, w_ref[:, c0:c0 + width], preferred_element_type=F32)

    lo = lax.broadcasted_iota(jnp.int32, (tm, LANES), 1) < HEAD_DIM
    tab = tab_ref[0]
    tab_hi = tab.astype(BF16)
    tab_lo = (tab - tab_hi.astype(F32)).astype(BF16)
    placed = (jnp.dot(tab_hi, rot_e_ref[...], preferred_element_type=F32)
              + jnp.dot(tab_lo, rot_e_ref[...], preferred_element_type=F32))
    cos = placed[:, 0:LANES] + rot_one_ref[...]
    sa = placed[:, LANES:2 * LANES]
    sb = placed[:, 2 * LANES:3 * LANES]
    scale = HEAD_DIM ** -0.5 * LOG2E

    def head_norm(a, g):
        sq = a * a
        s_lo = jnp.sum(jnp.where(lo, sq, 0.0), axis=-1, keepdims=True)
        s_hi = jnp.sum(jnp.where(lo, 0.0, sq), axis=-1, keepdims=True)
        ss = jnp.where(lo, s_lo, s_hi)
        return a * lax.rsqrt(ss * (1.0 / HEAD_DIM) + RMS_EPS) * g

    def rotary(a):
        return a * cos + pltpu.roll(a, LANES - ROT_DIM // 2, 1) * sa + pltpu.roll(a, ROT_DIM // 2, 1) * sb

    def qk_chunk(c0, out_ref, g, rot, mult):
        acc = proj(c0, 512)
        for j in range(4):
            a = head_norm(acc[:, j * LANES:(j + 1) * LANES], g)
            if rot:
                a = rotary(a)
            if mult != 1.0:
                a = a * mult
            out_ref[0, :, j * LANES:(j + 1) * LANES] = a.astype(BF16)

    u_ref[0] = proj(C_POOL, 512).astype(BF16)
    qk_chunk(C_DQ, dq_ref, qkg_ref[0:1, :], True, scale)
    qk_chunk(C_DK, dk_ref, qkg_ref[1:2, :], True, 1.0)
    dv_ref[0] = proj(C_DV, 512).astype(BF16)
    qk_chunk(C_FQ, fq_ref, qkg_ref[2:3, :], False, scale)
    qk_chunk(C_FK, fk_ref, qkg_ref[3:4, :], False, 1.0)
    fv_ref[0] = proj(C_FV, 512).astype(BF16)

    z = proj(C_FZ, LANES) + fb_ref[...]
    logf = jnp.minimum(z, 0.0) - jnp.log1p(jnp.exp(-jnp.abs(z)))

    @pl.when(i == 0)
    def _():
        carry_ref[...] = jnp.zeros_like(carry_ref)

    row = lax.broadcasted_iota(jnp.int32, (tm, tm), 0)
    col = lax.broadcasted_iota(jnp.int32, (tm, tm), 1)
    tri = jnp.where(row >= col, 1.0, 0.0).astype(BF16)
    hi = logf.astype(BF16)
    lo_part = (logf - hi.astype(F32)).astype(BF16)
    fcum = (jnp.dot(tri, hi, preferred_element_type=F32)
            + jnp.dot(tri, lo_part, preferred_element_type=F32)
            + carry_ref[0:1, :])
    carry_ref[...] = jnp.broadcast_to(fcum[tm - 1:tm, :], carry_ref.shape)
    f_ref[0] = fcum
    ft_ref[0] = fcum.T[0:FOX_HEADS, :]

    for c in range(N_BRANCH * D_MODEL // 512):
        gates_ref[0, :, c * 512:(c + 1) * 512] = jax.nn.sigmoid(proj(C_GZ + c * 512, 512)).astype(BF16)


def _inproj(layer, x, g, w, tab, rot_e, rot_one, qkg, fb):
    B, S, D = x.shape
    tm = TM_IN
    tok = lambda width: pl.BlockSpec((1, tm, width), lambda b, i: (b, i, 0))
    out_shape = (
        [jax.ShapeDtypeStruct((B, S, 512), BF16)] * 7
        + [jax.ShapeDtypeStruct((B, S, LANES), F32),
           jax.ShapeDtypeStruct((B, FOX_HEADS, S), F32),
           jax.ShapeDtypeStruct((B, S, N_BRANCH * D), BF16)]
    )
    out_specs = (
        [tok(512)] * 7
        + [tok(LANES),
           pl.BlockSpec((1, FOX_HEADS, tm), lambda b, i: (b, 0, i)),
           tok(N_BRANCH * D)]
    )
    return pl.pallas_call(
        _inproj_kernel,
        grid=(B, S // tm),
        in_specs=[tok(D), _layer_spec(layer, (1, D)), _layer_spec(layer, (D, IN_PACKED)),
                  tok(2 * (ROT_DIM // 2)), _const_spec((2 * (ROT_DIM // 2), 3 * LANES)), _const_spec((1, LANES)),
                  _layer_spec(layer, (4, LANES)), _layer_spec(layer, (1, LANES))],
        out_specs=out_specs,
        out_shape=out_shape,
        scratch_shapes=[pltpu.VMEM((8, LANES), F32)],
        compiler_params=pltpu.CompilerParams(
            dimension_semantics=("parallel", "arbitrary"), vmem_limit_bytes=VMEM_LIMIT),
        name="inproj",
    )(x, g, w, tab, rot_e, rot_one, qkg, fb)


def _nt_dot(a, b):
    return lax.dot_general(a, b, (((1,), (1,)), ((), ())), preferred_element_type=F32)


def _softmax_rows(score_chunk, n_chunks, s_scr, p_scr):
    tq = TQ
    mt = jnp.full((tq, LANES), NEG, F32)
    for c in range(n_chunks):
        s = score_chunk(c)
        s_scr[:, c * tq:(c + 1) * tq] = s
        for w in range(tq // LANES):
            mt = jnp.maximum(mt, s[:, w * LANES:(w + 1) * LANES])
    m = jnp.max(mt, axis=1, keepdims=True)
    for c in range(n_chunks):
        p_scr[:, c * tq:(c + 1) * tq] = jnp.exp2(s_scr[:, c * tq:(c + 1) * tq] - m).astype(BF16)


def _diff_kernel(lam_init, q_ref, k_ref, v_ref, lam_ref, g_ref, o_ref, s_scr, p_scr, v2_scr):
    S = q_ref.shape[1]
    tq = TQ
    lo = lax.broadcasted_iota(jnp.int32, (tq, LANES), 1) < HEAD_DIM
    causal = (lax.broadcasted_iota(jnp.int32, (tq, tq), 1) <= lax.broadcasted_iota(jnp.int32, (tq, tq), 0))
    lp = lam_ref[...]
    lam = (jnp.exp(jnp.sum(lp[0:1, :] * lp[1:2, :], axis=1, keepdims=True))
           - jnp.exp(jnp.sum(lp[2:3, :] * lp[3:4, :], axis=1, keepdims=True)) + lam_init)
    v2_scr[:, 0:LANES] = v_ref[0]
    v2_scr[:, LANES:2 * LANES] = jnp.ones((S, LANES), BF16)

    for qi in range(S // tq):
        q0, kend = qi * tq, (qi + 1) * tq
        q = q_ref[0, q0:kend, :]
        zero = jnp.zeros_like(q)
        parts = []
        for comp in range(2):
            qm = jnp.where(lo, q, zero) if comp == 0 else jnp.where(lo, zero, q)

            def score_chunk(c, qm=qm, qi=qi):
                s = _nt_dot(qm, k_ref[0, c * tq:(c + 1) * tq, :])
                return jnp.where(causal, s, NEG) if c == qi else s

            _softmax_rows(score_chunk, qi + 1, s_scr.at[comp], p_scr.at[comp])
            pv = jnp.dot(p_scr[comp, :, 0:kend], v2_scr[0:kend, :], preferred_element_type=F32)
            parts.append(pv[:, 0:LANES] / pv[:, LANES:2 * LANES])
        o = parts[0] - lam * parts[1]
        o = _rms(o, g_ref[...]) * (1.0 - lam_init)
        o_ref[0, q0:kend, :] = o.astype(BF16)


def _diff_attention(layer, dq, dk, dv, lam_p, subln_g, lam_init):
    B, S, _ = dq.shape
    blk = pl.BlockSpec((1, S, LANES), lambda b, h: (b, 0, h))
    return pl.pallas_call(
        functools.partial(_diff_kernel, lam_init),
        grid=(B, 4),
        in_specs=[blk, blk, blk, _layer_spec(layer, (4, HEAD_DIM)), _layer_spec(layer, (1, LANES))],
        out_specs=blk,
        out_shape=jax.ShapeDtypeStruct((B, S, 512), BF16),
        scratch_shapes=[pltpu.VMEM((2, TQ, S), F32), pltpu.VMEM((2, TQ, S), BF16),
                        pltpu.VMEM((S, 2 * LANES), BF16)],
        compiler_params=pltpu.CompilerParams(
            dimension_semantics=("parallel", "parallel"), vmem_limit_bytes=VMEM_LIMIT),
        name="diff_attn",
    )(dq, dk, dv, lam_p, subln_g)


def _fox_kernel(q_ref, k_ref, v_ref, f_ref, ft_ref, o_ref, s_scr, p_scr, v2_scr):
    S = q_ref.shape[1]
    tq = TQ
    j = pl.program_id(1)
    lane = lax.broadcasted_iota(jnp.int32, (tq, LANES), 1)
    lo = lane < HEAD_DIM
    causal = (lax.broadcasted_iota(jnp.int32, (tq, tq), 1) <= lax.broadcasted_iota(jnp.int32, (tq, tq), 0))
    v = v_ref[0]
    lo_s = lax.broadcasted_iota(jnp.int32, (S, LANES), 1) < HEAD_DIM
    one = jnp.ones_like(v)
    v2_scr[0] = jnp.where(lo_s, v, one)
    v2_scr[1] = jnp.where(lo_s, one, v)

    for qi in range(S // tq):
        q0, kend = qi * tq, (qi + 1) * tq
        q = q_ref[0, q0:kend, :]
        zero = jnp.zeros_like(q)
        f_tile = f_ref[0, q0:kend, :] * LOG2E
        pvs = []
        for hh in range(2):
            qm = jnp.where(lo, q, zero) if hh == 0 else jnp.where(lo, zero, q)
            f_t = jnp.sum(jnp.where(lane == 2 * j + hh, f_tile, 0.0), axis=1, keepdims=True)

            def score_chunk(c, qm=qm, f_t=f_t, hh=hh, qi=qi):
                f_s = ft_ref[0, pl.ds(2 * j + hh, 1), c * tq:(c + 1) * tq] * LOG2E
                s = _nt_dot(qm, k_ref[0, c * tq:(c + 1) * tq, :]) + (f_t - f_s)
                return jnp.where(causal, s, NEG) if c == qi else s

            _softmax_rows(score_chunk, qi + 1, s_scr.at[hh], p_scr.at[hh])
            pvs.append(jnp.dot(p_scr[hh, :, 0:kend], v2_scr[hh, 0:kend, :], preferred_element_type=F32))
        o = jnp.where(lo, pvs[0] / pltpu.roll(pvs[0], HEAD_DIM, 1), pvs[1] / pltpu.roll(pvs[1], HEAD_DIM, 1))
        o_ref[0, q0:kend, :] = o.astype(BF16)


def _fox_attention(fq, fk, fv, f, ft):
    B, S, _ = fq.shape
    blk = pl.BlockSpec((1, S, LANES), lambda b, j: (b, 0, j))
    return pl.pallas_call(
        _fox_kernel,
        grid=(B, 4),
        in_specs=[blk, blk, blk,
                  pl.BlockSpec((1, S, LANES), lambda b, j: (b, 0, 0)),
                  pl.BlockSpec((1, FOX_HEADS, S), lambda b, j: (b, 0, 0))],
        out_specs=blk,
        out_shape=jax.ShapeDtypeStruct((B, S, 512), BF16),
        scratch_shapes=[pltpu.VMEM((2, TQ, S), F32), pltpu.VMEM((2, TQ, S), BF16),
                        pltpu.VMEM((2, S, LANES), BF16)],
        compiler_params=pltpu.CompilerParams(
            dimension_semantics=("parallel", "parallel"), vmem_limit_bytes=VMEM_LIMIT),
        name="fox_attn",
    )(fq, fk, fv, f, ft)


def _merge_kernel(x_ref, u_ref, uh_ref, yd_ref, yf_ref, gates_ref, pw_ref, ps_ref, wb_ref, wo_ref,
                  ng_ref, wrh_ref, wrl_ref, br_ref,
                  x1_ref, hf_ref, info_ref, count_ref):
    i = pl.program_id(1)
    tm = x_ref.shape[1]

    halo = uh_ref[0].astype(F32) * jnp.where(i > 0, 1.0, 0.0)
    ext = jnp.concatenate([halo, u_ref[0].astype(F32)], axis=0)
    t = i * tm + lax.broadcasted_iota(jnp.int32, (tm, 1), 0)
    ys = []
    for g, w in enumerate(POOL_WINDOWS):
        e = ext[:, g * LANES:(g + 1) * LANES]
        win = e
        step = 1
        while step < w:
            win = win + pltpu.roll(win, step, 0)
            step *= 2
        cnt = jnp.minimum(t + 1, w).astype(F32)
        d = win[POOL_HALO:, :] / cnt - e[POOL_HALO:, :]
        ys.append(jnp.dot(d.astype(BF16), pw_ref[g], preferred_element_type=F32))
    y_pool = jnp.concatenate(ys, axis=1) * ps_ref[...]

    branches = (y_pool.astype(BF16), yd_ref[0], yf_ref[0])
    merged = jnp.zeros((tm, D_MODEL), F32)
    for n in range(N_BRANCH):
        br = jnp.dot(branches[n], wb_ref[n], preferred_element_type=F32)
        merged = merged + gates_ref[0, :, n * D_MODEL:(n + 1) * D_MODEL].astype(F32) * br
    x1 = x_ref[0] + jnp.dot(merged.astype(BF16), wo_ref[...], preferred_element_type=F32)
    x1_ref[0] = x1

    hf = _rms(x1, ng_ref[...])
    hf_ref[0] = hf
    lg = _router_logits(hf, wrh_ref, wrl_ref, br_ref)
    lane = lax.broadcasted_iota(jnp.int32, (tm, LANES), 1)
    lane_f = lane.astype(F32)
    gmask = lane < N_GROUPS
    gl = jnp.where(gmask, lg, NEG)
    gmax = jnp.max(gl, axis=1, keepdims=True)
    gi = jnp.min(jnp.where(gmask & (gl == gmax), lane_f, float(LANES)), axis=1, keepdims=True)

    @pl.when((pl.program_id(0) == 0) & (i == 0))
    def _():
        count_ref[...] = jnp.zeros_like(count_ref)

    onehot = jnp.where(lane_f == gi, 1.0, 0.0)
    row = lax.broadcasted_iota(jnp.int32, (tm, tm), 0)
    col = lax.broadcasted_iota(jnp.int32, (tm, tm), 1)
    tri = jnp.where(row >= col, 1.0, 0.0).astype(BF16)
    incl = jnp.dot(tri, onehot.astype(BF16), preferred_element_type=F32)
    before = count_ref[0:1, :]
    rank = jnp.sum(onehot * (incl - onehot + before), axis=1, keepdims=True)
    count_ref[...] = jnp.broadcast_to(before + incl[tm - 1:tm, :], count_ref.shape)
    info_ref[0] = jnp.where(lane == 0, gi, jnp.where(lane == 1, rank, 0.0)).T[0:8, :]


def _merge(layer, x, u, yd, yf, gates, pw, ps, wb, wo, ng, wrh, wrl, brt):
    B, S, D = x.shape
    tm = TM_MERGE
    tok = lambda width: pl.BlockSpec((1, tm, width), lambda b, i: (b, i, 0))
    halo_blocks = tm // POOL_HALO
    halo = pl.BlockSpec((1, POOL_HALO, 512), lambda b, i: (b, jnp.maximum(i * halo_blocks - 1, 0), 0))
    return pl.pallas_call(
        _merge_kernel,
        grid=(B, S // tm),
        in_specs=[tok(D), tok(512), halo, tok(512), tok(512), tok(N_BRANCH * D),
                  _layer_spec(layer, (4, LANES, LANES)), _layer_spec(layer, (1, 512)),
                  _layer_spec(layer, (N_BRANCH, BRANCH_WIDTH, D)), _layer_spec(layer, (D, D)),
                  _layer_spec(layer, (1, D)), _layer_spec(layer, (D, LANES)), _layer_spec(layer, (D, LANES)),
                  _layer_spec(layer, (1, LANES))],
        out_specs=[tok(D), tok(D), pl.BlockSpec((1, 8, tm), lambda b, i: (b, 0, i)),
                   pl.BlockSpec((8, LANES), lambda b, i: (0, 0))],
        out_shape=[jax.ShapeDtypeStruct((B, S, D), F32),
                   jax.ShapeDtypeStruct((B, S, D), F32),
                   jax.ShapeDtypeStruct((B, 8, S), F32),
                   jax.ShapeDtypeStruct((8, LANES), F32)],
        compiler_params=pltpu.CompilerParams(
            dimension_semantics=("arbitrary", "arbitrary"), vmem_limit_bytes=VMEM_LIMIT),
        name="merge",
    )(x, u, u, yd, yf, gates, pw, ps, wb, wo, ng, wrh, wrl, brt)


def _moe_kernel(gid_ref, nused_ref, src_ref, src_next_ref, dst_prev_ref,
                hf_hbm, wg_ref, wu_ref, wd_ref, wrh_ref, wrl_ref, br_ref,
                y_hbm,
                xbuf, ybuf, wgu_bf, wd_bf, sem_g, sem_s):
    i = pl.program_id(0)
    n_used = nused_ref[0]
    tm = TM_MOE
    n_tiles = pl.num_programs(0) - 1

    def gather_row(idx_ref, r, to_slot):
        return pltpu.make_async_copy(hf_hbm.at[pl.ds(idx_ref[0, 0, r], 1)], xbuf.at[to_slot, pl.ds(r, 1)],
                                     sem_g.at[to_slot])

    def scatter_row(r, from_slot):
        return pltpu.make_async_copy(ybuf.at[from_slot, pl.ds(r, 1)], y_hbm.at[pl.ds(dst_prev_ref[0, 0, r], 1)],
                                     sem_s.at[from_slot])

    def start_rows(make_copy, unrolled):
        if unrolled:
            for r in range(tm):
                make_copy(r).start()
        else:
            def body(r, c):
                make_copy(r).start()
                return c
            lax.fori_loop(0, tm, body, 0, unroll=8)

    def wait_gather(of_slot):
        pltpu.make_async_copy(hf_hbm.at[pl.ds(0, tm)], xbuf.at[of_slot], sem_g.at[of_slot]).wait()

    def wait_scatter(of_slot):
        pltpu.make_async_copy(ybuf.at[of_slot], y_hbm.at[pl.ds(0, tm)], sem_s.at[of_slot]).wait()

    def cast_weights():
        for e in range(EXPERTS_PER_GROUP):
            wgu_bf[:, e * D_EXPERT:(e + 1) * D_EXPERT] = wg_ref[e].astype(BF16)
            wgu_bf[:, GROUP_HIDDEN + e * D_EXPERT:GROUP_HIDDEN + (e + 1) * D_EXPERT] = wu_ref[e].astype(BF16)
            wd_bf[e * D_EXPERT:(e + 1) * D_EXPERT, :] = wd_ref[e].astype(BF16)

    def run_experts(slot, g):
        hf = xbuf[slot]

        lg = _router_logits(hf, wrh_ref, wrl_ref, br_ref)
        lane = lax.broadcasted_iota(jnp.int32, (tm, LANES), 1)
        lane_f = lane.astype(F32)
        far = float(LANES)
        rmax = lambda a: jnp.max(a, axis=1, keepdims=True)
        rmin = lambda a: jnp.min(a, axis=1, keepdims=True)
        rsum = lambda a: jnp.sum(a, axis=1, keepdims=True)
        lg_g = rsum(jnp.where(lane == g, lg, 0.0))
        gp = 1.0 / rsum(jnp.where(lane < N_GROUPS, jnp.exp(lg - lg_g), 0.0))
        e_lo = N_GROUPS + EXPERTS_PER_GROUP * g
        emask = (lane >= e_lo) & (lane < e_lo + EXPERTS_PER_GROUP)
        el = jnp.where(emask, lg, NEG)
        m1 = rmax(el)
        i1 = rmin(jnp.where(emask & (el == m1), lane_f, far))
        rest = emask & (lane_f != i1)
        el2 = jnp.where(rest, lg, NEG)
        m2 = rmax(el2)
        i2 = rmin(jnp.where(rest & (el2 == m2), lane_f, far))
        r = jnp.exp(m2 - m1)
        comb = jnp.where(lane_f == i1, gp / (1.0 + r), jnp.where(lane_f == i2, gp * r / (1.0 + r), 0.0))
        cw = jnp.concatenate(
            [jnp.broadcast_to(rsum(jnp.where(lane == e_lo + e, comb, 0.0)), (tm, D_EXPERT))
             for e in range(EXPERTS_PER_GROUP)], axis=1)

        gu = jnp.dot(hf.astype(BF16), wgu_bf[...], preferred_element_type=F32)
        gate, up = gu[:, 0:GROUP_HIDDEN], gu[:, GROUP_HIDDEN:2 * GROUP_HIDDEN]
        act = gate * jax.nn.sigmoid(gate) * up * cw
        ybuf[slot] = jnp.dot(act.astype(BF16), wd_bf[...], preferred_element_type=F32)

    tile = jnp.minimum(i, n_tiles - 1)
    g = gid_ref[tile]
    steady = (i >= 2) & (i + 1 < n_used)

    def steady_step(slot):
        wait_gather(slot)
        wait_scatter(slot)

        @pl.when(g != gid_ref[tile - 1])
        def _():
            cast_weights()

        start_rows(lambda r: gather_row(src_next_ref, r, 1 - slot), True)
        start_rows(lambda r: scatter_row(r, 1 - slot), True)
        run_experts(slot, g)

    for parity in range(2):
        @pl.when(steady & (lax.rem(i, 2) == parity))
        def _(parity=parity):
            steady_step(parity)

    @pl.when(jnp.logical_not(steady))
    def _():
        slot = lax.rem(i, 2)

        @pl.when(i == 0)
        def _():
            start_rows(lambda r: gather_row(src_ref, r, slot), False)
            ybuf[1] = jnp.zeros((tm, D_MODEL), F32)
            n_real = hf_hbm.shape[0]
            spare = [pltpu.make_async_copy(ybuf.at[1], y_hbm.at[pl.ds(n_real + k * tm, tm)], sem_s.at[1])
                     for k in range(N_GROUPS)]
            for cp in spare:
                cp.start()
            for cp in spare:
                cp.wait()

        @pl.when(i < n_used)
        def _():
            wait_gather(slot)

        @pl.when(i + 1 < n_used)
        def _():
            start_rows(lambda r: gather_row(src_next_ref, r, 1 - slot), False)

        @pl.when((i >= 1) & (i <= n_used))
        def _():
            start_rows(lambda r: scatter_row(r, 1 - slot), False)

        @pl.when(i < n_used)
        def _():
            @pl.when(i >= 2)
            def _():
                wait_scatter(slot)

            @pl.when((i == 0) | (g != gid_ref[jnp.maximum(tile - 1, 0)]))
            def _():
                cast_weights()

            run_experts(slot, g)

        @pl.when(i == n_used)
        def _():
            wait_scatter(1 - slot)

            @pl.when(n_used >= 2)
            def _():
                wait_scatter(slot)


def _moe(layer, gid, n_used, src, dst, hf, wg, wu, wd, wrh, wrl, brt):
    T, D = hf.shape
    tm = TM_MOE
    n_tiles = src.shape[0]
    clamp = lambda t: jnp.clip(t, 0, n_tiles - 1)
    idx_spec = lambda shift: pl.BlockSpec(
        (1, 1, tm), lambda i, gid, nu: (clamp(i + shift), 0, 0), memory_space=pltpu.SMEM)
    grp = lambda shape: pl.BlockSpec((None,) + shape, lambda i, gid, nu: (layer, gid[clamp(i)], 0, 0))
    return pl.pallas_call(
        _moe_kernel,
        grid_spec=pltpu.PrefetchScalarGridSpec(
            num_scalar_prefetch=2,
            grid=(n_tiles + 1,),
            in_specs=[idx_spec(0), idx_spec(1), idx_spec(-1),
                      pl.BlockSpec(memory_space=pl.ANY),
                      grp((EXPERTS_PER_GROUP, D, D_EXPERT)), grp((EXPERTS_PER_GROUP, D, D_EXPERT)),
                      grp((EXPERTS_PER_GROUP, D_EXPERT, D)),
                      _layer_spec(layer, (D, LANES)), _layer_spec(layer, (D, LANES)), _layer_spec(layer, (1, LANES))],
            out_specs=pl.BlockSpec(memory_space=pl.ANY),
            scratch_shapes=[pltpu.VMEM((2, tm, D), F32), pltpu.VMEM((2, tm, D), F32),
                            pltpu.VMEM((D, 2 * GROUP_HIDDEN), BF16), pltpu.VMEM((GROUP_HIDDEN, D), BF16),
                            pltpu.SemaphoreType.DMA((2,)), pltpu.SemaphoreType.DMA((2,))]),
        out_shape=jax.ShapeDtypeStruct((n_tiles * tm, D), F32),
        compiler_params=pltpu.CompilerParams(
            dimension_semantics=("arbitrary",), vmem_limit_bytes=VMEM_LIMIT),
        name="moe",
    )(gid, n_used, src, src, dst, hf, wg, wu, wd, wrh, wrl, brt)


def _ple_kernel(x1_ref, y_ref, p_ref, pg_ref, wpg_ref, wp_ref, o_ref):
    x2 = x1_ref[...] + y_ref[...]
    hp = _rms(x2, pg_ref[...]).astype(BF16)
    gate = jax.nn.sigmoid(jnp.dot(hp, wpg_ref[...], preferred_element_type=F32))
    emb = jnp.dot(p_ref[...].astype(BF16), wp_ref[...], preferred_element_type=F32)
    o_ref[...] = x2 + gate * emb


def _ple(layer, x1, y, p, pg, wpg, wp):
    T, D = x1.shape
    tm = TM_PLE
    tok = lambda width: pl.BlockSpec((tm, width), lambda i: (i, 0))
    return pl.pallas_call(
        _ple_kernel,
        grid=(T // tm,),
        in_specs=[tok(D), tok(D), pl.BlockSpec((None, tm, PLE_DIM), lambda i: (layer, i, 0)),
                  _layer_spec(layer, (1, D)), _layer_spec(layer, (D, D)), _layer_spec(layer, (PLE_DIM, D))],
        out_specs=tok(D),
        out_shape=jax.ShapeDtypeStruct((T, D), F32),
        compiler_params=pltpu.CompilerParams(
            dimension_semantics=("parallel",), vmem_limit_bytes=VMEM_LIMIT),
        name="ple",
    )(x1, y, p, pg, wpg, wp)


def _routing_tables(info, counts):
    tm = TM_MOE
    T = info.shape[0] * info.shape[2]
    n_rows = T + N_GROUPS * tm
    n_tiles = n_rows // tm
    group = info[:, 0, :].reshape(T).astype(jnp.int32)
    rank = info[:, 1, :].reshape(T).astype(jnp.int32)
    cnt = counts[0, :N_GROUPS].astype(jnp.int32)
    padded = (cnt + tm - 1) // tm * tm
    ends = jnp.cumsum(padded)
    base = ends - padded
    pos = base[group] + rank
    tok_of_row = jnp.full((n_rows,), -1, jnp.int32).at[pos].set(jnp.arange(T, dtype=jnp.int32))
    real = tok_of_row >= 0
    src = jnp.where(real, tok_of_row, 0)
    pad_index = jnp.cumsum(jnp.where(real, 0, 1)) - 1
    dst = jnp.where(real, tok_of_row, T + pad_index)
    starts = jnp.arange(n_tiles, dtype=jnp.int32) * tm
    gid = jnp.minimum(jnp.sum(starts[:, None] >= ends[None, :], axis=1), N_GROUPS - 1).astype(jnp.int32)
    n_used = (ends[N_GROUPS - 1] // tm).astype(jnp.int32)[None]
    gid = jnp.where(starts < ends[N_GROUPS - 1], gid, gid[jnp.maximum(n_used[0] - 1, 0)])
    shape3 = (n_tiles, 1, tm)
    return gid, n_used, src.reshape(shape3), dst.reshape(shape3)


def _rotary_table(positions):
    half = ROT_DIM // 2
    per_row = LANES // half
    inv_freq = ROPE_THETA ** (-jnp.arange(0, ROT_DIM, 2, dtype=F32) / ROT_DIM)
    B, S = positions.shape
    pos = jnp.repeat(positions.astype(F32).reshape(B, S // per_row, per_row), half, axis=-1)
    ang = pos * jnp.tile(inv_freq, per_row)
    c = jnp.cos(ang).reshape(B, S, half)
    s = jnp.sin(ang).reshape(B, S, half)
    return jnp.concatenate([c, s], axis=-1)


def _rotary_placement():
    half = ROT_DIM // 2
    e = np.zeros((2 * half, 3 * LANES), np.float32)
    one = np.zeros((1, LANES), np.float32)
    for lane in range(LANES):
        d = lane % HEAD_DIM
        if d < half:
            e[d, lane] = 1.0
            e[half + d, LANES + lane] = -1.0
        elif d < ROT_DIM:
            e[d - half, lane] = 1.0
            e[d, 2 * LANES + lane] = 1.0
        else:
            one[0, lane] = 1.0
    return jnp.asarray(e, BF16), jnp.asarray(one, F32)


def _pack_w_in(w):
    n_main = C_FZ
    fz = jnp.pad(w[:, :, n_main:n_main + FOX_HEADS], ((0, 0), (0, 0), (0, LANES - FOX_HEADS)))
    return jnp.concatenate([w[:, :, :n_main], fz, w[:, :, n_main + FOX_HEADS:]], axis=2).astype(BF16)


def kernel(x, p, positions, attn_norm_g, w_in, pool_w, pool_scale, diff_qn_g, diff_kn_g, diff_lambda, diff_subln_g, fox_qn_g, fox_kn_g, fox_forget_b, w_branch, w_out, ffn_norm_g, w_route_group, b_route_group, w_route_expert, b_route_expert, moe_w_gate, moe_w_up, moe_w_down, ple_norm_g, w_ple_gate, w_ple):
    B, S, D = x.shape
    T = B * S
    depth = w_in.shape[0]
    two = lambda a: jnp.concatenate([a, a], axis=-1)
    pad_lanes = lambda a: jnp.pad(a, [(0, 0)] * (a.ndim - 1) + [(0, LANES - a.shape[-1])])
    row = lambda a: a[:, None, :]

    tab = _rotary_table(positions)
    rot_e, rot_one = _rotary_placement()
    w_in_p = _pack_w_in(w_in)
    qkg = jnp.stack([two(diff_qn_g), two(diff_kn_g), two(fox_qn_g), two(fox_kn_g)], axis=1)
    fb = row(pad_lanes(fox_forget_b))
    w_r = jnp.concatenate([w_route_group, w_route_expert], axis=2)
    w_r_hi = w_r.astype(BF16)
    wrh = pad_lanes(w_r_hi)
    wrl = pad_lanes((w_r - w_r_hi.astype(F32)).astype(BF16))
    brt = row(pad_lanes(jnp.concatenate([b_route_group, b_route_expert], axis=1)))
    pool_w_b, w_branch_b, w_out_b = pool_w.astype(BF16), w_branch.astype(BF16), w_out.astype(BF16)
    w_ple_gate_b, w_ple_b = w_ple_gate.astype(BF16), w_ple.astype(BF16)
    p_flat = p.reshape(depth, T, PLE_DIM)

    for l in range(depth):
        lam_init = 0.8 - 0.6 * math.exp(-0.3 * l)
        u, dq, dk, dv, fq, fk, fv, f, ft, gates = _inproj(
            l, x, row(attn_norm_g), w_in_p, tab, rot_e, rot_one, qkg, fb)
        y_diff = _diff_attention(l, dq, dk, dv, diff_lambda, row(diff_subln_g), lam_init)
        y_fox = _fox_attention(fq, fk, fv, f, ft)
        x1, hf, info, counts = _merge(
            l, x, u, y_diff, y_fox, gates, pool_w_b, row(pool_scale), w_branch_b, w_out_b,
            row(ffn_norm_g), wrh, wrl, brt)
        gid, n_used, src, dst = _routing_tables(info, counts)
        y = _moe(l, gid, n_used, src, dst, hf.reshape(T, D), moe_w_gate, moe_w_up, moe_w_down, wrh, wrl, brt)
        x = _ple(l, x1.reshape(T, D), y, p_flat, row(ple_norm_g), w_ple_gate_b, w_ple_b).reshape(B, S, D)
    return x
```

```python
import functools
import math

import jax
import jax.numpy as jnp
import numpy as np
from jax import lax
from jax.experimental import pallas as pl
from jax.experimental.pallas import tpu as pltpu

F32 = jnp.float32
BF16 = jnp.bfloat16

D_MODEL = 1024
HEAD_DIM = 64
LANES = 128
POOL_WINDOWS = (2, 4, 8, 16)
POOL_HALO = 16
BRANCH_WIDTH = 512
N_BRANCH = 3
FOX_HEADS = 8
ROT_DIM = HEAD_DIM // 4
ROPE_THETA = 500000.0
PLE_DIM = 256
N_GROUPS = 4
EXPERTS_PER_GROUP = 4
N_EXPERTS = 16
D_EXPERT = 256
RMS_EPS = 1e-6
NEG = -1e30
LOG2E = math.log2(math.e)

C_POOL, C_DQ, C_DK, C_DV, C_FQ, C_FK, C_FV, C_FZ, C_GZ = (0, 512, 1024, 1536, 2048, 2560, 3072, 3584, 3712)
IN_PACKED = C_GZ + N_BRANCH * D_MODEL

TM_IN = 512
TM_MERGE = 512
TM_MOE = 512
TM_PLE = 512
GROUP_HIDDEN = EXPERTS_PER_GROUP * D_EXPERT
TQ = 256
VMEM_LIMIT = 56 * 1024 * 1024


def _const_spec(shape):
    zeros = (0,) * len(shape)
    return pl.BlockSpec(shape, lambda *_: zeros, pipeline_mode=pl.Buffered(1))


def _layer_spec(layer, shape):
    zeros = (0,) * len(shape)
    return pl.BlockSpec((None,) + tuple(shape), lambda *_: (layer,) + zeros, pipeline_mode=pl.Buffered(1))


def _rms(x, g):
    return x * lax.rsqrt(jnp.mean(x * x, axis=-1, keepdims=True) + RMS_EPS) * g


def _router_logits(hf, wrh_ref, wrl_ref, br_ref):
    hi = hf.astype(BF16)
    lo = (hf - hi.astype(F32)).astype(BF16)
    return (jnp.dot(hi, wrh_ref[...], preferred_element_type=F32)
            + jnp.dot(lo, wrh_ref[...], preferred_element_type=F32)
            + jnp.dot(hi, wrl_ref[...], preferred_element_type=F32)
            + br_ref[...])


def _inproj_kernel(x_ref, g_ref, w_ref, tab_ref, rot_e_ref, rot_one_ref, qkg_ref, fb_ref,
                   u_ref, dq_ref, dk_ref, dv_ref, fq_ref, fk_ref, fv_ref, f_ref, ft_ref, gates_ref,
                   carry_ref):
    i = pl.program_id(1)
    tm = x_ref.shape[1]
    h = _rms(x_ref[0], g_ref[...]).astype(BF16)

    def proj(c0, width):
        return jnp.dot(h, w_ref[:, c0:c0 + width], preferred_element_type=F32)

    lo = lax.broadcasted_iota(jnp.int32, (tm, LANES), 1) < HEAD_DIM
    tab = tab_ref[0]
    tab_hi = tab.astype(BF16)
    tab_lo = (tab - tab_hi.astype(F32)).astype(BF16)
    placed = (jnp.dot(tab_hi, rot_e_ref[...], preferred_element_type=F32)
              + jnp.dot(tab_lo, rot_e_ref[...], preferred_element_type=F32))
    cos = placed[:, 0:LANES] + rot_one_ref[...]
    sa = placed[:, LANES:2 * LANES]
    sb = placed[:, 2 * LANES:3 * LANES]
    scale = HEAD_DIM ** -0.5 * LOG2E

    def head_norm(a, g):
        sq = a * a
        s_lo = jnp.sum(jnp.where(lo, sq, 0.0), axis=-1, keepdims=True)
        s_hi = jnp.sum(jnp.where(lo, 0.0, sq), axis=-1, keepdims=True)
        ss = jnp.where(lo, s_lo, s_hi)
        return a * lax.rsqrt(ss * (1.0 / HEAD_DIM) + RMS_EPS) * g

    def rotary(a):
        return a * cos + pltpu.roll(a, LANES - ROT_DIM // 2, 1) * sa + pltpu.roll(a, ROT_DIM // 2, 1) * sb

    def qk_chunk(c0, out_ref, g, rot, mult):
        acc = proj(c0, 512)
        for j in range(4):
            a = head_norm(acc[:, j * LANES:(j + 1) * LANES], g)
            if rot:
                a = rotary(a)
            if mult != 1.0:
                a = a * mult
            out_ref[0, :, j * LANES:(j + 1) * LANES] = a.astype(BF16)

    u_ref[0] = proj(C_POOL, 512).astype(BF16)
    qk_chunk(C_DQ, dq_ref, qkg_ref[0:1, :], True, scale)
    qk_chunk(C_DK, dk_ref, qkg_ref[1:2, :], True, 1.0)
    dv_ref[0] = proj(C_DV, 512).astype(BF16)
    qk_chunk(C_FQ, fq_ref, qkg_ref[2:3, :], False, scale)
    qk_chunk(C_FK, fk_ref, qkg_ref[3:4, :], False, 1.0)
    fv_ref[0] = proj(C_FV, 512).astype(BF16)

    z = proj(C_FZ, LANES) + fb_ref[...]
    logf = jnp.minimum(z, 0.0) - jnp.log1p(jnp.exp(-jnp.abs(z)))

    @pl.when(i == 0)
    def _():
        carry_ref[...] = jnp.zeros_like(carry_ref)

    row = lax.broadcasted_iota(jnp.int32, (tm, tm), 0)
    col = lax.broadcasted_iota(jnp.int32, (tm, tm), 1)
    tri = jnp.where(row >= col, 1.0, 0.0).astype(BF16)
    hi = logf.astype(BF16)
    lo_part = (logf - hi.astype(F32)).astype(BF16)
    fcum = (jnp.dot(tri, hi, preferred_element_type=F32)
            + jnp.dot(tri, lo_part, preferred_element_type=F32)
            + carry_ref[0:1, :])
    carry_ref[...] = jnp.broadcast_to(fcum[tm - 1:tm, :], carry_ref.shape)
    f_ref[0] = fcum
    ft_ref[0] = fcum.T[0:FOX_HEADS, :]

    for c in range(N_BRANCH * D_MODEL // 512):
        gates_ref[0, :, c * 512:(c + 1) * 512] = jax.nn.sigmoid(proj(C_GZ + c * 512, 512)).astype(BF16)


def _inproj(layer, x, g, w, tab, rot_e, rot_one, qkg, fb):
    B, S, D = x.shape
    tm = TM_IN
    tok = lambda width: pl.BlockSpec((1, tm, width), lambda b, i: (b, i, 0))
    out_shape = (
        [jax.ShapeDtypeStruct((B, S, 512), BF16)] * 7
        + [jax.ShapeDtypeStruct((B, S, LANES), F32),
           jax.ShapeDtypeStruct((B, FOX_HEADS, S), F32),
           jax.ShapeDtypeStruct((B, S, N_BRANCH * D), BF16)]
    )
    out_specs = (
        [tok(512)] * 7
        + [tok(LANES),
           pl.BlockSpec((1, FOX_HEADS, tm), lambda b, i: (b, 0, i)),
           tok(N_BRANCH * D)]
    )
    return pl.pallas_call(
        _inproj_kernel,
        grid=(B, S // tm),
        in_specs=[tok(D), _layer_spec(layer, (1, D)), _layer_spec(layer, (D, IN_PACKED)),
                  tok(2 * (ROT_DIM // 2)), _const_spec((2 * (ROT_DIM // 2), 3 * LANES)), _const_spec((1, LANES)),
                  _layer_spec(layer, (4, LANES)), _layer_spec(layer, (1, LANES))],
        out_specs=out_specs,
        out_shape=out_shape,
        scratch_shapes=[pltpu.VMEM((8, LANES), F32)],
        compiler_params=pltpu.CompilerParams(
            dimension_semantics=("parallel", "arbitrary"), vmem_limit_bytes=VMEM_LIMIT),
        name="inproj",
    )(x, g, w, tab, rot_e, rot_one, qkg, fb)


def _nt_dot(a, b):
    return lax.dot_general(a, b, (((1,), (1,)), ((), ())), preferred_element_type=F32)


def _softmax_rows(score_chunk, n_chunks, s_scr, p_scr):
    tq = TQ
    mt = jnp.full((tq, LANES), NEG, F32)
    for c in range(n_chunks):
        s = score_chunk(c)
        s_scr[:, c * tq:(c + 1) * tq] = s
        for w in range(tq // LANES):
            mt = jnp.maximum(mt, s[:, w * LANES:(w + 1) * LANES])
    m = jnp.max(mt, axis=1, keepdims=True)
    for c in range(n_chunks):
        p_scr[:, c * tq:(c + 1) * tq] = jnp.exp2(s_scr[:, c * tq:(c + 1) * tq] - m).astype(BF16)


def _diff_kernel(lam_init, q_ref, k_ref, v_ref, lam_ref, g_ref, o_ref, s_scr, p_scr, v2_scr):
    S = q_ref.shape[1]
    tq = TQ
    lo = lax.broadcasted_iota(jnp.int32, (tq, LANES), 1) < HEAD_DIM
    causal = (lax.broadcasted_iota(jnp.int32, (tq, tq), 1) <= lax.broadcasted_iota(jnp.int32, (tq, tq), 0))
    lp = lam_ref[...]
    lam = (jnp.exp(jnp.sum(lp[0:1, :] * lp[1:2, :], axis=1, keepdims=True))
           - jnp.exp(jnp.sum(lp[2:3, :] * lp[3:4, :], axis=1, keepdims=True)) + lam_init)
    v2_scr[:, 0:LANES] = v_ref[0]
    v2_scr[:, LANES:2 * LANES] = jnp.ones((S, LANES), BF16)

    for qi in range(S // tq):
        q0, kend = qi * tq, (qi + 1) * tq
        q = q_ref[0, q0:kend, :]
        zero = jnp.zeros_like(q)
        parts = []
        for comp in range(2):
            qm = jnp.where(lo, q, zero) if comp == 0 else jnp.where(lo, zero, q)

            def score_chunk(c, qm=qm, qi=qi):
                s = _nt_dot(qm, k_ref[0, c * tq:(c + 1) * tq, :])
                return jnp.where(causal, s, NEG) if c == qi else s

            _softmax_rows(score_chunk, qi + 1, s_scr.at[comp], p_scr.at[comp])
            pv = jnp.dot(p_scr[comp, :, 0:kend], v2_scr[0:kend, :], preferred_element_type=F32)
            parts.append(pv[:, 0:LANES] / pv[:, LANES:2 * LANES])
        o = parts[0] - lam * parts[1]
        o = _rms(o, g_ref[...]) * (1.0 - lam_init)
        o_ref[0, q0:kend, :] = o.astype(BF16)


def _diff_attention(layer, dq, dk, dv, lam_p, subln_g, lam_init):
    B, S, _ = dq.shape
    blk = pl.BlockSpec((1, S, LANES), lambda b, h: (b, 0, h))
    return pl.pallas_call(
        functools.partial(_diff_kernel, lam_init),
        grid=(B, 4),
        in_specs=[blk, blk, blk, _layer_spec(layer, (4, HEAD_DIM)), _layer_spec(layer, (1, LANES))],
        out_specs=blk,
        out_shape=jax.ShapeDtypeStruct((B, S, 512), BF16),
        scratch_shapes=[pltpu.VMEM((2, TQ, S), F32), pltpu.VMEM((2, TQ, S), BF16),
                        pltpu.VMEM((S, 2 * LANES), BF16)],
        compiler_params=pltpu.CompilerParams(
            dimension_semantics=("parallel", "parallel"), vmem_limit_bytes=VMEM_LIMIT),
        name="diff_attn",
    )(dq, dk, dv, lam_p, subln_g)


def _fox_kernel(q_ref, k_ref, v_ref, f_ref, ft_ref, o_ref, s_scr, p_scr, v2_scr):
    S = q_ref.shape[1]
    tq = TQ
    j = pl.program_id(1)
    lane = lax.broadcasted_iota(jnp.int32, (tq, LANES), 1)
    lo = lane < HEAD_DIM
    causal = (lax.broadcasted_iota(jnp.int32, (tq, tq), 1) <= lax.broadcasted_iota(jnp.int32, (tq, tq), 0))
    v = v_ref[0]
    lo_s = lax.broadcasted_iota(jnp.int32, (S, LANES), 1) < HEAD_DIM
    one = jnp.ones_like(v)
    v2_scr[0] = jnp.where(lo_s, v, one)
    v2_scr[1] = jnp.where(lo_s, one, v)

    for qi in range(S // tq):
        q0, kend = qi * tq, (qi + 1) * tq
        q = q_ref[0, q0:kend, :]
        zero = jnp.zeros_like(q)
        f_tile = f_ref[0, q0:kend, :] * LOG2E
        pvs = []
        for hh in range(2):
            qm = jnp.where(lo, q, zero) if hh == 0 else jnp.where(lo, zero, q)
            f_t = jnp.sum(jnp.where(lane == 2 * j + hh, f_tile, 0.0), axis=1, keepdims=True)

            def score_chunk(c, qm=qm, f_t=f_t, hh=hh, qi=qi):
                f_s = ft_ref[0, pl.ds(2 * j + hh, 1), c * tq:(c + 1) * tq] * LOG2E
                s = _nt_dot(qm, k_ref[0, c * tq:(c + 1) * tq, :]) + (f_t - f_s)
                return jnp.where(causal, s, NEG) if c == qi else s

            _softmax_rows(score_chunk, qi + 1, s_scr.at[hh], p_scr.at[hh])
            pvs.append(jnp.dot(p_scr[hh, :, 0:kend], v2_scr[hh, 0:kend, :], preferred_element_type=F32))
        o = jnp.where(lo, pvs[0] / pltpu.roll(pvs[0], HEAD_DIM, 1), pvs[1] / pltpu.roll(pvs[1], HEAD_DIM, 1))
        o_ref[0, q0:kend, :] = o.astype(BF16)


def _fox_attention(fq, fk, fv, f, ft):
    B, S, _ = fq.shape
    blk = pl.BlockSpec((1, S, LANES), lambda b, j: (b, 0, j))
    return pl.pallas_call(
        _fox_kernel,
        grid=(B, 4),
        in_specs=[blk, blk, blk,
                  pl.BlockSpec((1, S, LANES), lambda b, j: (b, 0, 0)),
                  pl.BlockSpec((1, FOX_HEADS, S), lambda b, j: (b, 0, 0))],
        out_specs=blk,
        out_shape=jax.ShapeDtypeStruct((B, S, 512), BF16),
        scratch_shapes=[pltpu.VMEM((2, TQ, S), F32), pltpu.VMEM((2, TQ, S), BF16),
                        pltpu.VMEM((2, S, LANES), BF16)],
        compiler_params=pltpu.CompilerParams(
            dimension_semantics=("parallel", "parallel"), vmem_limit_bytes=VMEM_LIMIT),
        name="fox_attn",
    )(fq, fk, fv, f, ft)


def _merge_kernel(x_ref, u_ref, uh_ref, yd_ref, yf_ref, gates_ref, pw_ref, ps_ref, wb_ref, wo_ref,
                  ng_ref, wrh_ref, wrl_ref, br_ref,
                  x1_ref, hf_ref, info_ref, count_ref):
    i = pl.program_id(1)
    tm = x_ref.shape[1]

    halo = uh_ref[0].astype(F32) * jnp.where(i > 0, 1.0, 0.0)
    ext = jnp.concatenate([halo, u_ref[0].astype(F32)], axis=0)
    t = i * tm + lax.broadcasted_iota(jnp.int32, (tm, 1), 0)
    ys = []
    for g, w in enumerate(POOL_WINDOWS):
        e = ext[:, g * LANES:(g + 1) * LANES]
        win = e
        step = 1
        while step < w:
            win = win + pltpu.roll(win, step, 0)
            step *= 2
        cnt = jnp.minimum(t + 1, w).astype(F32)
        d = win[POOL_HALO:, :] / cnt - e[POOL_HALO:, :]
        ys.append(jnp.dot(d.astype(BF16), pw_ref[g], preferred_element_type=F32))
    y_pool = jnp.concatenate(ys, axis=1) * ps_ref[...]

    branches = (y_pool.astype(BF16), yd_ref[0], yf_ref[0])
    merged = jnp.zeros((tm, D_MODEL), F32)
    for n in range(N_BRANCH):
        br = jnp.dot(branches[n], wb_ref[n], preferred_element_type=F32)
        merged = merged + gates_ref[0, :, n * D_MODEL:(n + 1) * D_MODEL].astype(F32) * br
    x1 = x_ref[0] + jnp.dot(merged.astype(BF16), wo_ref[...], preferred_element_type=F32)
    x1_ref[0] = x1

    hf = _rms(x1, ng_ref[...])
    hf_ref[0] = hf
    lg = _router_logits(hf, wrh_ref, wrl_ref, br_ref)
    lane = lax.broadcasted_iota(jnp.int32, (tm, LANES), 1)
    lane_f = lane.astype(F32)
    gmask = lane < N_GROUPS
    gl = jnp.where(gmask, lg, NEG)
    gmax = jnp.max(gl, axis=1, keepdims=True)
    gi = jnp.min(jnp.where(gmask & (gl == gmax), lane_f, float(LANES)), axis=1, keepdims=True)

    @pl.when((pl.program_id(0) == 0) & (i == 0))
    def _():
        count_ref[...] = jnp.zeros_like(count_ref)

    onehot = jnp.where(lane_f == gi, 1.0, 0.0)
    row = lax.broadcasted_iota(jnp.int32, (tm, tm), 0)
    col = lax.broadcasted_iota(jnp.int32, (tm, tm), 1)
    tri = jnp.where(row >= col, 1.0, 0.0).astype(BF16)
    incl = jnp.dot(tri, onehot.astype(BF16), preferred_element_type=F32)
    before = count_ref[0:1, :]
    rank = jnp.sum(onehot * (incl - onehot + before), axis=1, keepdims=True)
    count_ref[...] = jnp.broadcast_to(before + incl[tm - 1:tm, :], count_ref.shape)
    info_ref[0] = jnp.where(lane == 0, gi, jnp.where(lane == 1, rank, 0.0)).T[0:8, :]


def _merge(layer, x, u, yd, yf, gates, pw, ps, wb, wo, ng, wrh, wrl, brt):
    B, S, D = x.shape
    tm = TM_MERGE
    tok = lambda width: pl.BlockSpec((1, tm, width), lambda b, i: (b, i, 0))
    halo_blocks = tm // POOL_HALO
    halo = pl.BlockSpec((1, POOL_HALO, 512), lambda b, i: (b, jnp.maximum(i * halo_blocks - 1, 0), 0))
    return pl.pallas_call(
        _merge_kernel,
        grid=(B, S // tm),
        in_specs=[tok(D), tok(512), halo, tok(512), tok(512), tok(N_BRANCH * D),
                  _layer_spec(layer, (4, LANES, LANES)), _layer_spec(layer, (1, 512)),
                  _layer_spec(layer, (N_BRANCH, BRANCH_WIDTH, D)), _layer_spec(layer, (D, D)),
                  _layer_spec(layer, (1, D)), _layer_spec(layer, (D, LANES)), _layer_spec(layer, (D, LANES)),
                  _layer_spec(layer, (1, LANES))],
        out_specs=[tok(D), tok(D), pl.BlockSpec((1, 8, tm), lambda b, i: (b, 0, i)),
                   pl.BlockSpec((8, LANES), lambda b, i: (0, 0))],
        out_shape=[jax.ShapeDtypeStruct((B, S, D), F32),
                   jax.ShapeDtypeStruct((B, S, D), F32),
                   jax.ShapeDtypeStruct((B, 8, S), F32),
                   jax.ShapeDtypeStruct((8, LANES), F32)],
        compiler_params=pltpu.CompilerParams(
            dimension_semantics=("arbitrary", "arbitrary"), vmem_limit_bytes=VMEM_LIMIT),
        name="merge",
    )(x, u, u, yd, yf, gates, pw, ps, wb, wo, ng, wrh, wrl, brt)


def _moe_kernel(gid_ref, nused_ref, src_ref, src_next_ref, dst_prev_ref,
                hf_hbm, wg_ref, wu_ref, wd_ref, wrh_ref, wrl_ref, br_ref,
                y_hbm,
                xbuf, ybuf, wgu_bf, wd_bf, sem_g, sem_s):
    i = pl.program_id(0)
    n_used = nused_ref[0]
    tm = TM_MOE
    n_tiles = pl.num_programs(0) - 1

    def gather_row(idx_ref, r, to_slot):
        return pltpu.make_async_copy(hf_hbm.at[pl.ds(idx_ref[0, 0, r], 1)], xbuf.at[to_slot, pl.ds(r, 1)],
                                     sem_g.at[to_slot])

    def scatter_row(r, from_slot):
        return pltpu.make_async_copy(ybuf.at[from_slot, pl.ds(r, 1)], y_hbm.at[pl.ds(dst_prev_ref[0, 0, r], 1)],
                                     sem_s.at[from_slot])

    def start_rows(make_copy, unrolled):
        if unrolled:
            for r in range(tm):
                make_copy(r).start(priority=r % 2)
        else:
            def body(r, c):
                make_copy(r).start()
                return c
            lax.fori_loop(0, tm, body, 0, unroll=8)

    def wait_gather(of_slot):
        pltpu.make_async_copy(hf_hbm.at[pl.ds(0, tm)], xbuf.at[of_slot], sem_g.at[of_slot]).wait()

    def wait_scatter(of_slot):
        pltpu.make_async_copy(ybuf.at[of_slot], y_hbm.at[pl.ds(0, tm)], sem_s.at[of_slot]).wait()

    def cast_weights():
        for e in range(EXPERTS_PER_GROUP):
            wgu_bf[:, e * D_EXPERT:(e + 1) * D_EXPERT] = wg_ref[e].astype(BF16)
            wgu_bf[:, GROUP_HIDDEN + e * D_EXPERT:GROUP_HIDDEN + (e + 1) * D_EXPERT] = wu_ref[e].astype(BF16)
            wd_bf[e * D_EXPERT:(e + 1) * D_EXPERT, :] = wd_ref[e].astype(BF16)

    def run_experts(slot, g):
        hf = xbuf[slot]

        lg = _router_logits(hf, wrh_ref, wrl_ref, br_ref)
        lane = lax.broadcasted_iota(jnp.int32, (tm, LANES), 1)
        lane_f = lane.astype(F32)
        far = float(LANES)
        rmax = lambda a: jnp.max(a, axis=1, keepdims=True)
        rmin = lambda a: jnp.min(a, axis=1, keepdims=True)
        rsum = lambda a: jnp.sum(a, axis=1, keepdims=True)
        lg_g = rsum(jnp.where(lane == g, lg, 0.0))
        gp = 1.0 / rsum(jnp.where(lane < N_GROUPS, jnp.exp(lg - lg_g), 0.0))
        e_lo = N_GROUPS + EXPERTS_PER_GROUP * g
        emask = (lane >= e_lo) & (lane < e_lo + EXPERTS_PER_GROUP)
        el = jnp.where(emask, lg, NEG)
        m1 = rmax(el)
        i1 = rmin(jnp.where(emask & (el == m1), lane_f, far))
        rest = emask & (lane_f != i1)
        el2 = jnp.where(rest, lg, NEG)
        m2 = rmax(el2)
        i2 = rmin(jnp.where(rest & (el2 == m2), lane_f, far))
        r = jnp.exp(m2 - m1)
        comb = jnp.where(lane_f == i1, gp / (1.0 + r), jnp.where(lane_f == i2, gp * r / (1.0 + r), 0.0))
        cw = jnp.concatenate(
            [jnp.broadcast_to(rsum(jnp.where(lane == e_lo + e, comb, 0.0)), (tm, D_EXPERT))
             for e in range(EXPERTS_PER_GROUP)], axis=1)

        gu = jnp.dot(hf.astype(BF16), wgu_bf[...], preferred_element_type=F32)
        gate, up = gu[:, 0:GROUP_HIDDEN], gu[:, GROUP_HIDDEN:2 * GROUP_HIDDEN]
        act = gate * jax.nn.sigmoid(gate) * up * cw
        ybuf[slot] = jnp.dot(act.astype(BF16), wd_bf[...], preferred_element_type=F32)

    tile = jnp.minimum(i, n_tiles - 1)
    g = gid_ref[tile]
    steady = (i >= 2) & (i + 1 < n_used)

    def steady_step(slot):
        wait_gather(slot)
        wait_scatter(slot)

        @pl.when(g != gid_ref[tile - 1])
        def _():
            cast_weights()

        start_rows(lambda r: gather_row(src_next_ref, r, 1 - slot), True)
        start_rows(lambda r: scatter_row(r, 1 - slot), True)
        run_experts(slot, g)

    for parity in range(2):
        @pl.when(steady & (lax.rem(i, 2) == parity))
        def _(parity=parity):
            steady_step(parity)

    @pl.when(jnp.logical_not(steady))
    def _():
        slot = lax.rem(i, 2)

        @pl.when(i == 0)
        def _():
            start_rows(lambda r: gather_row(src_ref, r, slot), False)
            ybuf[1] = jnp.zeros((tm, D_MODEL), F32)
            n_real = hf_hbm.shape[0]
            spare = [pltpu.make_async_copy(ybuf.at[1], y_hbm.at[pl.ds(n_real + k * tm, tm)], sem_s.at[1])
                     for k in range(N_GROUPS)]
            for cp in spare:
                cp.start()
            for cp in spare:
                cp.wait()

        @pl.when(i < n_used)
        def _():
            wait_gather(slot)

        @pl.when(i + 1 < n_used)
        def _():
            start_rows(lambda r: gather_row(src_next_ref, r, 1 - slot), False)

        @pl.when((i >= 1) & (i <= n_used))
        def _():
            start_rows(lambda r: scatter_row(r, 1 - slot), False)

        @pl.when(i < n_used)
        def _():
            @pl.when(i >= 2)
            def _():
                wait_scatter(slot)

            @pl.when((i == 0) | (g != gid_ref[jnp.maximum(tile - 1, 0)]))
            def _():
                cast_weights()

            run_experts(slot, g)

        @pl.when(i == n_used)
        def _():
            wait_scatter(1 - slot)

            @pl.when(n_used >= 2)
            def _():
                wait_scatter(slot)


def _moe(layer, gid, n_used, src, dst, hf, wg, wu, wd, wrh, wrl, brt):
    T, D = hf.shape
    tm = TM_MOE
    n_tiles = src.shape[0]
    clamp = lambda t: jnp.clip(t, 0, n_tiles - 1)
    idx_spec = lambda shift: pl.BlockSpec(
        (1, 1, tm), lambda i, gid, nu: (clamp(i + shift), 0, 0), memory_space=pltpu.SMEM)
    grp = lambda shape: pl.BlockSpec((None,) + shape, lambda i, gid, nu: (layer, gid[clamp(i)], 0, 0))
    return pl.pallas_call(
        _moe_kernel,
        grid_spec=pltpu.PrefetchScalarGridSpec(
            num_scalar_prefetch=2,
            grid=(n_tiles + 1,),
            in_specs=[idx_spec(0), idx_spec(1), idx_spec(-1),
                      pl.BlockSpec(memory_space=pl.ANY),
                      grp((EXPERTS_PER_GROUP, D, D_EXPERT)), grp((EXPERTS_PER_GROUP, D, D_EXPERT)),
                      grp((EXPERTS_PER_GROUP, D_EXPERT, D)),
                      _layer_spec(layer, (D, LANES)), _layer_spec(layer, (D, LANES)), _layer_spec(layer, (1, LANES))],
            out_specs=pl.BlockSpec(memory_space=pl.ANY),
            scratch_shapes=[pltpu.VMEM((2, tm, D), F32), pltpu.VMEM((2, tm, D), F32),
                            pltpu.VMEM((D, 2 * GROUP_HIDDEN), BF16), pltpu.VMEM((GROUP_HIDDEN, D), BF16),
                            pltpu.SemaphoreType.DMA((2,)), pltpu.SemaphoreType.DMA((2,))]),
        out_shape=jax.ShapeDtypeStruct((n_tiles * tm, D), F32),
        compiler_params=pltpu.CompilerParams(
            dimension_semantics=("arbitrary",), vmem_limit_bytes=VMEM_LIMIT),
        name="moe",
    )(gid, n_used, src, src, dst, hf, wg, wu, wd, wrh, wrl, brt)


def _ple_kernel(x1_ref, y_ref, p_ref, pg_ref, wpg_ref, wp_ref, o_ref):
    x2 = x1_ref[...] + y_ref[...]
    hp = _rms(x2, pg_ref[...]).astype(BF16)
    gate = jax.nn.sigmoid(jnp.dot(hp, wpg_ref[...], preferred_element_type=F32))
    emb = jnp.dot(p_ref[...].astype(BF16), wp_ref[...], preferred_element_type=F32)
    o_ref[...] = x2 + gate * emb


def _ple(layer, x1, y, p, pg, wpg, wp):
    T, D = x1.shape
    tm = TM_PLE
    tok = lambda width: pl.BlockSpec((tm, width), lambda i: (i, 0))
    return pl.pallas_call(
        _ple_kernel,
        grid=(T // tm,),
        in_specs=[tok(D), tok(D), pl.BlockSpec((None, tm, PLE_DIM), lambda i: (layer, i, 0)),
                  _layer_spec(layer, (1, D)), _layer_spec(layer, (D, D)), _layer_spec(layer, (PLE_DIM, D))],
        out_specs=tok(D),
        out_shape=jax.ShapeDtypeStruct((T, D), F32),
        compiler_params=pltpu.CompilerParams(
            dimension_semantics=("parallel",), vmem_limit_bytes=VMEM_LIMIT),
        name="ple",
    )(x1, y, p, pg, wpg, wp)


def _routing_tables(info, counts):
    tm = TM_MOE
    T = info.shape[0] * info.shape[2]
    n_rows = T + N_GROUPS * tm
    n_tiles = n_rows // tm
    group = info[:, 0, :].reshape(T).astype(jnp.int32)
    rank = info[:, 1, :].reshape(T).astype(jnp.int32)
    cnt = counts[0, :N_GROUPS].astype(jnp.int32)
    padded = (cnt + tm - 1) // tm * tm
    ends = jnp.cumsum(padded)
    base = ends - padded
    pos = base[group] + rank
    tok_of_row = jnp.full((n_rows,), -1, jnp.int32).at[pos].set(
        jnp.arange(T, dtype=jnp.int32), unique_indices=True, mode="promise_in_bounds")
    real = tok_of_row >= 0
    src = jnp.where(real, tok_of_row, 0)
    pad_index = jnp.cumsum(jnp.where(real, 0, 1)) - 1
    dst = jnp.where(real, tok_of_row, T + pad_index)
    starts = jnp.arange(n_tiles, dtype=jnp.int32) * tm
    gid = jnp.minimum(jnp.sum(starts[:, None] >= ends[None, :], axis=1), N_GROUPS - 1).astype(jnp.int32)
    n_used = (ends[N_GROUPS - 1] // tm).astype(jnp.int32)[None]
    gid = jnp.where(starts < ends[N_GROUPS - 1], gid, gid[jnp.maximum(n_used[0] - 1, 0)])
    shape3 = (n_tiles, 1, tm)
    return gid, n_used, src.reshape(shape3), dst.reshape(shape3)


def _rotary_table(positions):
    half = ROT_DIM // 2
    per_row = LANES // half
    inv_freq = ROPE_THETA ** (-jnp.arange(0, ROT_DIM, 2, dtype=F32) / ROT_DIM)
    B, S = positions.shape
    pos = jnp.repeat(positions.astype(F32).reshape(B, S // per_row, per_row), half, axis=-1)
    ang = pos * jnp.tile(inv_freq, per_row)
    c = jnp.cos(ang).reshape(B, S, half)
    s = jnp.sin(ang).reshape(B, S, half)
    return jnp.concatenate([c, s], axis=-1)


def _rotary_placement():
    half = ROT_DIM // 2
    e = np.zeros((2 * half, 3 * LANES), np.float32)
    one = np.zeros((1, LANES), np.float32)
    for lane in range(LANES):
        d = lane % HEAD_DIM
        if d < half:
            e[d, lane] = 1.0
            e[half + d, LANES + lane] = -1.0
        elif d < ROT_DIM:
            e[d - half, lane] = 1.0
            e[d, 2 * LANES + lane] = 1.0
        else:
            one[0, lane] = 1.0
    return jnp.asarray(e, BF16), jnp.asarray(one, F32)


def _pack_w_in(w):
    n_main = C_FZ
    fz = jnp.pad(w[:, :, n_main:n_main + FOX_HEADS], ((0, 0), (0, 0), (0, LANES - FOX_HEADS)))
    return jnp.concatenate([w[:, :, :n_main], fz, w[:, :, n_main + FOX_HEADS:]], axis=2).astype(BF16)


def kernel(x, p, positions, attn_norm_g, w_in, pool_w, pool_scale, diff_qn_g, diff_kn_g, diff_lambda, diff_subln_g, fox_qn_g, fox_kn_g, fox_forget_b, w_branch, w_out, ffn_norm_g, w_route_group, b_route_group, w_route_expert, b_route_expert, moe_w_gate, moe_w_up, moe_w_down, ple_norm_g, w_ple_gate, w_ple):
    B, S, D = x.shape
    T = B * S
    depth = w_in.shape[0]
    two = lambda a: jnp.concatenate([a, a], axis=-1)
    pad_lanes = lambda a: jnp.pad(a, [(0, 0)] * (a.ndim - 1) + [(0, LANES - a.shape[-1])])
    row = lambda a: a[:, None, :]

    tab = _rotary_table(positions)
    rot_e, rot_one = _rotary_placement()
    w_in_p = _pack_w_in(w_in)
    qkg = jnp.stack([two(diff_qn_g), two(diff_kn_g), two(fox_qn_g), two(fox_kn_g)], axis=1)
    fb = row(pad_lanes(fox_forget_b))
    w_r = jnp.concatenate([w_route_group, w_route_expert], axis=2)
    w_r_hi = w_r.astype(BF16)
    wrh = pad_lanes(w_r_hi)
    wrl = pad_lanes((w_r - w_r_hi.astype(F32)).astype(BF16))
    brt = row(pad_lanes(jnp.concatenate([b_route_group, b_route_expert], axis=1)))
    pool_w_b, w_branch_b, w_out_b = pool_w.astype(BF16), w_branch.astype(BF16), w_out.astype(BF16)
    w_ple_gate_b, w_ple_b = w_ple_gate.astype(BF16), w_ple.astype(BF16)
    p_flat = p.reshape(depth, T, PLE_DIM)

    for l in range(depth):
        lam_init = 0.8 - 0.6 * math.exp(-0.3 * l)
        u, dq, dk, dv, fq, fk, fv, f, ft, gates = _inproj(
            l, x, row(attn_norm_g), w_in_p, tab, rot_e, rot_one, qkg, fb)
        y_diff = _diff_attention(l, dq, dk, dv, diff_lambda, row(diff_subln_g), lam_init)
        y_fox = _fox_attention(fq, fk, fv, f, ft)
        x1, hf, info, counts = _merge(
            l, x, u, y_diff, y_fox, gates, pool_w_b, row(pool_scale), w_branch_b, w_out_b,
            row(ffn_norm_g), wrh, wrl, brt)
        gid, n_used, src, dst = _routing_tables(info, counts)
        y = _moe(l, gid, n_used, src, dst, hf.reshape(T, D), moe_w_gate, moe_w_up, moe_w_down, wrh, wrl, brt)
        x = _ple(l, x1.reshape(T, D), y, p_flat, row(ple_norm_g), w_ple_gate_b, w_ple_b).reshape(B, S, D)
    return x
```

```python
import functools
import math

import jax
import jax.numpy as jnp
import numpy as np
from jax import lax
from jax.experimental import pallas as pl
from jax.experimental.pallas import tpu as pltpu

F32 = jnp.float32
BF16 = jnp.bfloat16

D_MODEL = 1024
HEAD_DIM = 64
LANES = 128
POOL_WINDOWS = (2, 4, 8, 16)
POOL_HALO = 16
BRANCH_WIDTH = 512
N_BRANCH = 3
FOX_HEADS = 8
ROT_DIM = HEAD_DIM // 4
ROPE_THETA = 500000.0
PLE_DIM = 256
N_GROUPS = 4
EXPERTS_PER_GROUP = 4
N_EXPERTS = 16
D_EXPERT = 256
RMS_EPS = 1e-6
NEG = -1e30
LOG2E = math.log2(math.e)

C_POOL, C_DQ, C_DK, C_DV, C_FQ, C_FK, C_FV, C_FZ, C_GZ = (0, 512, 1024, 1536, 2048, 2560, 3072, 3584, 3712)
IN_PACKED = C_GZ + N_BRANCH * D_MODEL

TM_IN = 512
TM_MERGE = 512
TM_MOE = 512
TM_PLE = 512
GROUP_HIDDEN = EXPERTS_PER_GROUP * D_EXPERT
TQ = 256
VMEM_LIMIT = 56 * 1024 * 1024


def _const_spec(shape):
    zeros = (0,) * len(shape)
    return pl.BlockSpec(shape, lambda *_: zeros, pipeline_mode=pl.Buffered(1))


def _layer_spec(layer, shape):
    zeros = (0,) * len(shape)
    return pl.BlockSpec((None,) + tuple(shape), lambda *_: (layer,) + zeros, pipeline_mode=pl.Buffered(1))


def _rms(x, g):
    return x * lax.rsqrt(jnp.mean(x * x, axis=-1, keepdims=True) + RMS_EPS) * g


def _router_logits(hf, wrh_ref, wrl_ref, br_ref):
    hi = hf.astype(BF16)
    lo = (hf - hi.astype(F32)).astype(BF16)
    return (jnp.dot(hi, wrh_ref[...], preferred_element_type=F32)
            + jnp.dot(lo, wrh_ref[...], preferred_element_type=F32)
            + jnp.dot(hi, wrl_ref[...], preferred_element_type=F32)
            + br_ref[...])


def _inproj_kernel(x_ref, g_ref, w_ref, tab_ref, rot_e_ref, rot_one_ref, qkg_ref, fb_ref,
                   u_ref, dq_ref, dk_ref, dv_ref, fq_ref, fk_ref, fv_ref, f_ref, ft_ref, gates_ref,
                   carry_ref):
    i = pl.program_id(1)
    tm = x_ref.shape[1]
    h = _rms(x_ref[0], g_ref[...]).astype(BF16)

    def proj(c0, width):
        return jnp.dot(h, w_ref[:, c0:c0 + width], preferred_element_type=F32)

    lo = lax.broadcasted_iota(jnp.int32, (tm, LANES), 1) < HEAD_DIM
    tab = tab_ref[0]
    tab_hi = tab.astype(BF16)
    tab_lo = (tab - tab_hi.astype(F32)).astype(BF16)
    placed = (jnp.dot(tab_hi, rot_e_ref[...], preferred_element_type=F32)
              + jnp.dot(tab_lo, rot_e_ref[...], preferred_element_type=F32))
    cos = placed[:, 0:LANES] + rot_one_ref[...]
    sa = placed[:, LANES:2 * LANES]
    sb = placed[:, 2 * LANES:3 * LANES]
    scale = HEAD_DIM ** -0.5 * LOG2E

    def head_norm(a, g):
        sq = a * a
        s_lo = jnp.sum(jnp.where(lo, sq, 0.0), axis=-1, keepdims=True)
        s_hi = jnp.sum(jnp.where(lo, 0.0, sq), axis=-1, keepdims=True)
        ss = jnp.where(lo, s_lo, s_hi)
        return a * lax.rsqrt(ss * (1.0 / HEAD_DIM) + RMS_EPS) * g

    def rotary(a):
        return a * cos + pltpu.roll(a, LANES - ROT_DIM // 2, 1) * sa + pltpu.roll(a, ROT_DIM // 2, 1) * sb

    def qk_chunk(c0, out_ref, g, rot, mult):
        acc = proj(c0, 512)
        for j in range(4):
            a = head_norm(acc[:, j * LANES:(j + 1) * LANES], g)
            if rot:
                a = rotary(a)
            if mult != 1.0:
                a = a * mult
            out_ref[0, :, j * LANES:(j + 1) * LANES] = a.astype(BF16)

    u_ref[0] = proj(C_POOL, 512).astype(BF16)
    qk_chunk(C_DQ, dq_ref, qkg_ref[0:1, :], True, scale)
    qk_chunk(C_DK, dk_ref, qkg_ref[1:2, :], True, 1.0)
    dv_ref[0] = proj(C_DV, 512).astype(BF16)
    qk_chunk(C_FQ, fq_ref, qkg_ref[2:3, :], False, scale)
    qk_chunk(C_FK, fk_ref, qkg_ref[3:4, :], False, 1.0)
    fv_ref[0] = proj(C_FV, 512).astype(BF16)

    z = proj(C_FZ, LANES) + fb_ref[...]
    logf = jnp.minimum(z, 0.0) - jnp.log1p(jnp.exp(-jnp.abs(z)))

    @pl.when(i == 0)
    def _():
        carry_ref[...] = jnp.zeros_like(carry_ref)

    row = lax.broadcasted_iota(jnp.int32, (tm, tm), 0)
    col = lax.broadcasted_iota(jnp.int32, (tm, tm), 1)
    tri = jnp.where(row >= col, 1.0, 0.0).astype(BF16)
    hi = logf.astype(BF16)
    lo_part = (logf - hi.astype(F32)).astype(BF16)
    fcum = (jnp.dot(tri, hi, preferred_element_type=F32)
            + jnp.dot(tri, lo_part, preferred_element_type=F32)
            + carry_ref[0:1, :])
    carry_ref[...] = jnp.broadcast_to(fcum[tm - 1:tm, :], carry_ref.shape)
    f_ref[0] = fcum
    ft_ref[0] = fcum.T[0:FOX_HEADS, :]

    for c in range(N_BRANCH * D_MODEL // 512):
        gates_ref[0, :, c * 512:(c + 1) * 512] = jax.nn.sigmoid(proj(C_GZ + c * 512, 512)).astype(BF16)


def _inproj(layer, x, g, w, tab, rot_e, rot_one, qkg, fb):
    B, S, D = x.shape
    tm = TM_IN
    tok = lambda width: pl.BlockSpec((1, tm, width), lambda b, i: (b, i, 0))
    out_shape = (
        [jax.ShapeDtypeStruct((B, S, 512), BF16)] * 7
        + [jax.ShapeDtypeStruct((B, S, LANES), F32),
           jax.ShapeDtypeStruct((B, FOX_HEADS, S), F32),
           jax.ShapeDtypeStruct((B, S, N_BRANCH * D), BF16)]
    )
    out_specs = (
        [tok(512)] * 7
        + [tok(LANES),
           pl.BlockSpec((1, FOX_HEADS, tm), lambda b, i: (b, 0, i)),
           tok(N_BRANCH * D)]
    )
    return pl.pallas_call(
        _inproj_kernel,
        grid=(B, S // tm),
        in_specs=[tok(D), _layer_spec(layer, (1, D)), _layer_spec(layer, (D, IN_PACKED)),
                  tok(2 * (ROT_DIM // 2)), _const_spec((2 * (ROT_DIM // 2), 3 * LANES)), _const_spec((1, LANES)),
                  _layer_spec(layer, (4, LANES)), _layer_spec(layer, (1, LANES))],
        out_specs=out_specs,
        out_shape=out_shape,
        scratch_shapes=[pltpu.VMEM((8, LANES), F32)],
        compiler_params=pltpu.CompilerParams(
            dimension_semantics=("parallel", "arbitrary"), vmem_limit_bytes=VMEM_LIMIT),
        name="inproj",
    )(x, g, w, tab, rot_e, rot_one, qkg, fb)


def _nt_dot(a, b):
    return lax.dot_general(a, b, (((1,), (1,)), ((), ())), preferred_element_type=F32)


def _score_rows(score_chunk, n_chunks, s_scr):
    tq = TQ
    mt = jnp.full((tq, LANES), NEG, F32)
    for c in range(n_chunks):
        s = score_chunk(c)
        s_scr[:, c * tq:(c + 1) * tq] = s
        for w in range(tq // LANES):
            mt = jnp.maximum(mt, s[:, w * LANES:(w + 1) * LANES])
    return jnp.max(mt, axis=1, keepdims=True)


def _exp_rows(n_chunks, m, s_scr, p_scr):
    tq = TQ
    for c in range(n_chunks):
        p_scr[:, c * tq:(c + 1) * tq] = jnp.exp2(s_scr[:, c * tq:(c + 1) * tq] - m).astype(BF16)


def _diff_kernel(lam_init, q_ref, k_ref, v_ref, lam_ref, g_ref, o_ref, s_scr, p_scr, v2_scr):
    S = q_ref.shape[1]
    tq = TQ
    lo = lax.broadcasted_iota(jnp.int32, (tq, LANES), 1) < HEAD_DIM
    causal = (lax.broadcasted_iota(jnp.int32, (tq, tq), 1) <= lax.broadcasted_iota(jnp.int32, (tq, tq), 0))
    lp = lam_ref[...]
    lam = (jnp.exp(jnp.sum(lp[0:1, :] * lp[1:2, :], axis=1, keepdims=True))
           - jnp.exp(jnp.sum(lp[2:3, :] * lp[3:4, :], axis=1, keepdims=True)) + lam_init)
    v2_scr[:, 0:LANES] = v_ref[0]
    v2_scr[:, LANES:2 * LANES] = jnp.ones((S, LANES), BF16)

    def scores(qi, comp):
        q = q_ref[0, qi * tq:(qi + 1) * tq, :]
        zero = jnp.zeros_like(q)
        qm = jnp.where(lo, q, zero) if comp == 0 else jnp.where(lo, zero, q)

        def score_chunk(c):
            s = _nt_dot(qm, k_ref[0, c * tq:(c + 1) * tq, :])
            return jnp.where(causal, s, NEG) if c == qi else s

        return _score_rows(score_chunk, qi + 1, s_scr.at[2 * (qi % 2) + comp])

    n_q = S // tq
    row_max = {(0, comp): scores(0, comp) for comp in range(2)}
    for qi in range(n_q):
        kend = (qi + 1) * tq
        parts = []
        for comp in range(2):
            _exp_rows(qi + 1, row_max[(qi, comp)], s_scr.at[2 * (qi % 2) + comp], p_scr.at[comp])
            if qi + 1 < n_q:
                row_max[(qi + 1, comp)] = scores(qi + 1, comp)
            pv = jnp.dot(p_scr[comp, :, 0:kend], v2_scr[0:kend, :], preferred_element_type=F32)
            parts.append(pv[:, 0:LANES] / pv[:, LANES:2 * LANES])
        o = parts[0] - lam * parts[1]
        o = _rms(o, g_ref[...]) * (1.0 - lam_init)
        o_ref[0, qi * tq:kend, :] = o.astype(BF16)


def _diff_attention(layer, dq, dk, dv, lam_p, subln_g, lam_init):
    B, S, _ = dq.shape
    blk = pl.BlockSpec((1, S, LANES), lambda b, h: (b, 0, h))
    return pl.pallas_call(
        functools.partial(_diff_kernel, lam_init),
        grid=(B, 4),
        in_specs=[blk, blk, blk, _layer_spec(layer, (4, HEAD_DIM)), _layer_spec(layer, (1, LANES))],
        out_specs=blk,
        out_shape=jax.ShapeDtypeStruct((B, S, 512), BF16),
        scratch_shapes=[pltpu.VMEM((4, TQ, S), F32), pltpu.VMEM((2, TQ, S), BF16),
                        pltpu.VMEM((S, 2 * LANES), BF16)],
        compiler_params=pltpu.CompilerParams(
            dimension_semantics=("parallel", "parallel"), vmem_limit_bytes=VMEM_LIMIT),
        name="diff_attn",
    )(dq, dk, dv, lam_p, subln_g)


def _fox_kernel(q_ref, k_ref, v_ref, f_ref, ft_ref, o_ref, s_scr, p_scr, v2_scr):
    S = q_ref.shape[1]
    tq = TQ
    j = pl.program_id(1)
    lane = lax.broadcasted_iota(jnp.int32, (tq, LANES), 1)
    lo = lane < HEAD_DIM
    causal = (lax.broadcasted_iota(jnp.int32, (tq, tq), 1) <= lax.broadcasted_iota(jnp.int32, (tq, tq), 0))
    v = v_ref[0]
    lo_s = lax.broadcasted_iota(jnp.int32, (S, LANES), 1) < HEAD_DIM
    one = jnp.ones_like(v)
    v2_scr[0] = jnp.where(lo_s, v, one)
    v2_scr[1] = jnp.where(lo_s, one, v)

    def scores(qi, hh):
        q = q_ref[0, qi * tq:(qi + 1) * tq, :]
        zero = jnp.zeros_like(q)
        qm = jnp.where(lo, q, zero) if hh == 0 else jnp.where(lo, zero, q)
        f_tile = f_ref[0, qi * tq:(qi + 1) * tq, :] * LOG2E
        f_t = jnp.sum(jnp.where(lane == 2 * j + hh, f_tile, 0.0), axis=1, keepdims=True)

        def score_chunk(c):
            f_s = ft_ref[0, pl.ds(2 * j + hh, 1), c * tq:(c + 1) * tq] * LOG2E
            s = _nt_dot(qm, k_ref[0, c * tq:(c + 1) * tq, :]) + (f_t - f_s)
            return jnp.where(causal, s, NEG) if c == qi else s

        return _score_rows(score_chunk, qi + 1, s_scr.at[2 * (qi % 2) + hh])

    n_q = S // tq
    row_max = {(0, hh): scores(0, hh) for hh in range(2)}
    for qi in range(n_q):
        kend = (qi + 1) * tq
        pvs = []
        for hh in range(2):
            _exp_rows(qi + 1, row_max[(qi, hh)], s_scr.at[2 * (qi % 2) + hh], p_scr.at[hh])
            if qi + 1 < n_q:
                row_max[(qi + 1, hh)] = scores(qi + 1, hh)
            pvs.append(jnp.dot(p_scr[hh, :, 0:kend], v2_scr[hh, 0:kend, :], preferred_element_type=F32))
        o = jnp.where(lo, pvs[0] / pltpu.roll(pvs[0], HEAD_DIM, 1), pvs[1] / pltpu.roll(pvs[1], HEAD_DIM, 1))
        o_ref[0, qi * tq:kend, :] = o.astype(BF16)


def _fox_attention(fq, fk, fv, f, ft):
    B, S, _ = fq.shape
    blk = pl.BlockSpec((1, S, LANES), lambda b, j: (b, 0, j))
    return pl.pallas_call(
        _fox_kernel,
        grid=(B, 4),
        in_specs=[blk, blk, blk,
                  pl.BlockSpec((1, S, LANES), lambda b, j: (b, 0, 0)),
                  pl.BlockSpec((1, FOX_HEADS, S), lambda b, j: (b, 0, 0))],
        out_specs=blk,
        out_shape=jax.ShapeDtypeStruct((B, S, 512), BF16),
        scratch_shapes=[pltpu.VMEM((4, TQ, S), F32), pltpu.VMEM((2, TQ, S), BF16),
                        pltpu.VMEM((2, S, LANES), BF16)],
        compiler_params=pltpu.CompilerParams(
            dimension_semantics=("parallel", "parallel"), vmem_limit_bytes=VMEM_LIMIT),
        name="fox_attn",
    )(fq, fk, fv, f, ft)


def _merge_kernel(x_ref, u_ref, uh_ref, yd_ref, yf_ref, gates_ref, pw_ref, ps_ref, wb_ref, wo_ref,
                  ng_ref, wrh_ref, wrl_ref, br_ref,
                  x1_ref, hf_ref, info_ref, count_ref):
    i = pl.program_id(1)
    tm = x_ref.shape[1]

    halo = uh_ref[0].astype(F32) * jnp.where(i > 0, 1.0, 0.0)
    ext = jnp.concatenate([halo, u_ref[0].astype(F32)], axis=0)
    t = i * tm + lax.broadcasted_iota(jnp.int32, (tm, 1), 0)
    ys = []
    for g, w in enumerate(POOL_WINDOWS):
        e = ext[:, g * LANES:(g + 1) * LANES]
        win = e
        step = 1
        while step < w:
            win = win + pltpu.roll(win, step, 0)
            step *= 2
        cnt = jnp.minimum(t + 1, w).astype(F32)
        d = win[POOL_HALO:, :] / cnt - e[POOL_HALO:, :]
        ys.append(jnp.dot(d.astype(BF16), pw_ref[g], preferred_element_type=F32))
    y_pool = jnp.concatenate(ys, axis=1) * ps_ref[...]

    branches = (y_pool.astype(BF16), yd_ref[0], yf_ref[0])
    merged = jnp.zeros((tm, D_MODEL), F32)
    for n in range(N_BRANCH):
        br = jnp.dot(branches[n], wb_ref[n], preferred_element_type=F32)
        merged = merged + gates_ref[0, :, n * D_MODEL:(n + 1) * D_MODEL].astype(F32) * br
    x1 = x_ref[0] + jnp.dot(merged.astype(BF16), wo_ref[...], preferred_element_type=F32)
    x1_ref[0] = x1

    hf = _rms(x1, ng_ref[...])
    hf_ref[0] = hf
    lg = _router_logits(hf, wrh_ref, wrl_ref, br_ref)
    lane = lax.broadcasted_iota(jnp.int32, (tm, LANES), 1)
    lane_f = lane.astype(F32)
    gmask = lane < N_GROUPS
    gl = jnp.where(gmask, lg, NEG)
    gmax = jnp.max(gl, axis=1, keepdims=True)
    gi = jnp.min(jnp.where(gmask & (gl == gmax), lane_f, float(LANES)), axis=1, keepdims=True)

    @pl.when((pl.program_id(0) == 0) & (i == 0))
    def _():
        count_ref[...] = jnp.zeros_like(count_ref)

    onehot = jnp.where(lane_f == gi, 1.0, 0.0)
    row = lax.broadcasted_iota(jnp.int32, (tm, tm), 0)
    col = lax.broadcasted_iota(jnp.int32, (tm, tm), 1)
    tri = jnp.where(row >= col, 1.0, 0.0).astype(BF16)
    incl = jnp.dot(tri, onehot.astype(BF16), preferred_element_type=F32)
    before = count_ref[0:1, :]
    rank = jnp.sum(onehot * (incl - onehot + before), axis=1, keepdims=True)
    count_ref[...] = jnp.broadcast_to(before + incl[tm - 1:tm, :], count_ref.shape)
    info_ref[0] = jnp.where(lane == 0, gi, jnp.where(lane == 1, rank, 0.0)).T[0:8, :]


def _merge(layer, x, u, yd, yf, gates, pw, ps, wb, wo, ng, wrh, wrl, brt):
    B, S, D = x.shape
    tm = TM_MERGE
    tok = lambda width: pl.BlockSpec((1, tm, width), lambda b, i: (b, i, 0))
    halo_blocks = tm // POOL_HALO
    halo = pl.BlockSpec((1, POOL_HALO, 512), lambda b, i: (b, jnp.maximum(i * halo_blocks - 1, 0), 0))
    return pl.pallas_call(
        _merge_kernel,
        grid=(B, S // tm),
        in_specs=[tok(D), tok(512), halo, tok(512), tok(512), tok(N_BRANCH * D),
                  _layer_spec(layer, (4, LANES, LANES)), _layer_spec(layer, (1, 512)),
                  _layer_spec(layer, (N_BRANCH, BRANCH_WIDTH, D)), _layer_spec(layer, (D, D)),
                  _layer_spec(layer, (1, D)), _layer_spec(layer, (D, LANES)), _layer_spec(layer, (D, LANES)),
                  _layer_spec(layer, (1, LANES))],
        out_specs=[tok(D), tok(D), pl.BlockSpec((1, 8, tm), lambda b, i: (b, 0, i)),
                   pl.BlockSpec((8, LANES), lambda b, i: (0, 0))],
        out_shape=[jax.ShapeDtypeStruct((B, S, D), F32),
                   jax.ShapeDtypeStruct((B, S, D), F32),
                   jax.ShapeDtypeStruct((B, 8, S), F32),
                   jax.ShapeDtypeStruct((8, LANES), F32)],
        compiler_params=pltpu.CompilerParams(
            dimension_semantics=("arbitrary", "arbitrary"), vmem_limit_bytes=VMEM_LIMIT),
        name="merge",
    )(x, u, u, yd, yf, gates, pw, ps, wb, wo, ng, wrh, wrl, brt)


def _moe_kernel(gid_ref, nused_ref, src_ref, src_next_ref, dst_prev_ref,
                hf_hbm, wg_ref, wu_ref, wd_ref, wrh_ref, wrl_ref, br_ref,
                y_hbm,
                xbuf, ybuf, wgu_bf, wd_bf, sem_g, sem_s):
    i = pl.program_id(0)
    n_used = nused_ref[0]
    tm = TM_MOE
    n_tiles = pl.num_programs(0) - 1

    def gather_row(idx_ref, r, to_slot):
        return pltpu.make_async_copy(hf_hbm.at[pl.ds(idx_ref[0, 0, r], 1)], xbuf.at[to_slot, pl.ds(r, 1)],
                                     sem_g.at[to_slot])

    def scatter_row(r, from_slot):
        return pltpu.make_async_copy(ybuf.at[from_slot, pl.ds(r, 1)], y_hbm.at[pl.ds(dst_prev_ref[0, 0, r], 1)],
                                     sem_s.at[from_slot])

    def start_rows(make_copy, unrolled):
        if unrolled:
            for r in range(tm):
                make_copy(r).start()
        else:
            def body(r, c):
                make_copy(r).start()
                return c
            lax.fori_loop(0, tm, body, 0, unroll=8)

    def wait_gather(of_slot):
        pltpu.make_async_copy(hf_hbm.at[pl.ds(0, tm)], xbuf.at[of_slot], sem_g.at[of_slot]).wait()

    def wait_scatter(of_slot):
        pltpu.make_async_copy(ybuf.at[of_slot], y_hbm.at[pl.ds(0, tm)], sem_s.at[of_slot]).wait()

    def cast_weights():
        for e in range(EXPERTS_PER_GROUP):
            wgu_bf[:, e * D_EXPERT:(e + 1) * D_EXPERT] = wg_ref[e].astype(BF16)
            wgu_bf[:, GROUP_HIDDEN + e * D_EXPERT:GROUP_HIDDEN + (e + 1) * D_EXPERT] = wu_ref[e].astype(BF16)
            wd_bf[e * D_EXPERT:(e + 1) * D_EXPERT, :] = wd_ref[e].astype(BF16)

    def run_experts(slot, g):
        hf = xbuf[slot]

        lg = _router_logits(hf, wrh_ref, wrl_ref, br_ref)
        lane = lax.broadcasted_iota(jnp.int32, (tm, LANES), 1)
        lane_f = lane.astype(F32)
        far = float(LANES)
        rmax = lambda a: jnp.max(a, axis=1, keepdims=True)
        rmin = lambda a: jnp.min(a, axis=1, keepdims=True)
        rsum = lambda a: jnp.sum(a, axis=1, keepdims=True)
        lg_g = rsum(jnp.where(lane == g, lg, 0.0))
        gp = 1.0 / rsum(jnp.where(lane < N_GROUPS, jnp.exp(lg - lg_g), 0.0))
        e_lo = N_GROUPS + EXPERTS_PER_GROUP * g
        emask = (lane >= e_lo) & (lane < e_lo + EXPERTS_PER_GROUP)
        el = jnp.where(emask, lg, NEG)
        m1 = rmax(el)
        i1 = rmin(jnp.where(emask & (el == m1), lane_f, far))
        rest = emask & (lane_f != i1)
        el2 = jnp.where(rest, lg, NEG)
        m2 = rmax(el2)
        i2 = rmin(jnp.where(rest & (el2 == m2), lane_f, far))
        r = jnp.exp(m2 - m1)
        comb = jnp.where(lane_f == i1, gp / (1.0 + r), jnp.where(lane_f == i2, gp * r / (1.0 + r), 0.0))
        cw = jnp.concatenate(
            [jnp.broadcast_to(rsum(jnp.where(lane == e_lo + e, comb, 0.0)), (tm, D_EXPERT))
             for e in range(EXPERTS_PER_GROUP)], axis=1)

        gu = jnp.dot(hf.astype(BF16), wgu_bf[...], preferred_element_type=F32)
        gate, up = gu[:, 0:GROUP_HIDDEN], gu[:, GROUP_HIDDEN:2 * GROUP_HIDDEN]
        act = gate * jax.nn.sigmoid(gate) * up * cw
        ybuf[slot] = jnp.dot(act.astype(BF16), wd_bf[...], preferred_element_type=F32)

    tile = jnp.minimum(i, n_tiles - 1)
    g = gid_ref[tile]
    steady = (i >= 2) & (i + 1 < n_used)

    def steady_step(slot):
        wait_gather(slot)
        wait_scatter(slot)

        @pl.when(g != gid_ref[tile - 1])
        def _():
            cast_weights()

        start_rows(lambda r: gather_row(src_next_ref, r, 1 - slot), True)
        start_rows(lambda r: scatter_row(r, 1 - slot), True)
        run_experts(slot, g)

    for parity in range(2):
        @pl.when(steady & (lax.rem(i, 2) == parity))
        def _(parity=parity):
            steady_step(parity)

    @pl.when(jnp.logical_not(steady))
    def _():
        slot = lax.rem(i, 2)

        @pl.when(i == 0)
        def _():
            start_rows(lambda r: gather_row(src_ref, r, slot), False)
            ybuf[1] = jnp.zeros((tm, D_MODEL), F32)
            n_real = hf_hbm.shape[0]
            spare = [pltpu.make_async_copy(ybuf.at[1], y_hbm.at[pl.ds(n_real + k * tm, tm)], sem_s.at[1])
                     for k in range(N_GROUPS)]
            for cp in spare:
                cp.start()
            for cp in spare:
                cp.wait()

        @pl.when(i < n_used)
        def _():
            wait_gather(slot)

        @pl.when(i + 1 < n_used)
        def _():
            start_rows(lambda r: gather_row(src_next_ref, r, 1 - slot), False)

        @pl.when((i >= 1) & (i <= n_used))
        def _():
            start_rows(lambda r: scatter_row(r, 1 - slot), False)

        @pl.when(i < n_used)
        def _():
            @pl.when(i >= 2)
            def _():
                wait_scatter(slot)

            @pl.when((i == 0) | (g != gid_ref[jnp.maximum(tile - 1, 0)]))
            def _():
                cast_weights()

            run_experts(slot, g)

        @pl.when(i == n_used)
        def _():
            wait_scatter(1 - slot)

            @pl.when(n_used >= 2)
            def _():
                wait_scatter(slot)


def _moe(layer, gid, n_used, src, dst, hf, wg, wu, wd, wrh, wrl, brt):
    T, D = hf.shape
    tm = TM_MOE
    n_tiles = src.shape[0]
    clamp = lambda t: jnp.clip(t, 0, n_tiles - 1)
    idx_spec = lambda shift: pl.BlockSpec(
        (1, 1, tm), lambda i, gid, nu: (clamp(i + shift), 0, 0), memory_space=pltpu.SMEM)
    grp = lambda shape: pl.BlockSpec((None,) + shape, lambda i, gid, nu: (layer, gid[clamp(i)], 0, 0))
    return pl.pallas_call(
        _moe_kernel,
        grid_spec=pltpu.PrefetchScalarGridSpec(
            num_scalar_prefetch=2,
            grid=(n_tiles + 1,),
            in_specs=[idx_spec(0), idx_spec(1), idx_spec(-1),
                      pl.BlockSpec(memory_space=pl.ANY),
                      grp((EXPERTS_PER_GROUP, D, D_EXPERT)), grp((EXPERTS_PER_GROUP, D, D_EXPERT)),
                      grp((EXPERTS_PER_GROUP, D_EXPERT, D)),
                      _layer_spec(layer, (D, LANES)), _layer_spec(layer, (D, LANES)), _layer_spec(layer, (1, LANES))],
            out_specs=pl.BlockSpec(memory_space=pl.ANY),
            scratch_shapes=[pltpu.VMEM((2, tm, D), F32), pltpu.VMEM((2, tm, D), F32),
                            pltpu.VMEM((D, 2 * GROUP_HIDDEN), BF16), pltpu.VMEM((GROUP_HIDDEN, D), BF16),
                            pltpu.SemaphoreType.DMA((2,)), pltpu.SemaphoreType.DMA((2,))]),
        out_shape=jax.ShapeDtypeStruct((n_tiles * tm, D), F32),
        compiler_params=pltpu.CompilerParams(
            dimension_semantics=("arbitrary",), vmem_limit_bytes=VMEM_LIMIT),
        name="moe",
    )(gid, n_used, src, src, dst, hf, wg, wu, wd, wrh, wrl, brt)


def _ple_kernel(x1_ref, y_ref, p_ref, pg_ref, wpg_ref, wp_ref, o_ref):
    x2 = x1_ref[...] + y_ref[...]
    hp = _rms(x2, pg_ref[...]).astype(BF16)
    gate = jax.nn.sigmoid(jnp.dot(hp, wpg_ref[...], preferred_element_type=F32))
    emb = jnp.dot(p_ref[...].astype(BF16), wp_ref[...], preferred_element_type=F32)
    o_ref[...] = x2 + gate * emb


def _ple(layer, x1, y, p, pg, wpg, wp):
    T, D = x1.shape
    tm = TM_PLE
    tok = lambda width: pl.BlockSpec((tm, width), lambda i: (i, 0))
    return pl.pallas_call(
        _ple_kernel,
        grid=(T // tm,),
        in_specs=[tok(D), tok(D), pl.BlockSpec((None, tm, PLE_DIM), lambda i: (layer, i, 0)),
                  _layer_spec(layer, (1, D)), _layer_spec(layer, (D, D)), _layer_spec(layer, (PLE_DIM, D))],
        out_specs=tok(D),
        out_shape=jax.ShapeDtypeStruct((T, D), F32),
        compiler_params=pltpu.CompilerParams(
            dimension_semantics=("parallel",), vmem_limit_bytes=VMEM_LIMIT),
        name="ple",
    )(x1, y, p, pg, wpg, wp)


def _routing_tables(info, counts):
    tm = TM_MOE
    T = info.shape[0] * info.shape[2]
    n_rows = T + N_GROUPS * tm
    n_tiles = n_rows // tm
    group = info[:, 0, :].reshape(T).astype(jnp.int32)
    rank = info[:, 1, :].reshape(T).astype(jnp.int32)
    cnt = counts[0, :N_GROUPS].astype(jnp.int32)
    padded = (cnt + tm - 1) // tm * tm
    ends = jnp.cumsum(padded)
    base = ends - padded
    pos = base[group] + rank
    tok_of_row = jnp.full((n_rows,), -1, jnp.int32).at[pos].set(jnp.arange(T, dtype=jnp.int32))
    real = tok_of_row >= 0
    src = jnp.where(real, tok_of_row, 0)
    pad_index = jnp.cumsum(jnp.where(real, 0, 1)) - 1
    dst = jnp.where(real, tok_of_row, T + pad_index)
    starts = jnp.arange(n_tiles, dtype=jnp.int32) * tm
    gid = jnp.minimum(jnp.sum(starts[:, None] >= ends[None, :], axis=1), N_GROUPS - 1).astype(jnp.int32)
    n_used = (ends[N_GROUPS - 1] // tm).astype(jnp.int32)[None]
    gid = jnp.where(starts < ends[N_GROUPS - 1], gid, gid[jnp.maximum(n_used[0] - 1, 0)])
    shape3 = (n_tiles, 1, tm)
    return gid, n_used, src.reshape(shape3), dst.reshape(shape3)


def _rotary_table(positions):
    half = ROT_DIM // 2
    per_row = LANES // half
    inv_freq = ROPE_THETA ** (-jnp.arange(0, ROT_DIM, 2, dtype=F32) / ROT_DIM)
    B, S = positions.shape
    pos = jnp.repeat(positions.astype(F32).reshape(B, S // per_row, per_row), half, axis=-1)
    ang = pos * jnp.tile(inv_freq, per_row)
    c = jnp.cos(ang).reshape(B, S, half)
    s = jnp.sin(ang).reshape(B, S, half)
    return jnp.concatenate([c, s], axis=-1)


def _rotary_placement():
    half = ROT_DIM // 2
    e = np.zeros((2 * half, 3 * LANES), np.float32)
    one = np.zeros((1, LANES), np.float32)
    for lane in range(LANES):
        d = lane % HEAD_DIM
        if d < half:
            e[d, lane] = 1.0
            e[half + d, LANES + lane] = -1.0
        elif d < ROT_DIM:
            e[d - half, lane] = 1.0
            e[d, 2 * LANES + lane] = 1.0
        else:
            one[0, lane] = 1.0
    return jnp.asarray(e, BF16), jnp.asarray(one, F32)


def _pack_w_in(w):
    n_main = C_FZ
    fz = jnp.pad(w[:, :, n_main:n_main + FOX_HEADS], ((0, 0), (0, 0), (0, LANES - FOX_HEADS)))
    return jnp.concatenate([w[:, :, :n_main], fz, w[:, :, n_main + FOX_HEADS:]], axis=2).astype(BF16)


def kernel(x, p, positions, attn_norm_g, w_in, pool_w, pool_scale, diff_qn_g, diff_kn_g, diff_lambda, diff_subln_g, fox_qn_g, fox_kn_g, fox_forget_b, w_branch, w_out, ffn_norm_g, w_route_group, b_route_group, w_route_expert, b_route_expert, moe_w_gate, moe_w_up, moe_w_down, ple_norm_g, w_ple_gate, w_ple):
    B, S, D = x.shape
    T = B * S
    depth = w_in.shape[0]
    two = lambda a: jnp.concatenate([a, a], axis=-1)
    pad_lanes = lambda a: jnp.pad(a, [(0, 0)] * (a.ndim - 1) + [(0, LANES - a.shape[-1])])
    row = lambda a: a[:, None, :]

    tab = _rotary_table(positions)
    rot_e, rot_one = _rotary_placement()
    w_in_p = _pack_w_in(w_in)
    qkg = jnp.stack([two(diff_qn_g), two(diff_kn_g), two(fox_qn_g), two(fox_kn_g)], axis=1)
    fb = row(pad_lanes(fox_forget_b))
    w_r = jnp.concatenate([w_route_group, w_route_expert], axis=2)
    w_r_hi = w_r.astype(BF16)
    wrh = pad_lanes(w_r_hi)
    wrl = pad_lanes((w_r - w_r_hi.astype(F32)).astype(BF16))
    brt = row(pad_lanes(jnp.concatenate([b_route_group, b_route_expert], axis=1)))
    pool_w_b, w_branch_b, w_out_b = pool_w.astype(BF16), w_branch.astype(BF16), w_out.astype(BF16)
    w_ple_gate_b, w_ple_b = w_ple_gate.astype(BF16), w_ple.astype(BF16)
    p_flat = p.reshape(depth, T, PLE_DIM)

    for l in range(depth):
        lam_init = 0.8 - 0.6 * math.exp(-0.3 * l)
        u, dq, dk, dv, fq, fk, fv, f, ft, gates = _inproj(
            l, x, row(attn_norm_g), w_in_p, tab, rot_e, rot_one, qkg, fb)
        y_diff = _diff_attention(l, dq, dk, dv, diff_lambda, row(diff_subln_g), lam_init)
        y_fox = _fox_attention(fq, fk, fv, f, ft)
        x1, hf, info, counts = _merge(
            l, x, u, y_diff, y_fox, gates, pool_w_b, row(pool_scale), w_branch_b, w_out_b,
            row(ffn_norm_g), wrh, wrl, brt)
        gid, n_used, src, dst = _routing_tables(info, counts)
        y = _moe(l, gid, n_used, src, dst, hf.reshape(T, D), moe_w_gate, moe_w_up, moe_w_down, wrh, wrl, brt)
        x = _ple(l, x1.reshape(T, D), y, p_flat, row(ple_norm_g), w_ple_gate_b, w_ple_b).reshape(B, S, D)
    return x
```

```python
import functools
import math

import jax
import jax.numpy as jnp
import numpy as np
from jax import lax
from jax.experimental import pallas as pl
from jax.experimental.pallas import tpu as pltpu

F32 = jnp.float32
BF16 = jnp.bfloat16

D_MODEL = 1024
HEAD_DIM = 64
LANES = 128
POOL_WINDOWS = (2, 4, 8, 16)
POOL_HALO = 16
BRANCH_WIDTH = 512
N_BRANCH = 3
FOX_HEADS = 8
ROT_DIM = HEAD_DIM // 4
ROPE_THETA = 500000.0
PLE_DIM = 256
N_GROUPS = 4
EXPERTS_PER_GROUP = 4
N_EXPERTS = 16
D_EXPERT = 256
RMS_EPS = 1e-6
NEG = -1e30
LOG2E = math.log2(math.e)

C_POOL, C_DQ, C_DK, C_DV, C_FQ, C_FK, C_FV, C_FZ, C_GZ = (0, 512, 1024, 1536, 2048, 2560, 3072, 3584, 3712)
IN_PACKED = C_GZ + N_BRANCH * D_MODEL

TM_IN = 512
TM_MERGE = 512
TM_MOE = 512
TM_PLE = 512
GROUP_HIDDEN = EXPERTS_PER_GROUP * D_EXPERT
TQ = 256
VMEM_LIMIT = 56 * 1024 * 1024


def _const_spec(shape):
    zeros = (0,) * len(shape)
    return pl.BlockSpec(shape, lambda *_: zeros, pipeline_mode=pl.Buffered(1))


def _layer_spec(layer, shape):
    zeros = (0,) * len(shape)
    return pl.BlockSpec((None,) + tuple(shape), lambda *_: (layer,) + zeros, pipeline_mode=pl.Buffered(1))


def _rms(x, g):
    return x * lax.rsqrt(jnp.mean(x * x, axis=-1, keepdims=True) + RMS_EPS) * g


SLAB = D_MODEL // LANES


def _rows_to_slabs(dst, lead, x):
    n = x.shape[0]
    for c in range(SLAB):
        dst[lead + (pl.ds(c, n, stride=SLAB), slice(None))] = x[:, c * LANES:(c + 1) * LANES]


def _slabs_to_rows(src, lead, n):
    return jnp.concatenate([src[lead + (pl.ds(c, n, stride=SLAB), slice(None))] for c in range(SLAB)], axis=1)


def _router_logits(hf, wrh_ref, wrl_ref, br_ref):
    hi = hf.astype(BF16)
    lo = (hf - hi.astype(F32)).astype(BF16)
    return (jnp.dot(hi, wrh_ref[...], preferred_element_type=F32)
            + jnp.dot(lo, wrh_ref[...], preferred_element_type=F32)
            + jnp.dot(hi, wrl_ref[...], preferred_element_type=F32)
            + br_ref[...])


def _inproj_kernel(x_ref, g_ref, w_ref, tab_ref, rot_e_ref, rot_one_ref, qkg_ref, fb_ref,
                   u_ref, dq_ref, dk_ref, dv_ref, fq_ref, fk_ref, fv_ref, f_ref, ft_ref, gates_ref,
                   carry_ref):
    i = pl.program_id(1)
    tm = x_ref.shape[1]

    @pl.when(i == 0)
    def _():
        carry_ref[...] = jnp.zeros_like(carry_ref)

    h = _rms(x_ref[0], g_ref[...]).astype(BF16)

    def proj(c0, width):
        return jnp.dot(h, w_ref[:, c0:c0 + width], preferred_element_type=F32)

    lo = lax.broadcasted_iota(jnp.int32, (tm, LANES), 1) < HEAD_DIM
    tab = tab_ref[0]
    tab_hi = tab.astype(BF16)
    tab_lo = (tab - tab_hi.astype(F32)).astype(BF16)
    placed = (jnp.dot(tab_hi, rot_e_ref[...], preferred_element_type=F32)
              + jnp.dot(tab_lo, rot_e_ref[...], preferred_element_type=F32))
    cos = placed[:, 0:LANES] + rot_one_ref[...]
    sa = placed[:, LANES:2 * LANES]
    sb = placed[:, 2 * LANES:3 * LANES]
    scale = HEAD_DIM ** -0.5 * LOG2E

    def head_norm(a, g):
        sq = a * a
        s_lo = jnp.sum(jnp.where(lo, sq, 0.0), axis=-1, keepdims=True)
        s_hi = jnp.sum(jnp.where(lo, 0.0, sq), axis=-1, keepdims=True)
        ss = jnp.where(lo, s_lo, s_hi)
        return a * lax.rsqrt(ss * (1.0 / HEAD_DIM) + RMS_EPS) * g

    def rotary(a):
        return a * cos + pltpu.roll(a, LANES - ROT_DIM // 2, 1) * sa + pltpu.roll(a, ROT_DIM // 2, 1) * sb

    def plain(out_ref):
        def epilogue(acc):
            out_ref[0] = acc.astype(BF16)
        return epilogue

    def qk(out_ref, g, rot, mult):
        def epilogue(acc):
            for j in range(4):
                a = head_norm(acc[:, j * LANES:(j + 1) * LANES], g)
                if rot:
                    a = rotary(a)
                if mult != 1.0:
                    a = a * mult
                out_ref[0, :, j * LANES:(j + 1) * LANES] = a.astype(BF16)
        return epilogue

    def forget(acc):
        z = acc + fb_ref[...]
        logf = jnp.minimum(z, 0.0) - jnp.log1p(jnp.exp(-jnp.abs(z)))
        row = lax.broadcasted_iota(jnp.int32, (tm, tm), 0)
        col = lax.broadcasted_iota(jnp.int32, (tm, tm), 1)
        tri = jnp.where(row >= col, 1.0, 0.0).astype(BF16)
        hi = logf.astype(BF16)
        lo_part = (logf - hi.astype(F32)).astype(BF16)
        fcum = (jnp.dot(tri, hi, preferred_element_type=F32)
                + jnp.dot(tri, lo_part, preferred_element_type=F32)
                + carry_ref[0:1, :])
        carry_ref[...] = jnp.broadcast_to(fcum[tm - 1:tm, :], carry_ref.shape)
        f_ref[0] = fcum
        ft_ref[0] = fcum.T[0:FOX_HEADS, :]

    def gate(c):
        def epilogue(acc):
            gates_ref[0, :, c * 512:(c + 1) * 512] = jax.nn.sigmoid(acc).astype(BF16)
        return epilogue

    chunks = [(C_POOL, 512, plain(u_ref)),
              (C_DQ, 512, qk(dq_ref, qkg_ref[0:1, :], True, scale)),
              (C_DK, 512, qk(dk_ref, qkg_ref[1:2, :], True, 1.0)),
              (C_DV, 512, plain(dv_ref)),
              (C_FQ, 512, qk(fq_ref, qkg_ref[2:3, :], False, scale)),
              (C_FK, 512, qk(fk_ref, qkg_ref[3:4, :], False, 1.0)),
              (C_FV, 512, plain(fv_ref)),
              (C_FZ, LANES, forget)]
    chunks += [(C_GZ + c * 512, 512, gate(c)) for c in range(N_BRANCH * D_MODEL // 512)]
    acc = proj(chunks[0][0], chunks[0][1])
    for k, (_, _, epilogue) in enumerate(chunks):
        nxt = proj(chunks[k + 1][0], chunks[k + 1][1]) if k + 1 < len(chunks) else None
        epilogue(acc)
        acc = nxt


def _inproj(layer, x, g, w, tab, rot_e, rot_one, qkg, fb):
    B, S, D = x.shape
    tm = TM_IN
    tok = lambda width: pl.BlockSpec((1, tm, width), lambda b, i: (b, i, 0))
    out_shape = (
        [jax.ShapeDtypeStruct((B, S, 512), BF16)] * 7
        + [jax.ShapeDtypeStruct((B, S, LANES), F32),
           jax.ShapeDtypeStruct((B, FOX_HEADS, S), F32),
           jax.ShapeDtypeStruct((B, S, N_BRANCH * D), BF16)]
    )
    out_specs = (
        [tok(512)] * 7
        + [tok(LANES),
           pl.BlockSpec((1, FOX_HEADS, tm), lambda b, i: (b, 0, i)),
           tok(N_BRANCH * D)]
    )
    return pl.pallas_call(
        _inproj_kernel,
        grid=(B, S // tm),
        in_specs=[tok(D), _layer_spec(layer, (1, D)), _layer_spec(layer, (D, IN_PACKED)),
                  tok(2 * (ROT_DIM // 2)), _const_spec((2 * (ROT_DIM // 2), 3 * LANES)), _const_spec((1, LANES)),
                  _layer_spec(layer, (4, LANES)), _layer_spec(layer, (1, LANES))],
        out_specs=out_specs,
        out_shape=out_shape,
        scratch_shapes=[pltpu.VMEM((8, LANES), F32)],
        compiler_params=pltpu.CompilerParams(
            dimension_semantics=("parallel", "arbitrary"), vmem_limit_bytes=VMEM_LIMIT),
        name="inproj",
    )(x, g, w, tab, rot_e, rot_one, qkg, fb)


def _nt_dot(a, b):
    return lax.dot_general(a, b, (((1,), (1,)), ((), ())), preferred_element_type=F32)


def _score_rows(score_chunk, n_chunks, s_scr):
    tq = TQ
    mt = jnp.full((tq, LANES), NEG, F32)
    for c in range(n_chunks):
        s = score_chunk(c)
        s_scr[:, c * tq:(c + 1) * tq] = s
        for w in range(tq // LANES):
            mt = jnp.maximum(mt, s[:, w * LANES:(w + 1) * LANES])
    return jnp.max(mt, axis=1, keepdims=True)


def _exp_rows(n_chunks, m, s_scr, p_scr):
    tq = TQ
    for c in range(n_chunks):
        p_scr[:, c * tq:(c + 1) * tq] = jnp.exp2(s_scr[:, c * tq:(c + 1) * tq] - m).astype(BF16)


def _diff_kernel(lam_init, q_ref, k_ref, v_ref, lam_ref, g_ref, o_ref, s_scr, p_scr, v2_scr):
    S = q_ref.shape[1]
    tq = TQ
    lo = lax.broadcasted_iota(jnp.int32, (tq, LANES), 1) < HEAD_DIM
    causal = (lax.broadcasted_iota(jnp.int32, (tq, tq), 1) <= lax.broadcasted_iota(jnp.int32, (tq, tq), 0))
    lp = lam_ref[...]
    lam = (jnp.exp(jnp.sum(lp[0:1, :] * lp[1:2, :], axis=1, keepdims=True))
           - jnp.exp(jnp.sum(lp[2:3, :] * lp[3:4, :], axis=1, keepdims=True)) + lam_init)
    v2_scr[:, 0:LANES] = v_ref[0]
    v2_scr[:, LANES:2 * LANES] = jnp.ones((S, LANES), BF16)

    def scores(qi, comp):
        q = q_ref[0, qi * tq:(qi + 1) * tq, :]
        zero = jnp.zeros_like(q)
        qm = jnp.where(lo, q, zero) if comp == 0 else jnp.where(lo, zero, q)

        def score_chunk(c):
            s = _nt_dot(qm, k_ref[0, c * tq:(c + 1) * tq, :])
            return jnp.where(causal, s, NEG) if c == qi else s

        return _score_rows(score_chunk, qi + 1, s_scr.at[2 * (qi % 2) + comp])

    n_q = S // tq
    row_max = {(0, comp): scores(0, comp) for comp in range(2)}
    for qi in range(n_q):
        kend = (qi + 1) * tq
        parts = []
        for comp in range(2):
            _exp_rows(qi + 1, row_max[(qi, comp)], s_scr.at[2 * (qi % 2) + comp], p_scr.at[comp])
            if qi + 1 < n_q:
                row_max[(qi + 1, comp)] = scores(qi + 1, comp)
            pv = jnp.dot(p_scr[comp, :, 0:kend], v2_scr[0:kend, :], preferred_element_type=F32)
            parts.append(pv[:, 0:LANES] / pv[:, LANES:2 * LANES])
        o = parts[0] - lam * parts[1]
        o = _rms(o, g_ref[...]) * (1.0 - lam_init)
        o_ref[0, qi * tq:kend, :] = o.astype(BF16)


def _diff_attention(layer, dq, dk, dv, lam_p, subln_g, lam_init):
    B, S, _ = dq.shape
    blk = pl.BlockSpec((1, S, LANES), lambda b, h: (b, 0, h))
    return pl.pallas_call(
        functools.partial(_diff_kernel, lam_init),
        grid=(B, 4),
        in_specs=[blk, blk, blk, _layer_spec(layer, (4, HEAD_DIM)), _layer_spec(layer, (1, LANES))],
        out_specs=blk,
        out_shape=jax.ShapeDtypeStruct((B, S, 512), BF16),
        scratch_shapes=[pltpu.VMEM((4, TQ, S), F32), pltpu.VMEM((2, TQ, S), BF16),
                        pltpu.VMEM((S, 2 * LANES), BF16)],
        compiler_params=pltpu.CompilerParams(
            dimension_semantics=("parallel", "parallel"), vmem_limit_bytes=VMEM_LIMIT),
        name="diff_attn",
    )(dq, dk, dv, lam_p, subln_g)


def _fox_kernel(q_ref, k_ref, v_ref, f_ref, ft_ref, o_ref, s_scr, p_scr, v2_scr):
    S = q_ref.shape[1]
    tq = TQ
    j = pl.program_id(1)
    lane = lax.broadcasted_iota(jnp.int32, (tq, LANES), 1)
    lo = lane < HEAD_DIM
    causal = (lax.broadcasted_iota(jnp.int32, (tq, tq), 1) <= lax.broadcasted_iota(jnp.int32, (tq, tq), 0))
    v = v_ref[0]
    lo_s = lax.broadcasted_iota(jnp.int32, (S, LANES), 1) < HEAD_DIM
    one = jnp.ones_like(v)
    v2_scr[0] = jnp.where(lo_s, v, one)
    v2_scr[1] = jnp.where(lo_s, one, v)

    def scores(qi, hh):
        q = q_ref[0, qi * tq:(qi + 1) * tq, :]
        zero = jnp.zeros_like(q)
        qm = jnp.where(lo, q, zero) if hh == 0 else jnp.where(lo, zero, q)
        f_tile = f_ref[0, qi * tq:(qi + 1) * tq, :] * LOG2E
        f_t = jnp.sum(jnp.where(lane == 2 * j + hh, f_tile, 0.0), axis=1, keepdims=True)

        def score_chunk(c):
            f_s = ft_ref[0, pl.ds(2 * j + hh, 1), c * tq:(c + 1) * tq] * LOG2E
            s = _nt_dot(qm, k_ref[0, c * tq:(c + 1) * tq, :]) + (f_t - f_s)
            return jnp.where(causal, s, NEG) if c == qi else s

        return _score_rows(score_chunk, qi + 1, s_scr.at[2 * (qi % 2) + hh])

    n_q = S // tq
    row_max = {(0, hh): scores(0, hh) for hh in range(2)}
    for qi in range(n_q):
        kend = (qi + 1) * tq
        pvs = []
        for hh in range(2):
            _exp_rows(qi + 1, row_max[(qi, hh)], s_scr.at[2 * (qi % 2) + hh], p_scr.at[hh])
            if qi + 1 < n_q:
                row_max[(qi + 1, hh)] = scores(qi + 1, hh)
            pvs.append(jnp.dot(p_scr[hh, :, 0:kend], v2_scr[hh, 0:kend, :], preferred_element_type=F32))
        o = jnp.where(lo, pvs[0] / pltpu.roll(pvs[0], HEAD_DIM, 1), pvs[1] / pltpu.roll(pvs[1], HEAD_DIM, 1))
        o_ref[0, qi * tq:kend, :] = o.astype(BF16)


def _fox_attention(fq, fk, fv, f, ft):
    B, S, _ = fq.shape
    blk = pl.BlockSpec((1, S, LANES), lambda b, j: (b, 0, j))
    return pl.pallas_call(
        _fox_kernel,
        grid=(B, 4),
        in_specs=[blk, blk, blk,
                  pl.BlockSpec((1, S, LANES), lambda b, j: (b, 0, 0)),
                  pl.BlockSpec((1, FOX_HEADS, S), lambda b, j: (b, 0, 0))],
        out_specs=blk,
        out_shape=jax.ShapeDtypeStruct((B, S, 512), BF16),
        scratch_shapes=[pltpu.VMEM((4, TQ, S), F32), pltpu.VMEM((2, TQ, S), BF16),
                        pltpu.VMEM((2, S, LANES), BF16)],
        compiler_params=pltpu.CompilerParams(
            dimension_semantics=("parallel", "parallel"), vmem_limit_bytes=VMEM_LIMIT),
        name="fox_attn",
    )(fq, fk, fv, f, ft)


def _merge_kernel(x_ref, u_ref, uh_ref, yd_ref, yf_ref, gates_ref, pw_ref, ps_ref, wb_ref, wo_ref,
                  ng_ref, wrh_ref, wrl_ref, br_ref,
                  x1_ref, hf_ref, info_ref, count_ref):
    i = pl.program_id(1)
    tm = x_ref.shape[1]

    @pl.when((pl.program_id(0) == 0) & (i == 0))
    def _():
        count_ref[...] = jnp.zeros_like(count_ref)

    halo = uh_ref[0].astype(F32) * jnp.where(i > 0, 1.0, 0.0)
    ext = jnp.concatenate([halo, u_ref[0].astype(F32)], axis=0)
    t = i * tm + lax.broadcasted_iota(jnp.int32, (tm, 1), 0)
    half = D_MODEL // 2
    ys, gated = [], {}
    for g, w in enumerate(POOL_WINDOWS):
        n, c0 = 1 + g // 2, (g % 2) * half
        y_att = yd_ref[0] if n == 1 else yf_ref[0]
        gated[(n, g % 2)] = (gates_ref[0, :, n * D_MODEL + c0:n * D_MODEL + c0 + half].astype(F32)
                             * jnp.dot(y_att, wb_ref[n, :, c0:c0 + half], preferred_element_type=F32))
        e = ext[:, g * LANES:(g + 1) * LANES]
        win = e
        step = 1
        while step < w:
            win = win + pltpu.roll(win, step, 0)
            step *= 2
        cnt = jnp.minimum(t + 1, w).astype(F32)
        d = win[POOL_HALO:, :] / cnt - e[POOL_HALO:, :]
        ys.append(jnp.dot(d.astype(BF16), pw_ref[g], preferred_element_type=F32))
    y_pool = jnp.concatenate(ys, axis=1) * ps_ref[...]

    br_pool = jnp.dot(y_pool.astype(BF16), wb_ref[0], preferred_element_type=F32)
    att = [jnp.concatenate([gated[(n, 0)], gated[(n, 1)]], axis=1) for n in (1, 2)]
    merged = gates_ref[0, :, 0:D_MODEL].astype(F32) * br_pool + att[0] + att[1]
    x1 = x_ref[0] + jnp.dot(merged.astype(BF16), wo_ref[...], preferred_element_type=F32)
    x1_ref[0] = x1

    hf = _rms(x1, ng_ref[...])
    _rows_to_slabs(hf_ref, (0,), hf)
    lg = _router_logits(hf, wrh_ref, wrl_ref, br_ref)
    lane = lax.broadcasted_iota(jnp.int32, (tm, LANES), 1)
    lane_f = lane.astype(F32)
    gmask = lane < N_GROUPS
    gl = jnp.where(gmask, lg, NEG)
    gmax = jnp.max(gl, axis=1, keepdims=True)
    gi = jnp.min(jnp.where(gmask & (gl == gmax), lane_f, float(LANES)), axis=1, keepdims=True)

    onehot = jnp.where(lane_f == gi, 1.0, 0.0)
    row = lax.broadcasted_iota(jnp.int32, (tm, tm), 0)
    col = lax.broadcasted_iota(jnp.int32, (tm, tm), 1)
    tri = jnp.where(row >= col, 1.0, 0.0).astype(BF16)
    incl = jnp.dot(tri, onehot.astype(BF16), preferred_element_type=F32)
    before = count_ref[0:1, :]
    rank = jnp.sum(onehot * (incl - onehot + before), axis=1, keepdims=True)
    count_ref[...] = jnp.broadcast_to(before + incl[tm - 1:tm, :], count_ref.shape)
    info_ref[0] = jnp.where(lane == 0, gi, jnp.where(lane == 1, rank, 0.0)).T[0:8, :]


def _merge(layer, x, u, yd, yf, gates, pw, ps, wb, wo, ng, wrh, wrl, brt):
    B, S, D = x.shape
    tm = TM_MERGE
    tok = lambda width: pl.BlockSpec((1, tm, width), lambda b, i: (b, i, 0))
    halo_blocks = tm // POOL_HALO
    halo = pl.BlockSpec((1, POOL_HALO, 512), lambda b, i: (b, jnp.maximum(i * halo_blocks - 1, 0), 0))
    return pl.pallas_call(
        _merge_kernel,
        grid=(B, S // tm),
        in_specs=[tok(D), tok(512), halo, tok(512), tok(512), tok(N_BRANCH * D),
                  _layer_spec(layer, (4, LANES, LANES)), _layer_spec(layer, (1, 512)),
                  _layer_spec(layer, (N_BRANCH, BRANCH_WIDTH, D)), _layer_spec(layer, (D, D)),
                  _layer_spec(layer, (1, D)), _layer_spec(layer, (D, LANES)), _layer_spec(layer, (D, LANES)),
                  _layer_spec(layer, (1, LANES))],
        out_specs=[tok(D), pl.BlockSpec((1, tm * SLAB, LANES), lambda b, i: (b, i, 0)),
                   pl.BlockSpec((1, 8, tm), lambda b, i: (b, 0, i)),
                   pl.BlockSpec((8, LANES), lambda b, i: (0, 0))],
        out_shape=[jax.ShapeDtypeStruct((B, S, D), F32),
                   jax.ShapeDtypeStruct((B, S * SLAB, LANES), F32),
                   jax.ShapeDtypeStruct((B, 8, S), F32),
                   jax.ShapeDtypeStruct((8, LANES), F32)],
        compiler_params=pltpu.CompilerParams(
            dimension_semantics=("arbitrary", "arbitrary"), vmem_limit_bytes=VMEM_LIMIT),
        name="merge",
    )(x, u, u, yd, yf, gates, pw, ps, wb, wo, ng, wrh, wrl, brt)


def _moe_kernel(gid_ref, nused_ref, src_ref, src_next_ref, dst_prev_ref,
                hf_hbm, wg_ref, wu_ref, wd_ref, wrh_ref, wrl_ref, br_ref,
                y_hbm,
                xbuf, ybuf, wgu_bf, wd_bf, sem_g, sem_s):
    i = pl.program_id(0)
    n_used = nused_ref[0]
    tm = TM_MOE
    n_tiles = pl.num_programs(0) - 1

    def slab(row):
        start = row * SLAB
        return pl.ds(start if isinstance(row, int) else pl.multiple_of(start, SLAB), SLAB)

    def gather_row(idx_ref, r, to_slot):
        return pltpu.make_async_copy(hf_hbm.at[slab(idx_ref[0, 0, r])], xbuf.at[to_slot, slab(r)],
                                     sem_g.at[to_slot])

    def scatter_row(r, from_slot):
        return pltpu.make_async_copy(ybuf.at[from_slot, slab(r)], y_hbm.at[slab(dst_prev_ref[0, 0, r])],
                                     sem_s.at[from_slot])

    def start_rows(make_copy, unrolled):
        if unrolled:
            for r in range(tm):
                make_copy(r).start()
        else:
            def body(r, c):
                make_copy(r).start()
                return c
            lax.fori_loop(0, tm, body, 0, unroll=8)

    def wait_gather(of_slot):
        pltpu.make_async_copy(hf_hbm.at[pl.ds(0, tm * SLAB)], xbuf.at[of_slot], sem_g.at[of_slot]).wait()

    def wait_scatter(of_slot):
        pltpu.make_async_copy(ybuf.at[of_slot], y_hbm.at[pl.ds(0, tm * SLAB)], sem_s.at[of_slot]).wait()

    def cast_weights():
        for e in range(EXPERTS_PER_GROUP):
            wgu_bf[:, e * D_EXPERT:(e + 1) * D_EXPERT] = wg_ref[e].astype(BF16)
            wgu_bf[:, GROUP_HIDDEN + e * D_EXPERT:GROUP_HIDDEN + (e + 1) * D_EXPERT] = wu_ref[e].astype(BF16)
            wd_bf[e * D_EXPERT:(e + 1) * D_EXPERT, :] = wd_ref[e].astype(BF16)

    def run_experts(slot, g):
        hf = _slabs_to_rows(xbuf, (slot,), tm)

        lg = _router_logits(hf, wrh_ref, wrl_ref, br_ref)
        lane = lax.broadcasted_iota(jnp.int32, (tm, LANES), 1)
        lane_f = lane.astype(F32)
        far = float(LANES)
        rmax = lambda a: jnp.max(a, axis=1, keepdims=True)
        rmin = lambda a: jnp.min(a, axis=1, keepdims=True)
        rsum = lambda a: jnp.sum(a, axis=1, keepdims=True)
        lg_g = rsum(jnp.where(lane == g, lg, 0.0))
        gp = 1.0 / rsum(jnp.where(lane < N_GROUPS, jnp.exp(lg - lg_g), 0.0))
        e_lo = N_GROUPS + EXPERTS_PER_GROUP * g
        emask = (lane >= e_lo) & (lane < e_lo + EXPERTS_PER_GROUP)
        el = jnp.where(emask, lg, NEG)
        m1 = rmax(el)
        i1 = rmin(jnp.where(emask & (el == m1), lane_f, far))
        rest = emask & (lane_f != i1)
        el2 = jnp.where(rest, lg, NEG)
        m2 = rmax(el2)
        i2 = rmin(jnp.where(rest & (el2 == m2), lane_f, far))
        r = jnp.exp(m2 - m1)
        comb = jnp.where(lane_f == i1, gp / (1.0 + r), jnp.where(lane_f == i2, gp * r / (1.0 + r), 0.0))
        cw = jnp.concatenate(
            [jnp.broadcast_to(rsum(jnp.where(lane == e_lo + e, comb, 0.0)), (tm, D_EXPERT))
             for e in range(EXPERTS_PER_GROUP)], axis=1)

        gu = jnp.dot(hf.astype(BF16), wgu_bf[...], preferred_element_type=F32)
        gate, up = gu[:, 0:GROUP_HIDDEN], gu[:, GROUP_HIDDEN:2 * GROUP_HIDDEN]
        act = gate * jax.nn.sigmoid(gate) * up * cw
        _rows_to_slabs(ybuf, (slot,), jnp.dot(act.astype(BF16), wd_bf[...], preferred_element_type=F32))

    tile = jnp.minimum(i, n_tiles - 1)
    g = gid_ref[tile]
    steady = (i >= 2) & (i + 1 < n_used)

    def steady_step(slot):
        wait_gather(slot)
        wait_scatter(slot)

        @pl.when(g != gid_ref[tile - 1])
        def _():
            cast_weights()

        start_rows(lambda r: gather_row(src_next_ref, r, 1 - slot), True)
        start_rows(lambda r: scatter_row(r, 1 - slot), True)
        run_experts(slot, g)

    for parity in range(2):
        @pl.when(steady & (lax.rem(i, 2) == parity))
        def _(parity=parity):
            steady_step(parity)

    @pl.when(jnp.logical_not(steady))
    def _():
        slot = lax.rem(i, 2)

        @pl.when(i == 0)
        def _():
            start_rows(lambda r: gather_row(src_ref, r, slot), False)
            ybuf[1] = jnp.zeros((tm * SLAB, LANES), F32)
            n_real = hf_hbm.shape[0]
            spare = [pltpu.make_async_copy(ybuf.at[1], y_hbm.at[pl.ds(n_real + k * tm * SLAB, tm * SLAB)], sem_s.at[1])
                     for k in range(N_GROUPS)]
            for cp in spare:
                cp.start()
            for cp in spare:
                cp.wait()

        @pl.when(i < n_used)
        def _():
            wait_gather(slot)

        @pl.when(i + 1 < n_used)
        def _():
            start_rows(lambda r: gather_row(src_next_ref, r, 1 - slot), False)

        @pl.when((i >= 1) & (i <= n_used))
        def _():
            start_rows(lambda r: scatter_row(r, 1 - slot), False)

        @pl.when(i < n_used)
        def _():
            @pl.when(i >= 2)
            def _():
                wait_scatter(slot)

            @pl.when((i == 0) | (g != gid_ref[jnp.maximum(tile - 1, 0)]))
            def _():
                cast_weights()

            run_experts(slot, g)

        @pl.when(i == n_used)
        def _():
            wait_scatter(1 - slot)

            @pl.when(n_used >= 2)
            def _():
                wait_scatter(slot)


def _moe(layer, gid, n_used, src, dst, hf, wg, wu, wd, wrh, wrl, brt):
    D = D_MODEL
    tm = TM_MOE
    n_tiles = src.shape[0]
    clamp = lambda t: jnp.clip(t, 0, n_tiles - 1)
    idx_spec = lambda shift: pl.BlockSpec(
        (1, 1, tm), lambda i, gid, nu: (clamp(i + shift), 0, 0), memory_space=pltpu.SMEM)
    grp = lambda shape: pl.BlockSpec((None,) + shape, lambda i, gid, nu: (layer, gid[clamp(i)], 0, 0))
    return pl.pallas_call(
        _moe_kernel,
        grid_spec=pltpu.PrefetchScalarGridSpec(
            num_scalar_prefetch=2,
            grid=(n_tiles + 1,),
            in_specs=[idx_spec(0), idx_spec(1), idx_spec(-1),
                      pl.BlockSpec(memory_space=pl.ANY),
                      grp((EXPERTS_PER_GROUP, D, D_EXPERT)), grp((EXPERTS_PER_GROUP, D, D_EXPERT)),
                      grp((EXPERTS_PER_GROUP, D_EXPERT, D)),
                      _layer_spec(layer, (D, LANES)), _layer_spec(layer, (D, LANES)), _layer_spec(layer, (1, LANES))],
            out_specs=pl.BlockSpec(memory_space=pl.ANY),
            scratch_shapes=[pltpu.VMEM((2, tm * SLAB, LANES), F32), pltpu.VMEM((2, tm * SLAB, LANES), F32),
                            pltpu.VMEM((D, 2 * GROUP_HIDDEN), BF16), pltpu.VMEM((GROUP_HIDDEN, D), BF16),
                            pltpu.SemaphoreType.DMA((2,)), pltpu.SemaphoreType.DMA((2,))]),
        out_shape=jax.ShapeDtypeStruct((n_tiles * tm * SLAB, LANES), F32),
        compiler_params=pltpu.CompilerParams(
            dimension_semantics=("arbitrary",), vmem_limit_bytes=VMEM_LIMIT),
        name="moe",
    )(gid, n_used, src, src, dst, hf, wg, wu, wd, wrh, wrl, brt)


def _ple_kernel(x1_ref, y_ref, p_ref, pg_ref, wpg_ref, wp_ref, o_ref):
    x2 = x1_ref[...] + _slabs_to_rows(y_ref, (), x1_ref.shape[0])
    hp = _rms(x2, pg_ref[...]).astype(BF16)
    gate = jax.nn.sigmoid(jnp.dot(hp, wpg_ref[...], preferred_element_type=F32))
    emb = jnp.dot(p_ref[...].astype(BF16), wp_ref[...], preferred_element_type=F32)
    o_ref[...] = x2 + gate * emb


def _ple(layer, x1, y, p, pg, wpg, wp):
    T, D = x1.shape
    tm = TM_PLE
    tok = lambda width: pl.BlockSpec((tm, width), lambda i: (i, 0))
    return pl.pallas_call(
        _ple_kernel,
        grid=(T // tm,),
        in_specs=[tok(D), pl.BlockSpec((tm * SLAB, LANES), lambda i: (i, 0)),
                  pl.BlockSpec((None, tm, PLE_DIM), lambda i: (layer, i, 0)),
                  _layer_spec(layer, (1, D)), _layer_spec(layer, (D, D)), _layer_spec(layer, (PLE_DIM, D))],
        out_specs=tok(D),
        out_shape=jax.ShapeDtypeStruct((T, D), F32),
        compiler_params=pltpu.CompilerParams(
            dimension_semantics=("parallel",), vmem_limit_bytes=VMEM_LIMIT),
        name="ple",
    )(x1, y, p, pg, wpg, wp)


def _routing_tables(info, counts):
    tm = TM_MOE
    T = info.shape[0] * info.shape[2]
    n_rows = T + N_GROUPS * tm
    n_tiles = n_rows // tm
    group = info[:, 0, :].reshape(T).astype(jnp.int32)
    rank = info[:, 1, :].reshape(T).astype(jnp.int32)
    cnt = counts[0, :N_GROUPS].astype(jnp.int32)
    padded = (cnt + tm - 1) // tm * tm
    ends = jnp.cumsum(padded)
    base = ends - padded
    pos = base[group] + rank
    tok_of_row = jnp.full((n_rows,), -1, jnp.int32).at[pos].set(jnp.arange(T, dtype=jnp.int32))
    real = tok_of_row >= 0
    src = jnp.where(real, tok_of_row, 0)
    pad_index = jnp.cumsum(jnp.where(real, 0, 1)) - 1
    dst = jnp.where(real, tok_of_row, T + pad_index)
    starts = jnp.arange(n_tiles, dtype=jnp.int32) * tm
    gid = jnp.minimum(jnp.sum(starts[:, None] >= ends[None, :], axis=1), N_GROUPS - 1).astype(jnp.int32)
    n_used = (ends[N_GROUPS - 1] // tm).astype(jnp.int32)[None]
    gid = jnp.where(starts < ends[N_GROUPS - 1], gid, gid[jnp.maximum(n_used[0] - 1, 0)])
    shape3 = (n_tiles, 1, tm)
    return gid, n_used, src.reshape(shape3), dst.reshape(shape3)


def _rotary_table(positions):
    half = ROT_DIM // 2
    per_row = LANES // half
    inv_freq = ROPE_THETA ** (-jnp.arange(0, ROT_DIM, 2, dtype=F32) / ROT_DIM)
    B, S = positions.shape
    pos = jnp.repeat(positions.astype(F32).reshape(B, S // per_row, per_row), half, axis=-1)
    ang = pos * jnp.tile(inv_freq, per_row)
    c = jnp.cos(ang).reshape(B, S, half)
    s = jnp.sin(ang).reshape(B, S, half)
    return jnp.concatenate([c, s], axis=-1)


def _rotary_placement():
    half = ROT_DIM // 2
    e = np.zeros((2 * half, 3 * LANES), np.float32)
    one = np.zeros((1, LANES), np.float32)
    for lane in range(LANES):
        d = lane % HEAD_DIM
        if d < half:
            e[d, lane] = 1.0
            e[half + d, LANES + lane] = -1.0
        elif d < ROT_DIM:
            e[d - half, lane] = 1.0
            e[d, 2 * LANES + lane] = 1.0
        else:
            one[0, lane] = 1.0
    return jnp.asarray(e, BF16), jnp.asarray(one, F32)


def _pack_w_in(w):
    n_main = C_FZ
    fz = jnp.pad(w[:, :, n_main:n_main + FOX_HEADS], ((0, 0), (0, 0), (0, LANES - FOX_HEADS)))
    return jnp.concatenate([w[:, :, :n_main], fz, w[:, :, n_main + FOX_HEADS:]], axis=2).astype(BF16)


def kernel(x, p, positions, attn_norm_g, w_in, pool_w, pool_scale, diff_qn_g, diff_kn_g, diff_lambda, diff_subln_g, fox_qn_g, fox_kn_g, fox_forget_b, w_branch, w_out, ffn_norm_g, w_route_group, b_route_group, w_route_expert, b_route_expert, moe_w_gate, moe_w_up, moe_w_down, ple_norm_g, w_ple_gate, w_ple):
    B, S, D = x.shape
    T = B * S
    depth = w_in.shape[0]
    two = lambda a: jnp.concatenate([a, a], axis=-1)
    pad_lanes = lambda a: jnp.pad(a, [(0, 0)] * (a.ndim - 1) + [(0, LANES - a.shape[-1])])
    row = lambda a: a[:, None, :]

    tab = _rotary_table(positions)
    rot_e, rot_one = _rotary_placement()
    w_in_p = _pack_w_in(w_in)
    qkg = jnp.stack([two(diff_qn_g), two(diff_kn_g), two(fox_qn_g), two(fox_kn_g)], axis=1)
    fb = row(pad_lanes(fox_forget_b))
    w_r = jnp.concatenate([w_route_group, w_route_expert], axis=2)
    w_r_hi = w_r.astype(BF16)
    wrh = pad_lanes(w_r_hi)
    wrl = pad_lanes((w_r - w_r_hi.astype(F32)).astype(BF16))
    brt = row(pad_lanes(jnp.concatenate([b_route_group, b_route_expert], axis=1)))
    pool_w_b, w_branch_b, w_out_b = pool_w.astype(BF16), w_branch.astype(BF16), w_out.astype(BF16)
    w_ple_gate_b, w_ple_b = w_ple_gate.astype(BF16), w_ple.astype(BF16)
    p_flat = p.reshape(depth, T, PLE_DIM)

    for l in range(depth):
        lam_init = 0.8 - 0.6 * math.exp(-0.3 * l)
        u, dq, dk, dv, fq, fk, fv, f, ft, gates = _inproj(
            l, x, row(attn_norm_g), w_in_p, tab, rot_e, rot_one, qkg, fb)
        y_diff = _diff_attention(l, dq, dk, dv, diff_lambda, row(diff_subln_g), lam_init)
        y_fox = _fox_attention(fq, fk, fv, f, ft)
        x1, hf, info, counts = _merge(
            l, x, u, y_diff, y_fox, gates, pool_w_b, row(pool_scale), w_branch_b, w_out_b,
            row(ffn_norm_g), wrh, wrl, brt)
        gid, n_used, src, dst = _routing_tables(info, counts)
        y = _moe(l, gid, n_used, src, dst, hf.reshape(T * SLAB, LANES), moe_w_gate, moe_w_up, moe_w_down,
                 wrh, wrl, brt)
        x = _ple(l, x1.reshape(T, D), y, p_flat, row(ple_norm_g), w_ple_gate_b, w_ple_b).reshape(B, S, D)
    return x
```

```python
import functools
import math

import jax
import jax.numpy as jnp
import numpy as np
from jax import lax
from jax.experimental import pallas as pl
from jax.experimental.pallas import tpu as pltpu

F32 = jnp.float32
BF16 = jnp.bfloat16

D_MODEL = 1024
HEAD_DIM = 64
LANES = 128
POOL_WINDOWS = (2, 4, 8, 16)
POOL_HALO = 16
BRANCH_WIDTH = 512
N_BRANCH = 3
FOX_HEADS = 8
ROT_DIM = HEAD_DIM // 4
ROPE_THETA = 500000.0
PLE_DIM = 256
N_GROUPS = 4
EXPERTS_PER_GROUP = 4
N_EXPERTS = 16
D_EXPERT = 256
RMS_EPS = 1e-6
NEG = -1e30
LOG2E = math.log2(math.e)

C_POOL, C_DQ, C_DK, C_DV, C_FQ, C_FK, C_FV, C_FZ, C_GZ = (0, 512, 1024, 1536, 2048, 2560, 3072, 3584, 3712)
IN_PACKED = C_GZ + N_BRANCH * D_MODEL

TM_IN = 512
TM_MERGE = 512
TM_MOE = 512
TM_PLE = 1024
GROUP_HIDDEN = EXPERTS_PER_GROUP * D_EXPERT
TQ = 256
VMEM_LIMIT = 56 * 1024 * 1024


def _const_spec(shape):
    zeros = (0,) * len(shape)
    return pl.BlockSpec(shape, lambda *_: zeros, pipeline_mode=pl.Buffered(1))


def _layer_spec(layer, shape):
    zeros = (0,) * len(shape)
    return pl.BlockSpec((None,) + tuple(shape), lambda *_: (layer,) + zeros, pipeline_mode=pl.Buffered(1))


def _rms(x, g):
    return x * lax.rsqrt(jnp.mean(x * x, axis=-1, keepdims=True) + RMS_EPS) * g


SLAB = D_MODEL // LANES


def _rows_to_slabs(dst, lead, x, row0=0):
    n = x.shape[0]
    for c in range(SLAB):
        dst[lead + (pl.ds(row0 * SLAB + c, n, stride=SLAB), slice(None))] = x[:, c * LANES:(c + 1) * LANES]


def _slabs_to_rows(src, lead, n):
    return jnp.concatenate([src[lead + (pl.ds(c, n, stride=SLAB), slice(None))] for c in range(SLAB)], axis=1)


def _router_logits(hf, wrh_ref, wrl_ref, br_ref):
    hi = hf.astype(BF16)
    lo = (hf - hi.astype(F32)).astype(BF16)
    return (jnp.dot(hi, wrh_ref[...], preferred_element_type=F32)
            + jnp.dot(lo, wrh_ref[...], preferred_element_type=F32)
            + jnp.dot(hi, wrl_ref[...], preferred_element_type=F32)
            + br_ref[...])


def _inproj_kernel(x_ref, g_ref, w_ref, tab_ref, rot_e_ref, rot_one_ref, qkg_ref, fb_ref,
                   u_ref, dq_ref, dk_ref, dv_ref, fq_ref, fk_ref, fv_ref, f_ref, ft_ref, gates_ref,
                   carry_ref):
    i = pl.program_id(1)
    tm = x_ref.shape[1]

    @pl.when(i == 0)
    def _():
        carry_ref[...] = jnp.zeros_like(carry_ref)

    h = _rms(x_ref[0], g_ref[...]).astype(BF16)

    def proj(c0, width):
        return jnp.dot(h, w_ref[:, c0:c0 + width], preferred_element_type=F32)

    lo = lax.broadcasted_iota(jnp.int32, (tm, LANES), 1) < HEAD_DIM
    tab = tab_ref[0]
    tab_hi = tab.astype(BF16)
    tab_lo = (tab - tab_hi.astype(F32)).astype(BF16)
    placed = (jnp.dot(tab_hi, rot_e_ref[...], preferred_element_type=F32)
              + jnp.dot(tab_lo, rot_e_ref[...], preferred_element_type=F32))
    cos = placed[:, 0:LANES] + rot_one_ref[...]
    sa = placed[:, LANES:2 * LANES]
    sb = placed[:, 2 * LANES:3 * LANES]
    scale = HEAD_DIM ** -0.5 * LOG2E

    def head_norm(a, g):
        sq = a * a
        s_lo = jnp.sum(jnp.where(lo, sq, 0.0), axis=-1, keepdims=True)
        s_hi = jnp.sum(jnp.where(lo, 0.0, sq), axis=-1, keepdims=True)
        ss = jnp.where(lo, s_lo, s_hi)
        return a * lax.rsqrt(ss * (1.0 / HEAD_DIM) + RMS_EPS) * g

    def rotary(a):
        return a * cos + pltpu.roll(a, LANES - ROT_DIM // 2, 1) * sa + pltpu.roll(a, ROT_DIM // 2, 1) * sb

    def plain(out_ref):
        def epilogue(acc):
            out_ref[0] = acc.astype(BF16)
        return epilogue

    def qk(out_ref, g, rot, mult):
        def epilogue(acc):
            for j in range(4):
                a = head_norm(acc[:, j * LANES:(j + 1) * LANES], g)
                if rot:
                    a = rotary(a)
                if mult != 1.0:
                    a = a * mult
                out_ref[0, :, j * LANES:(j + 1) * LANES] = a.astype(BF16)
        return epilogue

    def forget(acc):
        z = acc + fb_ref[...]
        logf = jnp.minimum(z, 0.0) - jnp.log1p(jnp.exp(-jnp.abs(z)))
        row = lax.broadcasted_iota(jnp.int32, (tm, tm), 0)
        col = lax.broadcasted_iota(jnp.int32, (tm, tm), 1)
        tri = jnp.where(row >= col, 1.0, 0.0).astype(BF16)
        hi = logf.astype(BF16)
        lo_part = (logf - hi.astype(F32)).astype(BF16)
        fcum = (jnp.dot(tri, hi, preferred_element_type=F32)
                + jnp.dot(tri, lo_part, preferred_element_type=F32)
                + carry_ref[0:1, :])
        carry_ref[...] = jnp.broadcast_to(fcum[tm - 1:tm, :], carry_ref.shape)
        f_ref[0] = fcum
        ft_ref[0] = fcum.T[0:FOX_HEADS, :]

    def gate(c):
        def epilogue(acc):
            gates_ref[0, :, c * 512:(c + 1) * 512] = jax.nn.sigmoid(acc).astype(BF16)
        return epilogue

    chunks = [(C_POOL, 512, plain(u_ref)),
              (C_DQ, 512, qk(dq_ref, qkg_ref[0:1, :], True, scale)),
              (C_DK, 512, qk(dk_ref, qkg_ref[1:2, :], True, 1.0)),
              (C_DV, 512, plain(dv_ref)),
              (C_FQ, 512, qk(fq_ref, qkg_ref[2:3, :], False, scale)),
              (C_FK, 512, qk(fk_ref, qkg_ref[3:4, :], False, 1.0)),
              (C_FV, 512, plain(fv_ref)),
              (C_FZ, LANES, forget)]
    chunks += [(C_GZ + c * 512, 512, gate(c)) for c in range(N_BRANCH * D_MODEL // 512)]
    acc = proj(chunks[0][0], chunks[0][1])
    for k, (_, _, epilogue) in enumerate(chunks):
        nxt = proj(chunks[k + 1][0], chunks[k + 1][1]) if k + 1 < len(chunks) else None
        epilogue(acc)
        acc = nxt


def _inproj(layer, x, g, w, tab, rot_e, rot_one, qkg, fb):
    B, S, D = x.shape
    tm = TM_IN
    tok = lambda width: pl.BlockSpec((1, tm, width), lambda b, i: (b, i, 0))
    out_shape = (
        [jax.ShapeDtypeStruct((B, S, 512), BF16)] * 7
        + [jax.ShapeDtypeStruct((B, S, LANES), F32),
           jax.ShapeDtypeStruct((B, FOX_HEADS, S), F32),
           jax.ShapeDtypeStruct((B, S, N_BRANCH * D), BF16)]
    )
    out_specs = (
        [tok(512)] * 7
        + [tok(LANES),
           pl.BlockSpec((1, FOX_HEADS, tm), lambda b, i: (b, 0, i)),
           tok(N_BRANCH * D)]
    )
    return pl.pallas_call(
        _inproj_kernel,
        grid=(B, S // tm),
        in_specs=[tok(D), _layer_spec(layer, (1, D)), _layer_spec(layer, (D, IN_PACKED)),
                  tok(2 * (ROT_DIM // 2)), _const_spec((2 * (ROT_DIM // 2), 3 * LANES)), _const_spec((1, LANES)),
                  _layer_spec(layer, (4, LANES)), _layer_spec(layer, (1, LANES))],
        out_specs=out_specs,
        out_shape=out_shape,
        scratch_shapes=[pltpu.VMEM((8, LANES), F32)],
        compiler_params=pltpu.CompilerParams(
            dimension_semantics=("parallel", "arbitrary"), vmem_limit_bytes=VMEM_LIMIT),
        name="inproj",
    )(x, g, w, tab, rot_e, rot_one, qkg, fb)


def _nt_dot(a, b):
    return lax.dot_general(a, b, (((1,), (1,)), ((), ())), preferred_element_type=F32)


def _score_rows(score_chunk, n_chunks, s_scr):
    tq = TQ
    mt = jnp.full((tq, LANES), NEG, F32)
    for c in range(n_chunks):
        s = score_chunk(c)
        s_scr[:, c * tq:(c + 1) * tq] = s
        for w in range(tq // LANES):
            mt = jnp.maximum(mt, s[:, w * LANES:(w + 1) * LANES])
    return jnp.max(mt, axis=1, keepdims=True)


def _exp_rows(n_chunks, m, s_scr, p_scr):
    tq = TQ
    for c in range(n_chunks):
        p_scr[:, c * tq:(c + 1) * tq] = jnp.exp2(s_scr[:, c * tq:(c + 1) * tq] - m).astype(BF16)


def _diff_kernel(lam_init, q_ref, k_ref, v_ref, lam_ref, g_ref, o_ref, s_scr, p_scr, v2_scr):
    S = q_ref.shape[1]
    tq = TQ
    lo = lax.broadcasted_iota(jnp.int32, (tq, LANES), 1) < HEAD_DIM
    causal = (lax.broadcasted_iota(jnp.int32, (tq, tq), 1) <= lax.broadcasted_iota(jnp.int32, (tq, tq), 0))
    lp = lam_ref[...]
    lam = (jnp.exp(jnp.sum(lp[0:1, :] * lp[1:2, :], axis=1, keepdims=True))
           - jnp.exp(jnp.sum(lp[2:3, :] * lp[3:4, :], axis=1, keepdims=True)) + lam_init)
    v2_scr[:, 0:LANES] = v_ref[0]
    v2_scr[:, LANES:2 * LANES] = jnp.ones((S, LANES), BF16)

    def scores(qi, comp):
        q = q_ref[0, qi * tq:(qi + 1) * tq, :]
        zero = jnp.zeros_like(q)
        qm = jnp.where(lo, q, zero) if comp == 0 else jnp.where(lo, zero, q)

        def score_chunk(c):
            s = _nt_dot(qm, k_ref[0, c * tq:(c + 1) * tq, :])
            return jnp.where(causal, s, NEG) if c == qi else s

        return _score_rows(score_chunk, qi + 1, s_scr.at[2 * (qi % 2) + comp])

    n_q = S // tq
    row_max = {(0, comp): scores(0, comp) for comp in range(2)}
    for qi in range(n_q):
        kend = (qi + 1) * tq
        parts = []
        for comp in range(2):
            _exp_rows(qi + 1, row_max[(qi, comp)], s_scr.at[2 * (qi % 2) + comp], p_scr.at[comp])
            if qi + 1 < n_q:
                row_max[(qi + 1, comp)] = scores(qi + 1, comp)
            pv = jnp.dot(p_scr[comp, :, 0:kend], v2_scr[0:kend, :], preferred_element_type=F32)
            parts.append(pv[:, 0:LANES] / pv[:, LANES:2 * LANES])
        o = parts[0] - lam * parts[1]
        o = _rms(o, g_ref[...]) * (1.0 - lam_init)
        o_ref[0, qi * tq:kend, :] = o.astype(BF16)


def _diff_attention(layer, dq, dk, dv, lam_p, subln_g, lam_init):
    B, S, _ = dq.shape
    blk = pl.BlockSpec((1, S, LANES), lambda b, h: (b, 0, h))
    return pl.pallas_call(
        functools.partial(_diff_kernel, lam_init),
        grid=(B, 4),
        in_specs=[blk, blk, blk, _layer_spec(layer, (4, HEAD_DIM)), _layer_spec(layer, (1, LANES))],
        out_specs=blk,
        out_shape=jax.ShapeDtypeStruct((B, S, 512), BF16),
        scratch_shapes=[pltpu.VMEM((4, TQ, S), F32), pltpu.VMEM((2, TQ, S), BF16),
                        pltpu.VMEM((S, 2 * LANES), BF16)],
        compiler_params=pltpu.CompilerParams(
            dimension_semantics=("parallel", "parallel"), vmem_limit_bytes=VMEM_LIMIT),
        name="diff_attn",
    )(dq, dk, dv, lam_p, subln_g)


def _fox_kernel(q_ref, k_ref, v_ref, f_ref, ft_ref, o_ref, s_scr, p_scr, v2_scr):
    S = q_ref.shape[1]
    tq = TQ
    j = pl.program_id(1)
    lane = lax.broadcasted_iota(jnp.int32, (tq, LANES), 1)
    lo = lane < HEAD_DIM
    causal = (lax.broadcasted_iota(jnp.int32, (tq, tq), 1) <= lax.broadcasted_iota(jnp.int32, (tq, tq), 0))
    v = v_ref[0]
    lo_s = lax.broadcasted_iota(jnp.int32, (S, LANES), 1) < HEAD_DIM
    one = jnp.ones_like(v)
    v2_scr[0] = jnp.where(lo_s, v, one)
    v2_scr[1] = jnp.where(lo_s, one, v)

    def scores(qi, hh):
        q = q_ref[0, qi * tq:(qi + 1) * tq, :]
        zero = jnp.zeros_like(q)
        qm = jnp.where(lo, q, zero) if hh == 0 else jnp.where(lo, zero, q)
        f_tile = f_ref[0, qi * tq:(qi + 1) * tq, :] * LOG2E
        f_t = jnp.sum(jnp.where(lane == 2 * j + hh, f_tile, 0.0), axis=1, keepdims=True)

        def score_chunk(c):
            f_s = ft_ref[0, pl.ds(2 * j + hh, 1), c * tq:(c + 1) * tq] * LOG2E
            s = _nt_dot(qm, k_ref[0, c * tq:(c + 1) * tq, :]) + (f_t - f_s)
            return jnp.where(causal, s, NEG) if c == qi else s

        return _score_rows(score_chunk, qi + 1, s_scr.at[2 * (qi % 2) + hh])

    n_q = S // tq
    row_max = {(0, hh): scores(0, hh) for hh in range(2)}
    for qi in range(n_q):
        kend = (qi + 1) * tq
        pvs = []
        for hh in range(2):
            _exp_rows(qi + 1, row_max[(qi, hh)], s_scr.at[2 * (qi % 2) + hh], p_scr.at[hh])
            if qi + 1 < n_q:
                row_max[(qi + 1, hh)] = scores(qi + 1, hh)
            pvs.append(jnp.dot(p_scr[hh, :, 0:kend], v2_scr[hh, 0:kend, :], preferred_element_type=F32))
        o = jnp.where(lo, pvs[0] / pltpu.roll(pvs[0], HEAD_DIM, 1), pvs[1] / pltpu.roll(pvs[1], HEAD_DIM, 1))
        o_ref[0, qi * tq:kend, :] = o.astype(BF16)


def _fox_attention(fq, fk, fv, f, ft):
    B, S, _ = fq.shape
    blk = pl.BlockSpec((1, S, LANES), lambda b, j: (b, 0, j))
    return pl.pallas_call(
        _fox_kernel,
        grid=(B, 4),
        in_specs=[blk, blk, blk,
                  pl.BlockSpec((1, S, LANES), lambda b, j: (b, 0, 0)),
                  pl.BlockSpec((1, FOX_HEADS, S), lambda b, j: (b, 0, 0))],
        out_specs=blk,
        out_shape=jax.ShapeDtypeStruct((B, S, 512), BF16),
        scratch_shapes=[pltpu.VMEM((4, TQ, S), F32), pltpu.VMEM((2, TQ, S), BF16),
                        pltpu.VMEM((2, S, LANES), BF16)],
        compiler_params=pltpu.CompilerParams(
            dimension_semantics=("parallel", "parallel"), vmem_limit_bytes=VMEM_LIMIT),
        name="fox_attn",
    )(fq, fk, fv, f, ft)


def _merge_kernel(x_ref, u_ref, uh_ref, yd_ref, yf_ref, gates_ref, pw_ref, ps_ref, wb_ref, wo_ref,
                  ng_ref, wrh_ref, wrl_ref, br_ref,
                  x1_ref, hf_ref, info_ref, count_ref):
    i = pl.program_id(1)
    tm = x_ref.shape[1]

    @pl.when((pl.program_id(0) == 0) & (i == 0))
    def _():
        count_ref[...] = jnp.zeros_like(count_ref)

    hm = tm // 2
    half = D_MODEL // 2
    first_halo = uh_ref[0].astype(F32) * jnp.where(i > 0, 1.0, 0.0)

    def mix(h):
        r0 = h * hm
        halo = first_halo if h == 0 else u_ref[0, r0 - POOL_HALO:r0, :].astype(F32)
        ext = jnp.concatenate([halo, u_ref[0, r0:r0 + hm, :].astype(F32)], axis=0)
        t = i * tm + r0 + lax.broadcasted_iota(jnp.int32, (hm, 1), 0)
        ys, gated = [], {}
        for g, w in enumerate(POOL_WINDOWS):
            n, c0 = 1 + g // 2, (g % 2) * half
            y_att = (yd_ref if n == 1 else yf_ref)[0, r0:r0 + hm, :]
            gated[(n, g % 2)] = (gates_ref[0, r0:r0 + hm, n * D_MODEL + c0:n * D_MODEL + c0 + half].astype(F32)
                                 * jnp.dot(y_att, wb_ref[n, :, c0:c0 + half], preferred_element_type=F32))
            e = ext[:, g * LANES:(g + 1) * LANES]
            win = e
            step = 1
            while step < w:
                win = win + pltpu.roll(win, step, 0)
                step *= 2
            cnt = jnp.minimum(t + 1, w).astype(F32)
            d = win[POOL_HALO:, :] / cnt - e[POOL_HALO:, :]
            ys.append(jnp.dot(d.astype(BF16), pw_ref[g], preferred_element_type=F32))
        return jnp.concatenate(ys, axis=1) * ps_ref[...], gated

    def project(h, y_pool, gated):
        r0 = h * hm
        br_pool = jnp.dot(y_pool.astype(BF16), wb_ref[0], preferred_element_type=F32)
        att = [jnp.concatenate([gated[(n, 0)], gated[(n, 1)]], axis=1) for n in (1, 2)]
        merged = gates_ref[0, r0:r0 + hm, 0:D_MODEL].astype(F32) * br_pool + att[0] + att[1]
        x1 = x_ref[0, r0:r0 + hm, :] + jnp.dot(merged.astype(BF16), wo_ref[...], preferred_element_type=F32)
        x1_ref[0, r0:r0 + hm, :] = x1
        return x1

    def route(h, x1):
        hf = _rms(x1, ng_ref[...])
        _rows_to_slabs(hf_ref, (0,), hf, h * hm)
        return _router_logits(hf, wrh_ref, wrl_ref, br_ref)

    mixed = [mix(h) for h in range(2)]
    x1s = [project(h, *mixed[h]) for h in range(2)]
    lg = jnp.concatenate([route(h, x1s[h]) for h in range(2)], axis=0)

    lane = lax.broadcasted_iota(jnp.int32, (tm, LANES), 1)
    lane_f = lane.astype(F32)
    gmask = lane < N_GROUPS
    gl = jnp.where(gmask, lg, NEG)
    gmax = jnp.max(gl, axis=1, keepdims=True)
    gi = jnp.min(jnp.where(gmask & (gl == gmax), lane_f, float(LANES)), axis=1, keepdims=True)

    onehot = jnp.where(lane_f == gi, 1.0, 0.0)
    row = lax.broadcasted_iota(jnp.int32, (tm, tm), 0)
    col = lax.broadcasted_iota(jnp.int32, (tm, tm), 1)
    tri = jnp.where(row >= col, 1.0, 0.0).astype(BF16)
    incl = jnp.dot(tri, onehot.astype(BF16), preferred_element_type=F32)
    before = count_ref[0:1, :]
    rank = jnp.sum(onehot * (incl - onehot + before), axis=1, keepdims=True)
    count_ref[...] = jnp.broadcast_to(before + incl[tm - 1:tm, :], count_ref.shape)
    info_ref[0] = jnp.where(lane == 0, gi, jnp.where(lane == 1, rank, 0.0)).T[0:8, :]


def _merge(layer, x, u, yd, yf, gates, pw, ps, wb, wo, ng, wrh, wrl, brt):
    B, S, D = x.shape
    tm = TM_MERGE
    tok = lambda width: pl.BlockSpec((1, tm, width), lambda b, i: (b, i, 0))
    halo_blocks = tm // POOL_HALO
    halo = pl.BlockSpec((1, POOL_HALO, 512), lambda b, i: (b, jnp.maximum(i * halo_blocks - 1, 0), 0))
    return pl.pallas_call(
        _merge_kernel,
        grid=(B, S // tm),
        in_specs=[tok(D), tok(512), halo, tok(512), tok(512), tok(N_BRANCH * D),
                  _layer_spec(layer, (4, LANES, LANES)), _layer_spec(layer, (1, 512)),
                  _layer_spec(layer, (N_BRANCH, BRANCH_WIDTH, D)), _layer_spec(layer, (D, D)),
                  _layer_spec(layer, (1, D)), _layer_spec(layer, (D, LANES)), _layer_spec(layer, (D, LANES)),
                  _layer_spec(layer, (1, LANES))],
        out_specs=[tok(D), pl.BlockSpec((1, tm * SLAB, LANES), lambda b, i: (b, i, 0)),
                   pl.BlockSpec((1, 8, tm), lambda b, i: (b, 0, i)),
                   pl.BlockSpec((8, LANES), lambda b, i: (0, 0))],
        out_shape=[jax.ShapeDtypeStruct((B, S, D), F32),
                   jax.ShapeDtypeStruct((B, S * SLAB, LANES), F32),
                   jax.ShapeDtypeStruct((B, 8, S), F32),
                   jax.ShapeDtypeStruct((8, LANES), F32)],
        compiler_params=pltpu.CompilerParams(
            dimension_semantics=("arbitrary", "arbitrary"), vmem_limit_bytes=VMEM_LIMIT),
        name="merge",
    )(x, u, u, yd, yf, gates, pw, ps, wb, wo, ng, wrh, wrl, brt)


def _moe_kernel(gid_ref, nused_ref, src_ref, src_next_ref, dst_prev_ref,
                hf_hbm, wg_ref, wu_ref, wd_ref, wrh_ref, wrl_ref, br_ref,
                y_hbm,
                xbuf, ybuf, wgu_bf, wd_bf, sem_g, sem_s):
    i = pl.program_id(0)
    n_used = nused_ref[0]
    tm = TM_MOE
    n_tiles = pl.num_programs(0) - 1

    def slab(row):
        start = row * SLAB
        return pl.ds(start if isinstance(row, int) else pl.multiple_of(start, SLAB), SLAB)

    def gather_row(idx_ref, r, to_slot):
        return pltpu.make_async_copy(hf_hbm.at[slab(idx_ref[0, 0, r])], xbuf.at[to_slot, slab(r)],
                                     sem_g.at[to_slot])

    def scatter_row(r, from_slot):
        return pltpu.make_async_copy(ybuf.at[from_slot, slab(r)], y_hbm.at[slab(dst_prev_ref[0, 0, r])],
                                     sem_s.at[from_slot])

    def start_rows(make_copy, unrolled):
        if unrolled:
            for r in range(tm):
                make_copy(r).start()
        else:
            def body(r, c):
                make_copy(r).start()
                return c
            lax.fori_loop(0, tm, body, 0, unroll=8)

    def wait_gather(of_slot):
        pltpu.make_async_copy(hf_hbm.at[pl.ds(0, tm * SLAB)], xbuf.at[of_slot], sem_g.at[of_slot]).wait()

    def wait_scatter(of_slot):
        pltpu.make_async_copy(ybuf.at[of_slot], y_hbm.at[pl.ds(0, tm * SLAB)], sem_s.at[of_slot]).wait()

    def cast_weights():
        for e in range(EXPERTS_PER_GROUP):
            wgu_bf[:, e * D_EXPERT:(e + 1) * D_EXPERT] = wg_ref[e].astype(BF16)
            wgu_bf[:, GROUP_HIDDEN + e * D_EXPERT:GROUP_HIDDEN + (e + 1) * D_EXPERT] = wu_ref[e].astype(BF16)
            wd_bf[e * D_EXPERT:(e + 1) * D_EXPERT, :] = wd_ref[e].astype(BF16)

    def run_experts(slot, g):
        hf = _slabs_to_rows(xbuf, (slot,), tm)

        lg = _router_logits(hf, wrh_ref, wrl_ref, br_ref)
        lane = lax.broadcasted_iota(jnp.int32, (tm, LANES), 1)
        lane_f = lane.astype(F32)
        far = float(LANES)
        rmax = lambda a: jnp.max(a, axis=1, keepdims=True)
        rmin = lambda a: jnp.min(a, axis=1, keepdims=True)
        rsum = lambda a: jnp.sum(a, axis=1, keepdims=True)
        lg_g = rsum(jnp.where(lane == g, lg, 0.0))
        gp = 1.0 / rsum(jnp.where(lane < N_GROUPS, jnp.exp(lg - lg_g), 0.0))
        e_lo = N_GROUPS + EXPERTS_PER_GROUP * g
        emask = (lane >= e_lo) & (lane < e_lo + EXPERTS_PER_GROUP)
        el = jnp.where(emask, lg, NEG)
        m1 = rmax(el)
        i1 = rmin(jnp.where(emask & (el == m1), lane_f, far))
        rest = emask & (lane_f != i1)
        el2 = jnp.where(rest, lg, NEG)
        m2 = rmax(el2)
        i2 = rmin(jnp.where(rest & (el2 == m2), lane_f, far))
        r = jnp.exp(m2 - m1)
        comb = jnp.where(lane_f == i1, gp / (1.0 + r), jnp.where(lane_f == i2, gp * r / (1.0 + r), 0.0))
        cw = jnp.concatenate(
            [jnp.broadcast_to(rsum(jnp.where(lane == e_lo + e, comb, 0.0)), (tm, D_EXPERT))
             for e in range(EXPERTS_PER_GROUP)], axis=1)

        gu = jnp.dot(hf.astype(BF16), wgu_bf[...], preferred_element_type=F32)
        gate, up = gu[:, 0:GROUP_HIDDEN], gu[:, GROUP_HIDDEN:2 * GROUP_HIDDEN]
        act = gate * jax.nn.sigmoid(gate) * up * cw
        _rows_to_slabs(ybuf, (slot,), jnp.dot(act.astype(BF16), wd_bf[...], preferred_element_type=F32))

    tile = jnp.minimum(i, n_tiles - 1)
    g = gid_ref[tile]
    steady = (i >= 2) & (i + 1 < n_used)

    def steady_step(slot):
        wait_gather(slot)
        wait_scatter(slot)

        @pl.when(g != gid_ref[tile - 1])
        def _():
            cast_weights()

        start_rows(lambda r: gather_row(src_next_ref, r, 1 - slot), True)
        start_rows(lambda r: scatter_row(r, 1 - slot), True)
        run_experts(slot, g)

    for parity in range(2):
        @pl.when(steady & (lax.rem(i, 2) == parity))
        def _(parity=parity):
            steady_step(parity)

    @pl.when(jnp.logical_not(steady))
    def _():
        slot = lax.rem(i, 2)

        @pl.when(i == 0)
        def _():
            start_rows(lambda r: gather_row(src_ref, r, slot), False)
            ybuf[1] = jnp.zeros((tm * SLAB, LANES), F32)
            n_real = hf_hbm.shape[0]
            spare = [pltpu.make_async_copy(ybuf.at[1], y_hbm.at[pl.ds(n_real + k * tm * SLAB, tm * SLAB)], sem_s.at[1])
                     for k in range(N_GROUPS)]
            for cp in spare:
                cp.start()
            for cp in spare:
                cp.wait()

        @pl.when(i < n_used)
        def _():
            wait_gather(slot)

        @pl.when(i + 1 < n_used)
        def _():
            start_rows(lambda r: gather_row(src_next_ref, r, 1 - slot), False)

        @pl.when((i >= 1) & (i <= n_used))
        def _():
            start_rows(lambda r: scatter_row(r, 1 - slot), False)

        @pl.when(i < n_used)
        def _():
            @pl.when(i >= 2)
            def _():
                wait_scatter(slot)

            @pl.when((i == 0) | (g != gid_ref[jnp.maximum(tile - 1, 0)]))
            def _():
                cast_weights()

            run_experts(slot, g)

        @pl.when(i == n_used)
        def _():
            wait_scatter(1 - slot)

            @pl.when(n_used >= 2)
            def _():
                wait_scatter(slot)


def _moe(layer, gid, n_used, src, dst, hf, wg, wu, wd, wrh, wrl, brt):
    D = D_MODEL
    tm = TM_MOE
    n_tiles = src.shape[0]
    clamp = lambda t: jnp.clip(t, 0, n_tiles - 1)
    idx_spec = lambda shift: pl.BlockSpec(
        (1, 1, tm), lambda i, gid, nu: (clamp(i + shift), 0, 0), memory_space=pltpu.SMEM)
    grp = lambda shape: pl.BlockSpec((None,) + shape, lambda i, gid, nu: (layer, gid[clamp(i)], 0, 0))
    return pl.pallas_call(
        _moe_kernel,
        grid_spec=pltpu.PrefetchScalarGridSpec(
            num_scalar_prefetch=2,
            grid=(n_tiles + 1,),
            in_specs=[idx_spec(0), idx_spec(1), idx_spec(-1),
                      pl.BlockSpec(memory_space=pl.ANY),
                      grp((EXPERTS_PER_GROUP, D, D_EXPERT)), grp((EXPERTS_PER_GROUP, D, D_EXPERT)),
                      grp((EXPERTS_PER_GROUP, D_EXPERT, D)),
                      _layer_spec(layer, (D, LANES)), _layer_spec(layer, (D, LANES)), _layer_spec(layer, (1, LANES))],
            out_specs=pl.BlockSpec(memory_space=pl.ANY),
            scratch_shapes=[pltpu.VMEM((2, tm * SLAB, LANES), F32), pltpu.VMEM((2, tm * SLAB, LANES), F32),
                            pltpu.VMEM((D, 2 * GROUP_HIDDEN), BF16), pltpu.VMEM((GROUP_HIDDEN, D), BF16),
                            pltpu.SemaphoreType.DMA((2,)), pltpu.SemaphoreType.DMA((2,))]),
        out_shape=jax.ShapeDtypeStruct((n_tiles * tm * SLAB, LANES), F32),
        compiler_params=pltpu.CompilerParams(
            dimension_semantics=("arbitrary",), vmem_limit_bytes=VMEM_LIMIT),
        name="moe",
    )(gid, n_used, src, src, dst, hf, wg, wu, wd, wrh, wrl, brt)


def _ple_kernel(x1_ref, y_ref, p_ref, pg_ref, wpg_ref, wp_ref, o_ref):
    x2 = x1_ref[...] + _slabs_to_rows(y_ref, (), x1_ref.shape[0])
    hp = _rms(x2, pg_ref[...]).astype(BF16)
    gate = jax.nn.sigmoid(jnp.dot(hp, wpg_ref[...], preferred_element_type=F32))
    emb = jnp.dot(p_ref[...].astype(BF16), wp_ref[...], preferred_element_type=F32)
    o_ref[...] = x2 + gate * emb


def _ple(layer, x1, y, p, pg, wpg, wp):
    T, D = x1.shape
    tm = TM_PLE
    tok = lambda width: pl.BlockSpec((tm, width), lambda i: (i, 0))
    return pl.pallas_call(
        _ple_kernel,
        grid=(T // tm,),
        in_specs=[tok(D), pl.BlockSpec((tm * SLAB, LANES), lambda i: (i, 0)),
                  pl.BlockSpec((None, tm, PLE_DIM), lambda i: (layer, i, 0)),
                  _layer_spec(layer, (1, D)), _layer_spec(layer, (D, D)), _layer_spec(layer, (PLE_DIM, D))],
        out_specs=tok(D),
        out_shape=jax.ShapeDtypeStruct((T, D), F32),
        compiler_params=pltpu.CompilerParams(
            dimension_semantics=("parallel",), vmem_limit_bytes=VMEM_LIMIT),
        name="ple",
    )(x1, y, p, pg, wpg, wp)


def _routing_tables(info, counts):
    tm = TM_MOE
    T = info.shape[0] * info.shape[2]
    n_rows = T + N_GROUPS * tm
    n_tiles = n_rows // tm
    group = info[:, 0, :].reshape(T).astype(jnp.int32)
    rank = info[:, 1, :].reshape(T).astype(jnp.int32)
    cnt = counts[0, :N_GROUPS].astype(jnp.int32)
    padded = (cnt + tm - 1) // tm * tm
    ends = jnp.cumsum(padded)
    base = ends - padded
    pos = base[group] + rank
    tok_of_row = jnp.full((n_rows,), -1, jnp.int32).at[pos].set(jnp.arange(T, dtype=jnp.int32))
    real = tok_of_row >= 0
    src = jnp.where(real, tok_of_row, 0)
    pad_index = jnp.cumsum(jnp.where(real, 0, 1)) - 1
    dst = jnp.where(real, tok_of_row, T + pad_index)
    starts = jnp.arange(n_tiles, dtype=jnp.int32) * tm
    gid = jnp.minimum(jnp.sum(starts[:, None] >= ends[None, :], axis=1), N_GROUPS - 1).astype(jnp.int32)
    n_used = (ends[N_GROUPS - 1] // tm).astype(jnp.int32)[None]
    gid = jnp.where(starts < ends[N_GROUPS - 1], gid, gid[jnp.maximum(n_used[0] - 1, 0)])
    shape3 = (n_tiles, 1, tm)
    return gid, n_used, src.reshape(shape3), dst.reshape(shape3)


def _rotary_table(positions):
    half = ROT_DIM // 2
    per_row = LANES // half
    inv_freq = ROPE_THETA ** (-jnp.arange(0, ROT_DIM, 2, dtype=F32) / ROT_DIM)
    B, S = positions.shape
    pos = jnp.repeat(positions.astype(F32).reshape(B, S // per_row, per_row), half, axis=-1)
    ang = pos * jnp.tile(inv_freq, per_row)
    c = jnp.cos(ang).reshape(B, S, half)
    s = jnp.sin(ang).reshape(B, S, half)
    return jnp.concatenate([c, s], axis=-1)


def _rotary_placement():
    half = ROT_DIM // 2
    e = np.zeros((2 * half, 3 * LANES), np.float32)
    one = np.zeros((1, LANES), np.float32)
    for lane in range(LANES):
        d = lane % HEAD_DIM
        if d < half:
            e[d, lane] = 1.0
            e[half + d, LANES + lane] = -1.0
        elif d < ROT_DIM:
            e[d - half, lane] = 1.0
            e[d, 2 * LANES + lane] = 1.0
        else:
            one[0, lane] = 1.0
    return jnp.asarray(e, BF16), jnp.asarray(one, F32)


def _pack_w_in(w):
    n_main = C_FZ
    fz = jnp.pad(w[:, :, n_main:n_main + FOX_HEADS], ((0, 0), (0, 0), (0, LANES - FOX_HEADS)))
    return jnp.concatenate([w[:, :, :n_main], fz, w[:, :, n_main + FOX_HEADS:]], axis=2).astype(BF16)


def kernel(x, p, positions, attn_norm_g, w_in, pool_w, pool_scale, diff_qn_g, diff_kn_g, diff_lambda, diff_subln_g, fox_qn_g, fox_kn_g, fox_forget_b, w_branch, w_out, ffn_norm_g, w_route_group, b_route_group, w_route_expert, b_route_expert, moe_w_gate, moe_w_up, moe_w_down, ple_norm_g, w_ple_gate, w_ple):
    B, S, D = x.shape
    T = B * S
    depth = w_in.shape[0]
    two = lambda a: jnp.concatenate([a, a], axis=-1)
    pad_lanes = lambda a: jnp.pad(a, [(0, 0)] * (a.ndim - 1) + [(0, LANES - a.shape[-1])])
    row = lambda a: a[:, None, :]

    tab = _rotary_table(positions)
    rot_e, rot_one = _rotary_placement()
    w_in_p = _pack_w_in(w_in)
    qkg = jnp.stack([two(diff_qn_g), two(diff_kn_g), two(fox_qn_g), two(fox_kn_g)], axis=1)
    fb = row(pad_lanes(fox_forget_b))
    w_r = jnp.concatenate([w_route_group, w_route_expert], axis=2)
    w_r_hi = w_r.astype(BF16)
    wrh = pad_lanes(w_r_hi)
    wrl = pad_lanes((w_r - w_r_hi.astype(F32)).astype(BF16))
    brt = row(pad_lanes(jnp.concatenate([b_route_group, b_route_expert], axis=1)))
    pool_w_b, w_branch_b, w_out_b = pool_w.astype(BF16), w_branch.astype(BF16), w_out.astype(BF16)
    w_ple_gate_b, w_ple_b = w_ple_gate.astype(BF16), w_ple.astype(BF16)
    p_flat = p.reshape(depth, T, PLE_DIM)

    for l in range(depth):
        lam_init = 0.8 - 0.6 * math.exp(-0.3 * l)
        u, dq, dk, dv, fq, fk, fv, f, ft, gates = _inproj(
            l, x, row(attn_norm_g), w_in_p, tab, rot_e, rot_one, qkg, fb)
        y_diff = _diff_attention(l, dq, dk, dv, diff_lambda, row(diff_subln_g), lam_init)
        y_fox = _fox_attention(fq, fk, fv, f, ft)
        x1, hf, info, counts = _merge(
            l, x, u, y_diff, y_fox, gates, pool_w_b, row(pool_scale), w_branch_b, w_out_b,
            row(ffn_norm_g), wrh, wrl, brt)
        gid, n_used, src, dst = _routing_tables(info, counts)
        y = _moe(l, gid, n_used, src, dst, hf.reshape(T * SLAB, LANES), moe_w_gate, moe_w_up, moe_w_down,
                 wrh, wrl, brt)
        x = _ple(l, x1.reshape(T, D), y, p_flat, row(ple_norm_g), w_ple_gate_b, w_ple_b).reshape(B, S, D)
    return x
```

```python
import functools
import math

import jax
import jax.numpy as jnp
import numpy as np
from jax import lax
from jax.experimental import pallas as pl
from jax.experimental.pallas import tpu as pltpu

F32 = jnp.float32
BF16 = jnp.bfloat16

D_MODEL = 1024
HEAD_DIM = 64
LANES = 128
POOL_WINDOWS = (2, 4, 8, 16)
POOL_HALO = 16
BRANCH_WIDTH = 512
N_BRANCH = 3
FOX_HEADS = 8
ROT_DIM = HEAD_DIM // 4
ROPE_THETA = 500000.0
PLE_DIM = 256
N_GROUPS = 4
EXPERTS_PER_GROUP = 4
N_EXPERTS = 16
D_EXPERT = 256
RMS_EPS = 1e-6
NEG = -1e30
LOG2E = math.log2(math.e)

C_POOL, C_DQ, C_DK, C_DV, C_FQ, C_FK, C_FV, C_FZ, C_GZ = (0, 512, 1024, 1536, 2048, 2560, 3072, 3584, 3712)
IN_PACKED = C_GZ + N_BRANCH * D_MODEL

TM_IN = 512
TM_MERGE = 512
TM_MOE = 512
TM_PLE = 1024
GROUP_HIDDEN = EXPERTS_PER_GROUP * D_EXPERT
TQ = 256
VMEM_LIMIT = 56 * 1024 * 1024


def _const_spec(shape):
    zeros = (0,) * len(shape)
    return pl.BlockSpec(shape, lambda *_: zeros, pipeline_mode=pl.Buffered(1))


def _layer_spec(layer, shape):
    zeros = (0,) * len(shape)
    return pl.BlockSpec((None,) + tuple(shape), lambda *_: (layer,) + zeros, pipeline_mode=pl.Buffered(1))


def _rms(x, g):
    return x * lax.rsqrt(jnp.mean(x * x, axis=-1, keepdims=True) + RMS_EPS) * g


SLAB = D_MODEL // LANES


def _rows_to_slabs(dst, lead, x, row0=0):
    n = x.shape[0]
    for c in range(SLAB):
        dst[lead + (pl.ds(row0 * SLAB + c, n, stride=SLAB), slice(None))] = x[:, c * LANES:(c + 1) * LANES]


def _slabs_to_rows(src, lead, n):
    return jnp.concatenate([src[lead + (pl.ds(c, n, stride=SLAB), slice(None))] for c in range(SLAB)], axis=1)


def _router_logits(hf, wrh_ref, wrl_ref, br_ref):
    hi = hf.astype(BF16)
    lo = (hf - hi.astype(F32)).astype(BF16)
    return (jnp.dot(hi, wrh_ref[...], preferred_element_type=F32)
            + jnp.dot(lo, wrh_ref[...], preferred_element_type=F32)
            + jnp.dot(hi, wrl_ref[...], preferred_element_type=F32)
            + br_ref[...])


def _inproj_kernel(x_ref, g_ref, wa_ref, wb_ref, tab_ref, rot_e_ref, rot_one_ref, qkg_ref, fb_ref,
                   u_ref, dq_ref, dk_ref, dv_ref, fq_ref, fk_ref, fv_ref, f_ref, ft_ref, gates_ref,
                   carry_ref, wg_scr):
    i = pl.program_id(1)
    tm = x_ref.shape[1]

    @pl.when(i == 0)
    def _():
        carry_ref[...] = jnp.zeros_like(carry_ref)

    @pl.when((pl.program_id(0) == 0) & (i == 0))
    def _():
        wg_scr[...] = wb_ref[:, FOX_HEADS:FOX_HEADS + N_BRANCH * D_MODEL]

    h = _rms(x_ref[0], g_ref[...]).astype(BF16)

    def proj(c0, width):
        if c0 < C_FZ:
            w = wa_ref[:, c0:c0 + width]
        elif c0 == C_FZ:
            w = wb_ref[:, 0:width]
        else:
            w = wg_scr[:, c0 - C_GZ:c0 - C_GZ + width]
        return jnp.dot(h, w, preferred_element_type=F32)

    lo = lax.broadcasted_iota(jnp.int32, (tm, LANES), 1) < HEAD_DIM
    tab = tab_ref[0]
    tab_hi = tab.astype(BF16)
    tab_lo = (tab - tab_hi.astype(F32)).astype(BF16)
    placed = (jnp.dot(tab_hi, rot_e_ref[...], preferred_element_type=F32)
              + jnp.dot(tab_lo, rot_e_ref[...], preferred_element_type=F32))
    cos = placed[:, 0:LANES] + rot_one_ref[...]
    sa = placed[:, LANES:2 * LANES]
    sb = placed[:, 2 * LANES:3 * LANES]
    scale = HEAD_DIM ** -0.5 * LOG2E

    def head_norm(a, g):
        sq = a * a
        s_lo = jnp.sum(jnp.where(lo, sq, 0.0), axis=-1, keepdims=True)
        s_hi = jnp.sum(jnp.where(lo, 0.0, sq), axis=-1, keepdims=True)
        ss = jnp.where(lo, s_lo, s_hi)
        return a * lax.rsqrt(ss * (1.0 / HEAD_DIM) + RMS_EPS) * g

    def rotary(a):
        return a * cos + pltpu.roll(a, LANES - ROT_DIM // 2, 1) * sa + pltpu.roll(a, ROT_DIM // 2, 1) * sb

    def plain(out_ref):
        def epilogue(acc):
            out_ref[0] = acc.astype(BF16)
        return epilogue

    def qk(out_ref, g, rot, mult):
        def epilogue(acc):
            for j in range(4):
                a = head_norm(acc[:, j * LANES:(j + 1) * LANES], g)
                if rot:
                    a = rotary(a)
                if mult != 1.0:
                    a = a * mult
                out_ref[0, :, j * LANES:(j + 1) * LANES] = a.astype(BF16)
        return epilogue

    def forget(acc):
        z = acc + fb_ref[...]
        logf = jnp.minimum(z, 0.0) - jnp.log1p(jnp.exp(-jnp.abs(z)))
        row = lax.broadcasted_iota(jnp.int32, (tm, tm), 0)
        col = lax.broadcasted_iota(jnp.int32, (tm, tm), 1)
        tri = jnp.where(row >= col, 1.0, 0.0).astype(BF16)
        hi = logf.astype(BF16)
        lo_part = (logf - hi.astype(F32)).astype(BF16)
        fcum = (jnp.dot(tri, hi, preferred_element_type=F32)
                + jnp.dot(tri, lo_part, preferred_element_type=F32)
                + carry_ref[0:1, :])
        carry_ref[...] = jnp.broadcast_to(fcum[tm - 1:tm, :], carry_ref.shape)
        f_ref[0] = fcum
        ft_ref[0] = fcum.T[0:FOX_HEADS, :]

    def gate(c):
        def epilogue(acc):
            gates_ref[0, :, c * 512:(c + 1) * 512] = jax.nn.sigmoid(acc).astype(BF16)
        return epilogue

    chunks = [(C_POOL, 512, plain(u_ref)),
              (C_DQ, 512, qk(dq_ref, qkg_ref[0:1, :], True, scale)),
              (C_DK, 512, qk(dk_ref, qkg_ref[1:2, :], True, 1.0)),
              (C_DV, 512, plain(dv_ref)),
              (C_FQ, 512, qk(fq_ref, qkg_ref[2:3, :], False, scale)),
              (C_FK, 512, qk(fk_ref, qkg_ref[3:4, :], False, 1.0)),
              (C_FV, 512, plain(fv_ref)),
              (C_FZ, LANES, forget)]
    chunks += [(C_GZ + c * 512, 512, gate(c)) for c in range(N_BRANCH * D_MODEL // 512)]
    acc = proj(chunks[0][0], chunks[0][1])
    for k, (_, _, epilogue) in enumerate(chunks):
        nxt = proj(chunks[k + 1][0], chunks[k + 1][1]) if k + 1 < len(chunks) else None
        epilogue(acc)
        acc = nxt


def _inproj(layer, x, g, wa, wb, tab, rot_e, rot_one, qkg, fb):
    B, S, D = x.shape
    tm = TM_IN
    tok = lambda width: pl.BlockSpec((1, tm, width), lambda b, i: (b, i, 0))
    out_shape = (
        [jax.ShapeDtypeStruct((B, S, 512), BF16)] * 7
        + [jax.ShapeDtypeStruct((B, S, LANES), F32),
           jax.ShapeDtypeStruct((B, FOX_HEADS, S), F32),
           jax.ShapeDtypeStruct((B, S, N_BRANCH * D), BF16)]
    )
    out_specs = (
        [tok(512)] * 7
        + [tok(LANES),
           pl.BlockSpec((1, FOX_HEADS, tm), lambda b, i: (b, 0, i)),
           tok(N_BRANCH * D)]
    )
    return pl.pallas_call(
        _inproj_kernel,
        grid=(B, S // tm),
        in_specs=[tok(D), _layer_spec(layer, (1, D)),
                  _layer_spec(layer, (D, C_FZ)), _layer_spec(layer, (D, FOX_HEADS + N_BRANCH * D)),
                  tok(2 * (ROT_DIM // 2)), _const_spec((2 * (ROT_DIM // 2), 3 * LANES)), _const_spec((1, LANES)),
                  _layer_spec(layer, (4, LANES)), _layer_spec(layer, (1, LANES))],
        out_specs=out_specs,
        out_shape=out_shape,
        scratch_shapes=[pltpu.VMEM((8, LANES), F32), pltpu.VMEM((D, N_BRANCH * D), BF16)],
        compiler_params=pltpu.CompilerParams(
            dimension_semantics=("arbitrary", "arbitrary"), vmem_limit_bytes=VMEM_LIMIT),
        name="inproj",
    )(x, g, wa, wb, tab, rot_e, rot_one, qkg, fb)


def _nt_dot(a, b):
    return lax.dot_general(a, b, (((1,), (1,)), ((), ())), preferred_element_type=F32)


def _score_rows(score_chunk, n_chunks, s_scr):
    tq = TQ
    mt = jnp.full((tq, LANES), NEG, F32)
    for c in range(n_chunks):
        s = score_chunk(c)
        s_scr[:, c * tq:(c + 1) * tq] = s
        for w in range(tq // LANES):
            mt = jnp.maximum(mt, s[:, w * LANES:(w + 1) * LANES])
    return jnp.max(mt, axis=1, keepdims=True)


def _exp_rows(n_chunks, m, s_scr, p_scr):
    tq = TQ
    for c in range(n_chunks):
        p_scr[:, c * tq:(c + 1) * tq] = jnp.exp2(s_scr[:, c * tq:(c + 1) * tq] - m).astype(BF16)


def _diff_kernel(lam_init, q_ref, k_ref, v_ref, lam_ref, g_ref, o_ref, s_scr, p_scr, v2_scr):
    S = q_ref.shape[1]
    tq = TQ
    lo = lax.broadcasted_iota(jnp.int32, (tq, LANES), 1) < HEAD_DIM
    causal = (lax.broadcasted_iota(jnp.int32, (tq, tq), 1) <= lax.broadcasted_iota(jnp.int32, (tq, tq), 0))
    lp = lam_ref[...]
    lam = (jnp.exp(jnp.sum(lp[0:1, :] * lp[1:2, :], axis=1, keepdims=True))
           - jnp.exp(jnp.sum(lp[2:3, :] * lp[3:4, :], axis=1, keepdims=True)) + lam_init)
    v2_scr[:, 0:LANES] = v_ref[0]
    v2_scr[:, LANES:2 * LANES] = jnp.ones((S, LANES), BF16)

    def scores(qi, comp):
        q = q_ref[0, qi * tq:(qi + 1) * tq, :]
        zero = jnp.zeros_like(q)
        qm = jnp.where(lo, q, zero) if comp == 0 else jnp.where(lo, zero, q)

        def score_chunk(c):
            s = _nt_dot(qm, k_ref[0, c * tq:(c + 1) * tq, :])
            return jnp.where(causal, s, NEG) if c == qi else s

        return _score_rows(score_chunk, qi + 1, s_scr.at[2 * (qi % 2) + comp])

    n_q = S // tq
    row_max = {(0, comp): scores(0, comp) for comp in range(2)}
    for qi in range(n_q):
        kend = (qi + 1) * tq
        parts = []
        for comp in range(2):
            _exp_rows(qi + 1, row_max[(qi, comp)], s_scr.at[2 * (qi % 2) + comp], p_scr.at[comp])
            if qi + 1 < n_q:
                row_max[(qi + 1, comp)] = scores(qi + 1, comp)
            pv = jnp.dot(p_scr[comp, :, 0:kend], v2_scr[0:kend, :], preferred_element_type=F32)
            parts.append(pv[:, 0:LANES] / pv[:, LANES:2 * LANES])
        o = parts[0] - lam * parts[1]
        o = _rms(o, g_ref[...]) * (1.0 - lam_init)
        o_ref[0, qi * tq:kend, :] = o.astype(BF16)


def _diff_attention(layer, dq, dk, dv, lam_p, subln_g, lam_init):
    B, S, _ = dq.shape
    blk = pl.BlockSpec((1, S, LANES), lambda b, h: (b, 0, h))
    return pl.pallas_call(
        functools.partial(_diff_kernel, lam_init),
        grid=(B, 4),
        in_specs=[blk, blk, blk, _layer_spec(layer, (4, HEAD_DIM)), _layer_spec(layer, (1, LANES))],
        out_specs=blk,
        out_shape=jax.ShapeDtypeStruct((B, S, 512), BF16),
        scratch_shapes=[pltpu.VMEM((4, TQ, S), F32), pltpu.VMEM((2, TQ, S), BF16),
                        pltpu.VMEM((S, 2 * LANES), BF16)],
        compiler_params=pltpu.CompilerParams(
            dimension_semantics=("parallel", "parallel"), vmem_limit_bytes=VMEM_LIMIT),
        name="diff_attn",
    )(dq, dk, dv, lam_p, subln_g)


def _fox_kernel(q_ref, k_ref, v_ref, f_ref, ft_ref, o_ref, s_scr, p_scr, v2_scr):
    S = q_ref.shape[1]
    tq = TQ
    j = pl.program_id(1)
    lane = lax.broadcasted_iota(jnp.int32, (tq, LANES), 1)
    lo = lane < HEAD_DIM
    causal = (lax.broadcasted_iota(jnp.int32, (tq, tq), 1) <= lax.broadcasted_iota(jnp.int32, (tq, tq), 0))
    v = v_ref[0]
    lo_s = lax.broadcasted_iota(jnp.int32, (S, LANES), 1) < HEAD_DIM
    one = jnp.ones_like(v)
    v2_scr[0] = jnp.where(lo_s, v, one)
    v2_scr[1] = jnp.where(lo_s, one, v)

    def scores(qi, hh):
        q = q_ref[0, qi * tq:(qi + 1) * tq, :]
        zero = jnp.zeros_like(q)
        qm = jnp.where(lo, q, zero) if hh == 0 else jnp.where(lo, zero, q)
        f_tile = f_ref[0, qi * tq:(qi + 1) * tq, :] * LOG2E
        f_t = jnp.sum(jnp.where(lane == 2 * j + hh, f_tile, 0.0), axis=1, keepdims=True)

        def score_chunk(c):
            f_s = ft_ref[0, pl.ds(2 * j + hh, 1), c * tq:(c + 1) * tq] * LOG2E
            s = _nt_dot(qm, k_ref[0, c * tq:(c + 1) * tq, :]) + (f_t - f_s)
            return jnp.where(causal, s, NEG) if c == qi else s

        return _score_rows(score_chunk, qi + 1, s_scr.at[2 * (qi % 2) + hh])

    n_q = S // tq
    row_max = {(0, hh): scores(0, hh) for hh in range(2)}
    for qi in range(n_q):
        kend = (qi + 1) * tq
        pvs = []
        for hh in range(2):
            _exp_rows(qi + 1, row_max[(qi, hh)], s_scr.at[2 * (qi % 2) + hh], p_scr.at[hh])
            if qi + 1 < n_q:
                row_max[(qi + 1, hh)] = scores(qi + 1, hh)
            pvs.append(jnp.dot(p_scr[hh, :, 0:kend], v2_scr[hh, 0:kend, :], preferred_element_type=F32))
        o = jnp.where(lo, pvs[0] / pltpu.roll(pvs[0], HEAD_DIM, 1), pvs[1] / pltpu.roll(pvs[1], HEAD_DIM, 1))
        o_ref[0, qi * tq:kend, :] = o.astype(BF16)


def _fox_attention(fq, fk, fv, f, ft):
    B, S, _ = fq.shape
    blk = pl.BlockSpec((1, S, LANES), lambda b, j: (b, 0, j))
    return pl.pallas_call(
        _fox_kernel,
        grid=(B, 4),
        in_specs=[blk, blk, blk,
                  pl.BlockSpec((1, S, LANES), lambda b, j: (b, 0, 0)),
                  pl.BlockSpec((1, FOX_HEADS, S), lambda b, j: (b, 0, 0))],
        out_specs=blk,
        out_shape=jax.ShapeDtypeStruct((B, S, 512), BF16),
        scratch_shapes=[pltpu.VMEM((4, TQ, S), F32), pltpu.VMEM((2, TQ, S), BF16),
                        pltpu.VMEM((2, S, LANES), BF16)],
        compiler_params=pltpu.CompilerParams(
            dimension_semantics=("parallel", "parallel"), vmem_limit_bytes=VMEM_LIMIT),
        name="fox_attn",
    )(fq, fk, fv, f, ft)


def _merge_kernel(x_ref, u_ref, uh_ref, yd_ref, yf_ref, gates_ref, pw_ref, ps_ref, wb_ref, wo_ref,
                  ng_ref, wrh_ref, wrl_ref, br_ref,
                  x1_ref, hf_ref, info_ref, count_ref):
    i = pl.program_id(1)
    tm = x_ref.shape[1]

    @pl.when((pl.program_id(0) == 0) & (i == 0))
    def _():
        count_ref[...] = jnp.zeros_like(count_ref)

    hm = tm // 2
    half = D_MODEL // 2
    first_halo = uh_ref[0].astype(F32) * jnp.where(i > 0, 1.0, 0.0)

    def mix(h):
        r0 = h * hm
        halo = first_halo if h == 0 else u_ref[0, r0 - POOL_HALO:r0, :].astype(F32)
        ext = jnp.concatenate([halo, u_ref[0, r0:r0 + hm, :].astype(F32)], axis=0)
        t = i * tm + r0 + lax.broadcasted_iota(jnp.int32, (hm, 1), 0)
        ys, gated = [], {}
        for g, w in enumerate(POOL_WINDOWS):
            n, c0 = 1 + g // 2, (g % 2) * half
            y_att = (yd_ref if n == 1 else yf_ref)[0, r0:r0 + hm, :]
            gated[(n, g % 2)] = (gates_ref[0, r0:r0 + hm, n * D_MODEL + c0:n * D_MODEL + c0 + half].astype(F32)
                                 * jnp.dot(y_att, wb_ref[n, :, c0:c0 + half], preferred_element_type=F32))
            e = ext[:, g * LANES:(g + 1) * LANES]
            win = e
            step = 1
            while step < w:
                win = win + pltpu.roll(win, step, 0)
                step *= 2
            cnt = jnp.minimum(t + 1, w).astype(F32)
            d = win[POOL_HALO:, :] / cnt - e[POOL_HALO:, :]
            ys.append(jnp.dot(d.astype(BF16), pw_ref[g], preferred_element_type=F32))
        return jnp.concatenate(ys, axis=1) * ps_ref[...], gated

    def project(h, y_pool, gated):
        r0 = h * hm
        br_pool = jnp.dot(y_pool.astype(BF16), wb_ref[0], preferred_element_type=F32)
        att = [jnp.concatenate([gated[(n, 0)], gated[(n, 1)]], axis=1) for n in (1, 2)]
        merged = gates_ref[0, r0:r0 + hm, 0:D_MODEL].astype(F32) * br_pool + att[0] + att[1]
        x1 = x_ref[0, r0:r0 + hm, :] + jnp.dot(merged.astype(BF16), wo_ref[...], preferred_element_type=F32)
        x1_ref[0, r0:r0 + hm, :] = x1
        return x1

    def route(h, x1):
        hf = _rms(x1, ng_ref[...])
        _rows_to_slabs(hf_ref, (0,), hf, h * hm)
        return _router_logits(hf, wrh_ref, wrl_ref, br_ref)

    mixed = [mix(h) for h in range(2)]
    x1s = [project(h, *mixed[h]) for h in range(2)]
    lg = jnp.concatenate([route(h, x1s[h]) for h in range(2)], axis=0)

    lane = lax.broadcasted_iota(jnp.int32, (tm, LANES), 1)
    lane_f = lane.astype(F32)
    gmask = lane < N_GROUPS
    gl = jnp.where(gmask, lg, NEG)
    gmax = jnp.max(gl, axis=1, keepdims=True)
    gi = jnp.min(jnp.where(gmask & (gl == gmax), lane_f, float(LANES)), axis=1, keepdims=True)

    onehot = jnp.where(lane_f == gi, 1.0, 0.0)
    row = lax.broadcasted_iota(jnp.int32, (tm, tm), 0)
    col = lax.broadcasted_iota(jnp.int32, (tm, tm), 1)
    tri = jnp.where(row >= col, 1.0, 0.0).astype(BF16)
    incl = jnp.dot(tri, onehot.astype(BF16), preferred_element_type=F32)
    before = count_ref[0:1, :]
    rank = jnp.sum(onehot * (incl - onehot + before), axis=1, keepdims=True)
    count_ref[...] = jnp.broadcast_to(before + incl[tm - 1:tm, :], count_ref.shape)
    info_ref[0] = jnp.where(lane == 0, gi, jnp.where(lane == 1, rank, 0.0)).T[0:8, :]


def _merge(layer, x, u, yd, yf, gates, pw, ps, wb, wo, ng, wrh, wrl, brt):
    B, S, D = x.shape
    tm = TM_MERGE
    tok = lambda width: pl.BlockSpec((1, tm, width), lambda b, i: (b, i, 0))
    halo_blocks = tm // POOL_HALO
    halo = pl.BlockSpec((1, POOL_HALO, 512), lambda b, i: (b, jnp.maximum(i * halo_blocks - 1, 0), 0))
    return pl.pallas_call(
        _merge_kernel,
        grid=(B, S // tm),
        in_specs=[tok(D), tok(512), halo, tok(512), tok(512), tok(N_BRANCH * D),
                  _layer_spec(layer, (4, LANES, LANES)), _layer_spec(layer, (1, 512)),
                  _layer_spec(layer, (N_BRANCH, BRANCH_WIDTH, D)), _layer_spec(layer, (D, D)),
                  _layer_spec(layer, (1, D)), _layer_spec(layer, (D, LANES)), _layer_spec(layer, (D, LANES)),
                  _layer_spec(layer, (1, LANES))],
        out_specs=[tok(D), pl.BlockSpec((1, tm * SLAB, LANES), lambda b, i: (b, i, 0)),
                   pl.BlockSpec((1, 8, tm), lambda b, i: (b, 0, i)),
                   pl.BlockSpec((8, LANES), lambda b, i: (0, 0))],
        out_shape=[jax.ShapeDtypeStruct((B, S, D), F32),
                   jax.ShapeDtypeStruct((B, S * SLAB, LANES), F32),
                   jax.ShapeDtypeStruct((B, 8, S), F32),
                   jax.ShapeDtypeStruct((8, LANES), F32)],
        compiler_params=pltpu.CompilerParams(
            dimension_semantics=("arbitrary", "arbitrary"), vmem_limit_bytes=VMEM_LIMIT),
        name="merge",
    )(x, u, u, yd, yf, gates, pw, ps, wb, wo, ng, wrh, wrl, brt)


def _moe_kernel(gid_ref, nused_ref, src_ref, src_next_ref, dst_prev_ref,
                hf_hbm, wg_ref, wu_ref, wd_ref, wrh_ref, wrl_ref, br_ref,
                y_hbm,
                xbuf, ybuf, wgu_bf, wd_bf, sem_g, sem_s):
    i = pl.program_id(0)
    n_used = nused_ref[0]
    tm = TM_MOE
    n_tiles = pl.num_programs(0) - 1

    def slab(row):
        start = row * SLAB
        return pl.ds(start if isinstance(row, int) else pl.multiple_of(start, SLAB), SLAB)

    def gather_row(idx_ref, r, to_slot):
        return pltpu.make_async_copy(hf_hbm.at[slab(idx_ref[0, 0, r])], xbuf.at[to_slot, slab(r)],
                                     sem_g.at[to_slot])

    def scatter_row(r, from_slot):
        return pltpu.make_async_copy(ybuf.at[from_slot, slab(r)], y_hbm.at[slab(dst_prev_ref[0, 0, r])],
                                     sem_s.at[from_slot])

    def start_rows(make_copy, unrolled):
        if unrolled:
            for r in range(tm):
                make_copy(r).start()
        else:
            def body(r, c):
                make_copy(r).start()
                return c
            lax.fori_loop(0, tm, body, 0, unroll=8)

    def wait_gather(of_slot):
        pltpu.make_async_copy(hf_hbm.at[pl.ds(0, tm * SLAB)], xbuf.at[of_slot], sem_g.at[of_slot]).wait()

    def wait_scatter(of_slot):
        pltpu.make_async_copy(ybuf.at[of_slot], y_hbm.at[pl.ds(0, tm * SLAB)], sem_s.at[of_slot]).wait()

    def cast_weights():
        for e in range(EXPERTS_PER_GROUP):
            wgu_bf[:, e * D_EXPERT:(e + 1) * D_EXPERT] = wg_ref[e].astype(BF16)
            wgu_bf[:, GROUP_HIDDEN + e * D_EXPERT:GROUP_HIDDEN + (e + 1) * D_EXPERT] = wu_ref[e].astype(BF16)
            wd_bf[e * D_EXPERT:(e + 1) * D_EXPERT, :] = wd_ref[e].astype(BF16)

    def run_experts(slot, g):
        hf = _slabs_to_rows(xbuf, (slot,), tm)

        lg = _router_logits(hf, wrh_ref, wrl_ref, br_ref)
        lane = lax.broadcasted_iota(jnp.int32, (tm, LANES), 1)
        lane_f = lane.astype(F32)
        far = float(LANES)
        rmax = lambda a: jnp.max(a, axis=1, keepdims=True)
        rmin = lambda a: jnp.min(a, axis=1, keepdims=True)
        rsum = lambda a: jnp.sum(a, axis=1, keepdims=True)
        lg_g = rsum(jnp.where(lane == g, lg, 0.0))
        gp = 1.0 / rsum(jnp.where(lane < N_GROUPS, jnp.exp(lg - lg_g), 0.0))
        e_lo = N_GROUPS + EXPERTS_PER_GROUP * g
        emask = (lane >= e_lo) & (lane < e_lo + EXPERTS_PER_GROUP)
        el = jnp.where(emask, lg, NEG)
        m1 = rmax(el)
        i1 = rmin(jnp.where(emask & (el == m1), lane_f, far))
        rest = emask & (lane_f != i1)
        el2 = jnp.where(rest, lg, NEG)
        m2 = rmax(el2)
        i2 = rmin(jnp.where(rest & (el2 == m2), lane_f, far))
        r = jnp.exp(m2 - m1)
        comb = jnp.where(lane_f == i1, gp / (1.0 + r), jnp.where(lane_f == i2, gp * r / (1.0 + r), 0.0))
        cw = jnp.concatenate(
            [jnp.broadcast_to(rsum(jnp.where(lane == e_lo + e, comb, 0.0)), (tm, D_EXPERT))
             for e in range(EXPERTS_PER_GROUP)], axis=1)

        gu = jnp.dot(hf.astype(BF16), wgu_bf[...], preferred_element_type=F32)
        gate, up = gu[:, 0:GROUP_HIDDEN], gu[:, GROUP_HIDDEN:2 * GROUP_HIDDEN]
        act = gate * jax.nn.sigmoid(gate) * up * cw
        _rows_to_slabs(ybuf, (slot,), jnp.dot(act.astype(BF16), wd_bf[...], preferred_element_type=F32))

    tile = jnp.minimum(i, n_tiles - 1)
    g = gid_ref[tile]
    steady = (i >= 2) & (i + 1 < n_used)

    def steady_step(slot):
        wait_gather(slot)
        wait_scatter(slot)

        @pl.when(g != gid_ref[tile - 1])
        def _():
            cast_weights()

        start_rows(lambda r: gather_row(src_next_ref, r, 1 - slot), True)
        start_rows(lambda r: scatter_row(r, 1 - slot), True)
        run_experts(slot, g)

    for parity in range(2):
        @pl.when(steady & (lax.rem(i, 2) == parity))
        def _(parity=parity):
            steady_step(parity)

    @pl.when(jnp.logical_not(steady))
    def _():
        slot = lax.rem(i, 2)

        @pl.when(i == 0)
        def _():
            start_rows(lambda r: gather_row(src_ref, r, slot), False)
            ybuf[1] = jnp.zeros((tm * SLAB, LANES), F32)
            n_real = hf_hbm.shape[0]
            spare = [pltpu.make_async_copy(ybuf.at[1], y_hbm.at[pl.ds(n_real + k * tm * SLAB, tm * SLAB)], sem_s.at[1])
                     for k in range(N_GROUPS)]
            for cp in spare:
                cp.start()
            for cp in spare:
                cp.wait()

        @pl.when(i < n_used)
        def _():
            wait_gather(slot)

        @pl.when(i + 1 < n_used)
        def _():
            start_rows(lambda r: gather_row(src_next_ref, r, 1 - slot), False)

        @pl.when((i >= 1) & (i <= n_used))
        def _():
            start_rows(lambda r: scatter_row(r, 1 - slot), False)

        @pl.when(i < n_used)
        def _():
            @pl.when(i >= 2)
            def _():
                wait_scatter(slot)

            @pl.when((i == 0) | (g != gid_ref[jnp.maximum(tile - 1, 0)]))
            def _():
                cast_weights()

            run_experts(slot, g)

        @pl.when(i == n_used)
        def _():
            wait_scatter(1 - slot)

            @pl.when(n_used >= 2)
            def _():
                wait_scatter(slot)


def _moe(layer, gid, n_used, src, dst, hf, wg, wu, wd, wrh, wrl, brt):
    D = D_MODEL
    tm = TM_MOE
    n_tiles = src.shape[0]
    clamp = lambda t: jnp.clip(t, 0, n_tiles - 1)
    idx_spec = lambda shift: pl.BlockSpec(
        (1, 1, tm), lambda i, gid, nu: (clamp(i + shift), 0, 0), memory_space=pltpu.SMEM)
    grp = lambda shape: pl.BlockSpec((None,) + shape, lambda i, gid, nu: (layer, gid[clamp(i)], 0, 0))
    return pl.pallas_call(
        _moe_kernel,
        grid_spec=pltpu.PrefetchScalarGridSpec(
            num_scalar_prefetch=2,
            grid=(n_tiles + 1,),
            in_specs=[idx_spec(0), idx_spec(1), idx_spec(-1),
                      pl.BlockSpec(memory_space=pl.ANY),
                      grp((EXPERTS_PER_GROUP, D, D_EXPERT)), grp((EXPERTS_PER_GROUP, D, D_EXPERT)),
                      grp((EXPERTS_PER_GROUP, D_EXPERT, D)),
                      _layer_spec(layer, (D, LANES)), _layer_spec(layer, (D, LANES)), _layer_spec(layer, (1, LANES))],
            out_specs=pl.BlockSpec(memory_space=pl.ANY),
            scratch_shapes=[pltpu.VMEM((2, tm * SLAB, LANES), F32), pltpu.VMEM((2, tm * SLAB, LANES), F32),
                            pltpu.VMEM((D, 2 * GROUP_HIDDEN), BF16), pltpu.VMEM((GROUP_HIDDEN, D), BF16),
                            pltpu.SemaphoreType.DMA((2,)), pltpu.SemaphoreType.DMA((2,))]),
        out_shape=jax.ShapeDtypeStruct((n_tiles * tm * SLAB, LANES), F32),
        compiler_params=pltpu.CompilerParams(
            dimension_semantics=("arbitrary",), vmem_limit_bytes=VMEM_LIMIT),
        name="moe",
    )(gid, n_used, src, src, dst, hf, wg, wu, wd, wrh, wrl, brt)


def _ple_kernel(x1_ref, y_ref, p_ref, pg_ref, wpg_ref, wp_ref, o_ref):
    x2 = x1_ref[...] + _slabs_to_rows(y_ref, (), x1_ref.shape[0])
    hp = _rms(x2, pg_ref[...]).astype(BF16)
    gate = jax.nn.sigmoid(jnp.dot(hp, wpg_ref[...], preferred_element_type=F32))
    emb = jnp.dot(p_ref[...].astype(BF16), wp_ref[...], preferred_element_type=F32)
    o_ref[...] = x2 + gate * emb


def _ple(layer, x1, y, p, pg, wpg, wp):
    T, D = x1.shape
    tm = TM_PLE
    tok = lambda width: pl.BlockSpec((tm, width), lambda i: (i, 0))
    return pl.pallas_call(
        _ple_kernel,
        grid=(T // tm,),
        in_specs=[tok(D), pl.BlockSpec((tm * SLAB, LANES), lambda i: (i, 0)),
                  pl.BlockSpec((None, tm, PLE_DIM), lambda i: (layer, i, 0)),
                  _layer_spec(layer, (1, D)), _layer_spec(layer, (D, D)), _layer_spec(layer, (PLE_DIM, D))],
        out_specs=tok(D),
        out_shape=jax.ShapeDtypeStruct((T, D), F32),
        compiler_params=pltpu.CompilerParams(
            dimension_semantics=("parallel",), vmem_limit_bytes=VMEM_LIMIT),
        name="ple",
    )(x1, y, p, pg, wpg, wp)


def _routing_tables(info, counts):
    tm = TM_MOE
    T = info.shape[0] * info.shape[2]
    n_rows = T + N_GROUPS * tm
    n_tiles = n_rows // tm
    group = info[:, 0, :].reshape(T).astype(jnp.int32)
    rank = info[:, 1, :].reshape(T).astype(jnp.int32)
    cnt = counts[0, :N_GROUPS].astype(jnp.int32)
    padded = (cnt + tm - 1) // tm * tm
    ends = jnp.cumsum(padded)
    base = ends - padded
    pos = base[group] + rank
    tok_of_row = jnp.full((n_rows,), -1, jnp.int32).at[pos].set(jnp.arange(T, dtype=jnp.int32))
    real = tok_of_row >= 0
    src = jnp.where(real, tok_of_row, 0)
    pad_index = jnp.cumsum(jnp.where(real, 0, 1)) - 1
    dst = jnp.where(real, tok_of_row, T + pad_index)
    starts = jnp.arange(n_tiles, dtype=jnp.int32) * tm
    gid = jnp.minimum(jnp.sum(starts[:, None] >= ends[None, :], axis=1), N_GROUPS - 1).astype(jnp.int32)
    n_used = (ends[N_GROUPS - 1] // tm).astype(jnp.int32)[None]
    gid = jnp.where(starts < ends[N_GROUPS - 1], gid, gid[jnp.maximum(n_used[0] - 1, 0)])
    shape3 = (n_tiles, 1, tm)
    return gid, n_used, src.reshape(shape3), dst.reshape(shape3)


def _rotary_table(positions):
    half = ROT_DIM // 2
    per_row = LANES // half
    inv_freq = ROPE_THETA ** (-jnp.arange(0, ROT_DIM, 2, dtype=F32) / ROT_DIM)
    B, S = positions.shape
    pos = jnp.repeat(positions.astype(F32).reshape(B, S // per_row, per_row), half, axis=-1)
    ang = pos * jnp.tile(inv_freq, per_row)
    c = jnp.cos(ang).reshape(B, S, half)
    s = jnp.sin(ang).reshape(B, S, half)
    return jnp.concatenate([c, s], axis=-1)


def _rotary_placement():
    half = ROT_DIM // 2
    e = np.zeros((2 * half, 3 * LANES), np.float32)
    one = np.zeros((1, LANES), np.float32)
    for lane in range(LANES):
        d = lane % HEAD_DIM
        if d < half:
            e[d, lane] = 1.0
            e[half + d, LANES + lane] = -1.0
        elif d < ROT_DIM:
            e[d - half, lane] = 1.0
            e[d, 2 * LANES + lane] = 1.0
        else:
            one[0, lane] = 1.0
    return jnp.asarray(e, BF16), jnp.asarray(one, F32)


def kernel(x, p, positions, attn_norm_g, w_in, pool_w, pool_scale, diff_qn_g, diff_kn_g, diff_lambda, diff_subln_g, fox_qn_g, fox_kn_g, fox_forget_b, w_branch, w_out, ffn_norm_g, w_route_group, b_route_group, w_route_expert, b_route_expert, moe_w_gate, moe_w_up, moe_w_down, ple_norm_g, w_ple_gate, w_ple):
    B, S, D = x.shape
    T = B * S
    depth = w_in.shape[0]
    two = lambda a: jnp.concatenate([a, a], axis=-1)
    pad_lanes = lambda a: jnp.pad(a, [(0, 0)] * (a.ndim - 1) + [(0, LANES - a.shape[-1])])
    row = lambda a: a[:, None, :]

    tab = _rotary_table(positions)
    rot_e, rot_one = _rotary_placement()
    w_in_a = w_in[:, :, :C_FZ].astype(BF16)
    w_in_b = w_in[:, :, C_FZ:].astype(BF16)
    qkg = jnp.stack([two(diff_qn_g), two(diff_kn_g), two(fox_qn_g), two(fox_kn_g)], axis=1)
    fb = row(pad_lanes(fox_forget_b))
    w_r = jnp.concatenate([w_route_group, w_route_expert], axis=2)
    w_r_hi = w_r.astype(BF16)
    wrh = pad_lanes(w_r_hi)
    wrl = pad_lanes((w_r - w_r_hi.astype(F32)).astype(BF16))
    brt = row(pad_lanes(jnp.concatenate([b_route_group, b_route_expert], axis=1)))
    pool_w_b, w_branch_b, w_out_b = pool_w.astype(BF16), w_branch.astype(BF16), w_out.astype(BF16)
    w_ple_gate_b, w_ple_b = w_ple_gate.astype(BF16), w_ple.astype(BF16)
    p_flat = p.reshape(depth, T, PLE_DIM)

    for l in range(depth):
        lam_init = 0.8 - 0.6 * math.exp(-0.3 * l)
        u, dq, dk, dv, fq, fk, fv, f, ft, gates = _inproj(
            l, x, row(attn_norm_g), w_in_a, w_in_b, tab, rot_e, rot_one, qkg, fb)
        y_diff = _diff_attention(l, dq, dk, dv, diff_lambda, row(diff_subln_g), lam_init)
        y_fox = _fox_attention(fq, fk, fv, f, ft)
        x1, hf, info, counts = _merge(
            l, x, u, y_diff, y_fox, gates, pool_w_b, row(pool_scale), w_branch_b, w_out_b,
            row(ffn_norm_g), wrh, wrl, brt)
        gid, n_used, src, dst = _routing_tables(info, counts)
        y = _moe(l, gid, n_used, src, dst, hf.reshape(T * SLAB, LANES), moe_w_gate, moe_w_up, moe_w_down,
                 wrh, wrl, brt)
        x = _ple(l, x1.reshape(T, D), y, p_flat, row(ple_norm_g), w_ple_gate_b, w_ple_b).reshape(B, S, D)
    return x
```

```python
import functools
import math

import jax
import jax.numpy as jnp
import numpy as np
from jax import lax
from jax.experimental import pallas as pl
from jax.experimental.pallas import tpu as pltpu

F32 = jnp.float32
BF16 = jnp.bfloat16

D_MODEL = 1024
HEAD_DIM = 64
LANES = 128
POOL_WINDOWS = (2, 4, 8, 16)
POOL_HALO = 16
BRANCH_WIDTH = 512
N_BRANCH = 3
FOX_HEADS = 8
ROT_DIM = HEAD_DIM // 4
ROPE_THETA = 500000.0
PLE_DIM = 256
N_GROUPS = 4
EXPERTS_PER_GROUP = 4
N_EXPERTS = 16
D_EXPERT = 256
RMS_EPS = 1e-6
NEG = -1e30
LOG2E = math.log2(math.e)

C_POOL, C_DQ, C_DK, C_DV, C_FQ, C_FK, C_FV, C_FZ, C_GZ = (0, 512, 1024, 1536, 2048, 2560, 3072, 3584, 3712)
IN_PACKED = C_GZ + N_BRANCH * D_MODEL

TM_IN = 512
TM_MERGE = 512
TM_MOE = 512
TM_PLE = 1024
GROUP_HIDDEN = EXPERTS_PER_GROUP * D_EXPERT
TQ = 256
VMEM_LIMIT = 56 * 1024 * 1024


def _const_spec(shape):
    zeros = (0,) * len(shape)
    return pl.BlockSpec(shape, lambda *_: zeros, pipeline_mode=pl.Buffered(1))


def _layer_spec(layer, shape):
    zeros = (0,) * len(shape)
    return pl.BlockSpec((None,) + tuple(shape), lambda *_: (layer,) + zeros, pipeline_mode=pl.Buffered(1))


def _rms(x, g):
    return x * lax.rsqrt(jnp.mean(x * x, axis=-1, keepdims=True) + RMS_EPS) * g


SLAB = D_MODEL // LANES


def _rows_to_slabs(dst, lead, x, row0=0):
    n = x.shape[0]
    for c in range(SLAB):
        dst[lead + (pl.ds(row0 * SLAB + c, n, stride=SLAB), slice(None))] = x[:, c * LANES:(c + 1) * LANES]


def _slabs_to_rows(src, lead, n):
    return jnp.concatenate([src[lead + (pl.ds(c, n, stride=SLAB), slice(None))] for c in range(SLAB)], axis=1)


def _router_logits(hf, wrh_ref, wrl_ref, br_ref):
    hi = hf.astype(BF16)
    lo = (hf - hi.astype(F32)).astype(BF16)
    w_both = jnp.concatenate([wrh_ref[...], wrl_ref[...]], axis=1)
    hi_both = jnp.dot(hi, w_both, preferred_element_type=F32)
    return (hi_both[:, 0:LANES]
            + jnp.dot(lo, wrh_ref[...], preferred_element_type=F32)
            + hi_both[:, LANES:2 * LANES]
            + br_ref[...])


def _inproj_kernel(x_ref, g_ref, wa_ref, wb_ref, tab_ref, rot_e_ref, rot_one_ref, qkg_ref, fb_ref,
                   u_ref, dq_ref, dk_ref, dv_ref, fq_ref, fk_ref, fv_ref, f_ref, ft_ref, gates_ref,
                   carry_ref, wg_scr):
    i = pl.program_id(1)
    tm = x_ref.shape[1]

    @pl.when(i == 0)
    def _():
        carry_ref[...] = jnp.zeros_like(carry_ref)

    @pl.when((pl.program_id(0) == 0) & (i == 0))
    def _():
        wg_scr[...] = wb_ref[:, FOX_HEADS:FOX_HEADS + N_BRANCH * D_MODEL]

    h = _rms(x_ref[0], g_ref[...]).astype(BF16)

    def proj(c0, width):
        if c0 < C_FZ:
            w = wa_ref[:, c0:c0 + width]
        elif c0 == C_FZ:
            w = wb_ref[:, 0:width]
        else:
            w = wg_scr[:, c0 - C_GZ:c0 - C_GZ + width]
        return jnp.dot(h, w, preferred_element_type=F32)

    lo = lax.broadcasted_iota(jnp.int32, (tm, LANES), 1) < HEAD_DIM
    tab = tab_ref[0]
    tab_hi = tab.astype(BF16)
    tab_lo = (tab - tab_hi.astype(F32)).astype(BF16)
    placed = (jnp.dot(tab_hi, rot_e_ref[...], preferred_element_type=F32)
              + jnp.dot(tab_lo, rot_e_ref[...], preferred_element_type=F32))
    cos = placed[:, 0:LANES] + rot_one_ref[...]
    sa = placed[:, LANES:2 * LANES]
    sb = placed[:, 2 * LANES:3 * LANES]
    scale = HEAD_DIM ** -0.5 * LOG2E

    def head_norm(a, g):
        sq = a * a
        s_lo = jnp.sum(jnp.where(lo, sq, 0.0), axis=-1, keepdims=True)
        s_hi = jnp.sum(jnp.where(lo, 0.0, sq), axis=-1, keepdims=True)
        ss = jnp.where(lo, s_lo, s_hi)
        return a * lax.rsqrt(ss * (1.0 / HEAD_DIM) + RMS_EPS) * g

    def rotary(a):
        return a * cos + pltpu.roll(a, LANES - ROT_DIM // 2, 1) * sa + pltpu.roll(a, ROT_DIM // 2, 1) * sb

    def plain(out_ref):
        def epilogue(acc):
            out_ref[0] = acc.astype(BF16)
        return epilogue

    def qk(out_ref, g, rot, mult):
        def epilogue(acc):
            for j in range(4):
                a = head_norm(acc[:, j * LANES:(j + 1) * LANES], g)
                if rot:
                    a = rotary(a)
                if mult != 1.0:
                    a = a * mult
                out_ref[0, :, j * LANES:(j + 1) * LANES] = a.astype(BF16)
        return epilogue

    def forget(acc):
        z = acc + fb_ref[...]
        logf = jnp.minimum(z, 0.0) - jnp.log1p(jnp.exp(-jnp.abs(z)))
        row = lax.broadcasted_iota(jnp.int32, (tm, tm), 0)
        col = lax.broadcasted_iota(jnp.int32, (tm, tm), 1)
        tri = jnp.where(row >= col, 1.0, 0.0).astype(BF16)
        hi = logf.astype(BF16)
        lo_part = (logf - hi.astype(F32)).astype(BF16)
        both = jnp.dot(tri, jnp.concatenate([hi, lo_part], axis=1), preferred_element_type=F32)
        fcum = both[:, 0:LANES] + both[:, LANES:2 * LANES] + carry_ref[0:1, :]
        carry_ref[...] = jnp.broadcast_to(fcum[tm - 1:tm, :], carry_ref.shape)
        f_ref[0] = fcum
        ft_ref[0] = fcum.T[0:FOX_HEADS, :]

    def gate(c):
        def epilogue(acc):
            gates_ref[0, :, c * 512:(c + 1) * 512] = jax.nn.sigmoid(acc).astype(BF16)
        return epilogue

    chunks = [(C_POOL, 512, plain(u_ref)),
              (C_DQ, 512, qk(dq_ref, qkg_ref[0:1, :], True, scale)),
              (C_DK, 512, qk(dk_ref, qkg_ref[1:2, :], True, 1.0)),
              (C_DV, 512, plain(dv_ref)),
              (C_FQ, 512, qk(fq_ref, qkg_ref[2:3, :], False, scale)),
              (C_FK, 512, qk(fk_ref, qkg_ref[3:4, :], False, 1.0)),
              (C_FV, 512, plain(fv_ref)),
              (C_FZ, LANES, forget)]
    chunks += [(C_GZ + c * 512, 512, gate(c)) for c in range(N_BRANCH * D_MODEL // 512)]
    acc = proj(chunks[0][0], chunks[0][1])
    for k, (_, _, epilogue) in enumerate(chunks):
        nxt = proj(chunks[k + 1][0], chunks[k + 1][1]) if k + 1 < len(chunks) else None
        epilogue(acc)
        acc = nxt


def _inproj(layer, x, g, w, tab, rot_e, rot_one, qkg, fb):
    B, S, D = x.shape
    tm = TM_IN
    tok = lambda width: pl.BlockSpec((1, tm, width), lambda b, i: (b, i, 0))
    out_shape = (
        [jax.ShapeDtypeStruct((B, S, 512), BF16)] * 7
        + [jax.ShapeDtypeStruct((B, S, LANES), F32),
           jax.ShapeDtypeStruct((B, FOX_HEADS, S), F32),
           jax.ShapeDtypeStruct((B, S, N_BRANCH * D), BF16)]
    )
    out_specs = (
        [tok(512)] * 7
        + [tok(LANES),
           pl.BlockSpec((1, FOX_HEADS, tm), lambda b, i: (b, 0, i)),
           tok(N_BRANCH * D)]
    )
    return pl.pallas_call(
        _inproj_kernel,
        grid=(B, S // tm),
        in_specs=[tok(D), _layer_spec(layer, (1, D)),
                  _layer_spec(layer, (D, C_FZ)),
                  pl.BlockSpec((None, D, C_FZ), lambda *_: (layer, 0, 1), pipeline_mode=pl.Buffered(1)),
                  tok(2 * (ROT_DIM // 2)), _const_spec((2 * (ROT_DIM // 2), 3 * LANES)), _const_spec((1, LANES)),
                  _layer_spec(layer, (4, LANES)), _layer_spec(layer, (1, LANES))],
        out_specs=out_specs,
        out_shape=out_shape,
        scratch_shapes=[pltpu.VMEM((8, LANES), F32), pltpu.VMEM((D, N_BRANCH * D), BF16)],
        compiler_params=pltpu.CompilerParams(
            dimension_semantics=("arbitrary", "arbitrary"), vmem_limit_bytes=VMEM_LIMIT),
        name="inproj",
    )(x, g, w, w, tab, rot_e, rot_one, qkg, fb)


def _nt_dot(a, b):
    return lax.dot_general(a, b, (((1,), (1,)), ((), ())), preferred_element_type=F32)


def _score_rows(score_chunk, n_chunks, s_scr):
    tq = TQ
    mt = jnp.full((tq, LANES), NEG, F32)
    for c in range(n_chunks):
        s = score_chunk(c)
        s_scr[:, c * tq:(c + 1) * tq] = s
        for w in range(tq // LANES):
            mt = jnp.maximum(mt, s[:, w * LANES:(w + 1) * LANES])
    return jnp.max(mt, axis=1, keepdims=True)


def _exp_rows(n_chunks, m, s_scr, p_scr):
    tq = TQ
    for c in range(n_chunks):
        p_scr[:, c * tq:(c + 1) * tq] = jnp.exp2(s_scr[:, c * tq:(c + 1) * tq] - m).astype(BF16)


def _diff_kernel(lam_init, q_ref, k_ref, v_ref, lam_ref, g_ref, o_ref, s_scr, p_scr, v2_scr):
    S = q_ref.shape[1]
    tq = TQ
    lo = lax.broadcasted_iota(jnp.int32, (tq, LANES), 1) < HEAD_DIM
    causal = (lax.broadcasted_iota(jnp.int32, (tq, tq), 1) <= lax.broadcasted_iota(jnp.int32, (tq, tq), 0))
    lp = lam_ref[...]
    lam = (jnp.exp(jnp.sum(lp[0:1, :] * lp[1:2, :], axis=1, keepdims=True))
           - jnp.exp(jnp.sum(lp[2:3, :] * lp[3:4, :], axis=1, keepdims=True)) + lam_init)
    v2_scr[:, 0:LANES] = v_ref[0]
    v2_scr[:, LANES:2 * LANES] = jnp.ones((S, LANES), BF16)

    def scores(qi, comp):
        q = q_ref[0, qi * tq:(qi + 1) * tq, :]
        zero = jnp.zeros_like(q)
        qm = jnp.where(lo, q, zero) if comp == 0 else jnp.where(lo, zero, q)

        def score_chunk(c):
            s = _nt_dot(qm, k_ref[0, c * tq:(c + 1) * tq, :])
            return jnp.where(causal, s, NEG) if c == qi else s

        return _score_rows(score_chunk, qi + 1, s_scr.at[2 * (qi % 2) + comp])

    n_q = S // tq
    row_max = {(0, comp): scores(0, comp) for comp in range(2)}
    for qi in range(n_q):
        kend = (qi + 1) * tq
        parts = []
        for comp in range(2):
            _exp_rows(qi + 1, row_max[(qi, comp)], s_scr.at[2 * (qi % 2) + comp], p_scr.at[comp])
            if qi + 1 < n_q:
                row_max[(qi + 1, comp)] = scores(qi + 1, comp)
            pv = jnp.dot(p_scr[comp, :, 0:kend], v2_scr[0:kend, :], preferred_element_type=F32)
            parts.append(pv[:, 0:LANES] / pv[:, LANES:2 * LANES])
        o = parts[0] - lam * parts[1]
        o = _rms(o, g_ref[...]) * (1.0 - lam_init)
        o_ref[0, qi * tq:kend, :] = o.astype(BF16)


def _diff_attention(layer, dq, dk, dv, lam_p, subln_g, lam_init):
    B, S, _ = dq.shape
    blk = pl.BlockSpec((1, S, LANES), lambda b, h: (b, 0, h))
    return pl.pallas_call(
        functools.partial(_diff_kernel, lam_init),
        grid=(B, 4),
        in_specs=[blk, blk, blk, _layer_spec(layer, (4, HEAD_DIM)), _layer_spec(layer, (1, LANES))],
        out_specs=blk,
        out_shape=jax.ShapeDtypeStruct((B, S, 512), BF16),
        scratch_shapes=[pltpu.VMEM((4, TQ, S), F32), pltpu.VMEM((2, TQ, S), BF16),
                        pltpu.VMEM((S, 2 * LANES), BF16)],
        compiler_params=pltpu.CompilerParams(
            dimension_semantics=("parallel", "parallel"), vmem_limit_bytes=VMEM_LIMIT),
        name="diff_attn",
    )(dq, dk, dv, lam_p, subln_g)


def _fox_kernel(q_ref, k_ref, v_ref, f_ref, ft_ref, o_ref, s_scr, p_scr, v2_scr):
    S = q_ref.shape[1]
    tq = TQ
    j = pl.program_id(1)
    lane = lax.broadcasted_iota(jnp.int32, (tq, LANES), 1)
    lo = lane < HEAD_DIM
    causal = (lax.broadcasted_iota(jnp.int32, (tq, tq), 1) <= lax.broadcasted_iota(jnp.int32, (tq, tq), 0))
    v = v_ref[0]
    lo_s = lax.broadcasted_iota(jnp.int32, (S, LANES), 1) < HEAD_DIM
    one = jnp.ones_like(v)
    v2_scr[0] = jnp.where(lo_s, v, one)
    v2_scr[1] = jnp.where(lo_s, one, v)

    def scores(qi, hh):
        q = q_ref[0, qi * tq:(qi + 1) * tq, :]
        zero = jnp.zeros_like(q)
        qm = jnp.where(lo, q, zero) if hh == 0 else jnp.where(lo, zero, q)
        f_tile = f_ref[0, qi * tq:(qi + 1) * tq, :] * LOG2E
        f_t = jnp.sum(jnp.where(lane == 2 * j + hh, f_tile, 0.0), axis=1, keepdims=True)

        def score_chunk(c):
            f_s = ft_ref[0, pl.ds(2 * j + hh, 1), c * tq:(c + 1) * tq] * LOG2E
            s = _nt_dot(qm, k_ref[0, c * tq:(c + 1) * tq, :]) + (f_t - f_s)
            return jnp.where(causal, s, NEG) if c == qi else s

        return _score_rows(score_chunk, qi + 1, s_scr.at[2 * (qi % 2) + hh])

    n_q = S // tq
    row_max = {(0, hh): scores(0, hh) for hh in range(2)}
    for qi in range(n_q):
        kend = (qi + 1) * tq
        pvs = []
        for hh in range(2):
            _exp_rows(qi + 1, row_max[(qi, hh)], s_scr.at[2 * (qi % 2) + hh], p_scr.at[hh])
            if qi + 1 < n_q:
                row_max[(qi + 1, hh)] = scores(qi + 1, hh)
            pvs.append(jnp.dot(p_scr[hh, :, 0:kend], v2_scr[hh, 0:kend, :], preferred_element_type=F32))
        o = jnp.where(lo, pvs[0] / pltpu.roll(pvs[0], HEAD_DIM, 1), pvs[1] / pltpu.roll(pvs[1], HEAD_DIM, 1))
        o_ref[0, qi * tq:kend, :] = o.astype(BF16)


def _fox_attention(fq, fk, fv, f, ft):
    B, S, _ = fq.shape
    blk = pl.BlockSpec((1, S, LANES), lambda b, j: (b, 0, j))
    return pl.pallas_call(
        _fox_kernel,
        grid=(B, 4),
        in_specs=[blk, blk, blk,
                  pl.BlockSpec((1, S, LANES), lambda b, j: (b, 0, 0)),
                  pl.BlockSpec((1, FOX_HEADS, S), lambda b, j: (b, 0, 0))],
        out_specs=blk,
        out_shape=jax.ShapeDtypeStruct((B, S, 512), BF16),
        scratch_shapes=[pltpu.VMEM((4, TQ, S), F32), pltpu.VMEM((2, TQ, S), BF16),
                        pltpu.VMEM((2, S, LANES), BF16)],
        compiler_params=pltpu.CompilerParams(
            dimension_semantics=("parallel", "parallel"), vmem_limit_bytes=VMEM_LIMIT),
        name="fox_attn",
    )(fq, fk, fv, f, ft)


def _merge_kernel(x_ref, u_ref, uh_ref, yd_ref, yf_ref, gates_ref, pw_ref, ps_ref, wb_ref, wo_ref,
                  ng_ref, wrh_ref, wrl_ref, br_ref,
                  x1_ref, hf_ref, info_ref, count_ref):
    i = pl.program_id(1)
    tm = x_ref.shape[1]

    @pl.when((pl.program_id(0) == 0) & (i == 0))
    def _():
        count_ref[...] = jnp.zeros_like(count_ref)

    hm = tm // 2
    half = D_MODEL // 2
    first_halo = uh_ref[0].astype(F32) * jnp.where(i > 0, 1.0, 0.0)

    def mix(h):
        r0 = h * hm
        halo = first_halo if h == 0 else u_ref[0, r0 - POOL_HALO:r0, :].astype(F32)
        ext = jnp.concatenate([halo, u_ref[0, r0:r0 + hm, :].astype(F32)], axis=0)
        t = i * tm + r0 + lax.broadcasted_iota(jnp.int32, (hm, 1), 0)
        ys, gated = [], {}
        for g, w in enumerate(POOL_WINDOWS):
            n, c0 = 1 + g // 2, (g % 2) * half
            y_att = (yd_ref if n == 1 else yf_ref)[0, r0:r0 + hm, :]
            gated[(n, g % 2)] = (gates_ref[0, r0:r0 + hm, n * D_MODEL + c0:n * D_MODEL + c0 + half].astype(F32)
                                 * jnp.dot(y_att, wb_ref[n, :, c0:c0 + half], preferred_element_type=F32))
            e = ext[:, g * LANES:(g + 1) * LANES]
            win = e
            step = 1
            while step < w:
                win = win + pltpu.roll(win, step, 0)
                step *= 2
            cnt = jnp.minimum(t + 1, w).astype(F32)
            d = win[POOL_HALO:, :] / cnt - e[POOL_HALO:, :]
            ys.append(jnp.dot(d.astype(BF16), pw_ref[g], preferred_element_type=F32))
        return jnp.concatenate(ys, axis=1) * ps_ref[...], gated

    def project(h, y_pool, gated):
        r0 = h * hm
        br_pool = jnp.dot(y_pool.astype(BF16), wb_ref[0], preferred_element_type=F32)
        att = [jnp.concatenate([gated[(n, 0)], gated[(n, 1)]], axis=1) for n in (1, 2)]
        merged = gates_ref[0, r0:r0 + hm, 0:D_MODEL].astype(F32) * br_pool + att[0] + att[1]
        x1 = x_ref[0, r0:r0 + hm, :] + jnp.dot(merged.astype(BF16), wo_ref[...], preferred_element_type=F32)
        x1_ref[0, r0:r0 + hm, :] = x1
        return x1

    def route(h, x1):
        hf = _rms(x1, ng_ref[...])
        _rows_to_slabs(hf_ref, (0,), hf, h * hm)
        return _router_logits(hf, wrh_ref, wrl_ref, br_ref)

    mixed = [mix(h) for h in range(2)]
    x1s = [project(h, *mixed[h]) for h in range(2)]
    lg = jnp.concatenate([route(h, x1s[h]) for h in range(2)], axis=0)

    lane = lax.broadcasted_iota(jnp.int32, (tm, LANES), 1)
    lane_f = lane.astype(F32)
    gmask = lane < N_GROUPS
    gl = jnp.where(gmask, lg, NEG)
    gmax = jnp.max(gl, axis=1, keepdims=True)
    gi = jnp.min(jnp.where(gmask & (gl == gmax), lane_f, float(LANES)), axis=1, keepdims=True)

    onehot = jnp.where(lane_f == gi, 1.0, 0.0)
    row = lax.broadcasted_iota(jnp.int32, (tm, tm), 0)
    col = lax.broadcasted_iota(jnp.int32, (tm, tm), 1)
    tri = jnp.where(row >= col, 1.0, 0.0).astype(BF16)
    incl = jnp.dot(tri, onehot.astype(BF16), preferred_element_type=F32)
    before = count_ref[0:1, :]
    rank = jnp.sum(onehot * (incl - onehot + before), axis=1, keepdims=True)
    count_ref[...] = jnp.broadcast_to(before + incl[tm - 1:tm, :], count_ref.shape)
    info_ref[0] = jnp.where(lane == 0, gi, jnp.where(lane == 1, rank, 0.0)).T[0:8, :]


def _merge(layer, x, u, yd, yf, gates, pw, ps, wb, wo, ng, wrh, wrl, brt):
    B, S, D = x.shape
    tm = TM_MERGE
    tok = lambda width: pl.BlockSpec((1, tm, width), lambda b, i: (b, i, 0))
    halo_blocks = tm // POOL_HALO
    halo = pl.BlockSpec((1, POOL_HALO, 512), lambda b, i: (b, jnp.maximum(i * halo_blocks - 1, 0), 0))
    return pl.pallas_call(
        _merge_kernel,
        grid=(B, S // tm),
        in_specs=[tok(D), tok(512), halo, tok(512), tok(512), tok(N_BRANCH * D),
                  _layer_spec(layer, (4, LANES, LANES)), _layer_spec(layer, (1, 512)),
                  _layer_spec(layer, (N_BRANCH, BRANCH_WIDTH, D)), _layer_spec(layer, (D, D)),
                  _layer_spec(layer, (1, D)), _layer_spec(layer, (D, LANES)), _layer_spec(layer, (D, LANES)),
                  _layer_spec(layer, (1, LANES))],
        out_specs=[tok(D), pl.BlockSpec((1, tm * SLAB, LANES), lambda b, i: (b, i, 0)),
                   pl.BlockSpec((1, 8, tm), lambda b, i: (b, 0, i)),
                   pl.BlockSpec((8, LANES), lambda b, i: (0, 0))],
        out_shape=[jax.ShapeDtypeStruct((B, S, D), F32),
                   jax.ShapeDtypeStruct((B, S * SLAB, LANES), F32),
                   jax.ShapeDtypeStruct((B, 8, S), F32),
                   jax.ShapeDtypeStruct((8, LANES), F32)],
        compiler_params=pltpu.CompilerParams(
            dimension_semantics=("arbitrary", "arbitrary"), vmem_limit_bytes=VMEM_LIMIT),
        name="merge",
    )(x, u, u, yd, yf, gates, pw, ps, wb, wo, ng, wrh, wrl, brt)


def _moe_kernel(gid_ref, nused_ref, src_ref, src_next_ref, dst_prev_ref,
                hf_hbm, wg_ref, wu_ref, wd_ref, wrh_ref, wrl_ref, br_ref,
                y_hbm,
                xbuf, ybuf, wgu_bf, wd_bf, sem_g, sem_s):
    i = pl.program_id(0)
    n_used = nused_ref[0]
    tm = TM_MOE
    n_tiles = pl.num_programs(0) - 1

    def slab(row):
        start = row * SLAB
        return pl.ds(start if isinstance(row, int) else pl.multiple_of(start, SLAB), SLAB)

    def gather_row(idx_ref, r, to_slot):
        return pltpu.make_async_copy(hf_hbm.at[slab(idx_ref[0, 0, r])], xbuf.at[to_slot, slab(r)],
                                     sem_g.at[to_slot])

    def scatter_row(r, from_slot):
        return pltpu.make_async_copy(ybuf.at[from_slot, slab(r)], y_hbm.at[slab(dst_prev_ref[0, 0, r])],
                                     sem_s.at[from_slot])

    def start_rows(make_copy, unrolled):
        if unrolled:
            for r in range(tm):
                make_copy(r).start()
        else:
            def body(r, c):
                make_copy(r).start()
                return c
            lax.fori_loop(0, tm, body, 0, unroll=8)

    def wait_gather(of_slot):
        pltpu.make_async_copy(hf_hbm.at[pl.ds(0, tm * SLAB)], xbuf.at[of_slot], sem_g.at[of_slot]).wait()

    def wait_scatter(of_slot):
        pltpu.make_async_copy(ybuf.at[of_slot], y_hbm.at[pl.ds(0, tm * SLAB)], sem_s.at[of_slot]).wait()

    def cast_weights():
        for e in range(EXPERTS_PER_GROUP):
            wgu_bf[:, e * D_EXPERT:(e + 1) * D_EXPERT] = wg_ref[e].astype(BF16)
            wgu_bf[:, GROUP_HIDDEN + e * D_EXPERT:GROUP_HIDDEN + (e + 1) * D_EXPERT] = wu_ref[e].astype(BF16)
            wd_bf[e * D_EXPERT:(e + 1) * D_EXPERT, :] = wd_ref[e].astype(BF16)

    def run_experts(slot, g):
        hf = _slabs_to_rows(xbuf, (slot,), tm)

        lg = _router_logits(hf, wrh_ref, wrl_ref, br_ref)
        lane = lax.broadcasted_iota(jnp.int32, (tm, LANES), 1)
        lane_f = lane.astype(F32)
        far = float(LANES)
        rmax = lambda a: jnp.max(a, axis=1, keepdims=True)
        rmin = lambda a: jnp.min(a, axis=1, keepdims=True)
        rsum = lambda a: jnp.sum(a, axis=1, keepdims=True)
        lg_g = rsum(jnp.where(lane == g, lg, 0.0))
        gp = 1.0 / rsum(jnp.where(lane < N_GROUPS, jnp.exp(lg - lg_g), 0.0))
        e_lo = N_GROUPS + EXPERTS_PER_GROUP * g
        emask = (lane >= e_lo) & (lane < e_lo + EXPERTS_PER_GROUP)
        el = jnp.where(emask, lg, NEG)
        m1 = rmax(el)
        i1 = rmin(jnp.where(emask & (el == m1), lane_f, far))
        rest = emask & (lane_f != i1)
        el2 = jnp.where(rest, lg, NEG)
        m2 = rmax(el2)
        i2 = rmin(jnp.where(rest & (el2 == m2), lane_f, far))
        r = jnp.exp(m2 - m1)
        comb = jnp.where(lane_f == i1, gp / (1.0 + r), jnp.where(lane_f == i2, gp * r / (1.0 + r), 0.0))
        cw = jnp.concatenate(
            [jnp.broadcast_to(rsum(jnp.where(lane == e_lo + e, comb, 0.0)), (tm, D_EXPERT))
             for e in range(EXPERTS_PER_GROUP)], axis=1)

        gu = jnp.dot(hf.astype(BF16), wgu_bf[...], preferred_element_type=F32)
        gate, up = gu[:, 0:GROUP_HIDDEN], gu[:, GROUP_HIDDEN:2 * GROUP_HIDDEN]
        act = gate * jax.nn.sigmoid(gate) * up * cw
        _rows_to_slabs(ybuf, (slot,), jnp.dot(act.astype(BF16), wd_bf[...], preferred_element_type=F32))

    tile = jnp.minimum(i, n_tiles - 1)
    g = gid_ref[tile]
    steady = (i >= 2) & (i + 1 < n_used)

    def steady_step(slot):
        wait_gather(slot)
        wait_scatter(slot)

        @pl.when(g != gid_ref[tile - 1])
        def _():
            cast_weights()

        start_rows(lambda r: gather_row(src_next_ref, r, 1 - slot), True)
        start_rows(lambda r: scatter_row(r, 1 - slot), True)
        run_experts(slot, g)

    for parity in range(2):
        @pl.when(steady & (lax.rem(i, 2) == parity))
        def _(parity=parity):
            steady_step(parity)

    @pl.when(jnp.logical_not(steady))
    def _():
        slot = lax.rem(i, 2)

        @pl.when(i == 0)
        def _():
            start_rows(lambda r: gather_row(src_ref, r, slot), False)
            ybuf[1] = jnp.zeros((tm * SLAB, LANES), F32)
            n_real = hf_hbm.shape[0]
            spare = [pltpu.make_async_copy(ybuf.at[1], y_hbm.at[pl.ds(n_real + k * tm * SLAB, tm * SLAB)], sem_s.at[1])
                     for k in range(N_GROUPS)]
            for cp in spare:
                cp.start()
            for cp in spare:
                cp.wait()

        @pl.when(i < n_used)
        def _():
            wait_gather(slot)

        @pl.when(i + 1 < n_used)
        def _():
            start_rows(lambda r: gather_row(src_next_ref, r, 1 - slot), False)

        @pl.when((i >= 1) & (i <= n_used))
        def _():
            start_rows(lambda r: scatter_row(r, 1 - slot), False)

        @pl.when(i < n_used)
        def _():
            @pl.when(i >= 2)
            def _():
                wait_scatter(slot)

            @pl.when((i == 0) | (g != gid_ref[jnp.maximum(tile - 1, 0)]))
            def _():
                cast_weights()

            run_experts(slot, g)

        @pl.when(i == n_used)
        def _():
            wait_scatter(1 - slot)

            @pl.when(n_used >= 2)
            def _():
                wait_scatter(slot)


def _moe(layer, gid, n_used, src, dst, hf, wg, wu, wd, wrh, wrl, brt):
    D = D_MODEL
    tm = TM_MOE
    n_tiles = src.shape[0]
    clamp = lambda t: jnp.clip(t, 0, n_tiles - 1)
    idx_spec = lambda shift: pl.BlockSpec(
        (1, 1, tm), lambda i, gid, nu: (clamp(i + shift), 0, 0), memory_space=pltpu.SMEM)
    grp = lambda shape: pl.BlockSpec((None,) + shape, lambda i, gid, nu: (layer, gid[clamp(i)], 0, 0))
    return pl.pallas_call(
        _moe_kernel,
        grid_spec=pltpu.PrefetchScalarGridSpec(
            num_scalar_prefetch=2,
            grid=(n_tiles + 1,),
            in_specs=[idx_spec(0), idx_spec(1), idx_spec(-1),
                      pl.BlockSpec(memory_space=pl.ANY),
                      grp((EXPERTS_PER_GROUP, D, D_EXPERT)), grp((EXPERTS_PER_GROUP, D, D_EXPERT)),
                      grp((EXPERTS_PER_GROUP, D_EXPERT, D)),
                      _layer_spec(layer, (D, LANES)), _layer_spec(layer, (D, LANES)), _layer_spec(layer, (1, LANES))],
            out_specs=pl.BlockSpec(memory_space=pl.ANY),
            scratch_shapes=[pltpu.VMEM((2, tm * SLAB, LANES), F32), pltpu.VMEM((2, tm * SLAB, LANES), F32),
                            pltpu.VMEM((D, 2 * GROUP_HIDDEN), BF16), pltpu.VMEM((GROUP_HIDDEN, D), BF16),
                            pltpu.SemaphoreType.DMA((2,)), pltpu.SemaphoreType.DMA((2,))]),
        out_shape=jax.ShapeDtypeStruct((n_tiles * tm * SLAB, LANES), F32),
        compiler_params=pltpu.CompilerParams(
            dimension_semantics=("arbitrary",), vmem_limit_bytes=VMEM_LIMIT),
        name="moe",
    )(gid, n_used, src, src, dst, hf, wg, wu, wd, wrh, wrl, brt)


def _ple_kernel(x1_ref, y_ref, p_ref, pg_ref, wpg_ref, wp_ref, o_ref):
    x2 = x1_ref[...] + _slabs_to_rows(y_ref, (), x1_ref.shape[0])
    hp = _rms(x2, pg_ref[...]).astype(BF16)
    gate = jax.nn.sigmoid(jnp.dot(hp, wpg_ref[...], preferred_element_type=F32))
    emb = jnp.dot(p_ref[...].astype(BF16), wp_ref[...], preferred_element_type=F32)
    o_ref[...] = x2 + gate * emb


def _ple(layer, x1, y, p, pg, wpg, wp):
    T, D = x1.shape
    tm = TM_PLE
    tok = lambda width: pl.BlockSpec((tm, width), lambda i: (i, 0))
    return pl.pallas_call(
        _ple_kernel,
        grid=(T // tm,),
        in_specs=[tok(D), pl.BlockSpec((tm * SLAB, LANES), lambda i: (i, 0)),
                  pl.BlockSpec((None, tm, PLE_DIM), lambda i: (layer, i, 0)),
                  _layer_spec(layer, (1, D)), _layer_spec(layer, (D, D)), _layer_spec(layer, (PLE_DIM, D))],
        out_specs=tok(D),
        out_shape=jax.ShapeDtypeStruct((T, D), F32),
        compiler_params=pltpu.CompilerParams(
            dimension_semantics=("parallel",), vmem_limit_bytes=VMEM_LIMIT),
        name="ple",
    )(x1, y, p, pg, wpg, wp)


def _routing_tables(info, counts):
    tm = TM_MOE
    T = info.shape[0] * info.shape[2]
    n_rows = T + N_GROUPS * tm
    n_tiles = n_rows // tm
    group = info[:, 0, :].reshape(T).astype(jnp.int32)
    rank = info[:, 1, :].reshape(T).astype(jnp.int32)
    cnt = counts[0, :N_GROUPS].astype(jnp.int32)
    padded = (cnt + tm - 1) // tm * tm
    ends = jnp.cumsum(padded)
    base = ends - padded
    pos = base[group] + rank
    tok_of_row = jnp.full((n_rows,), -1, jnp.int32).at[pos].set(jnp.arange(T, dtype=jnp.int32))
    real = tok_of_row >= 0
    src = jnp.where(real, tok_of_row, 0)
    pad_index = jnp.cumsum(jnp.where(real, 0, 1)) - 1
    dst = jnp.where(real, tok_of_row, T + pad_index)
    starts = jnp.arange(n_tiles, dtype=jnp.int32) * tm
    gid = jnp.minimum(jnp.sum(starts[:, None] >= ends[None, :], axis=1), N_GROUPS - 1).astype(jnp.int32)
    n_used = (ends[N_GROUPS - 1] // tm).astype(jnp.int32)[None]
    gid = jnp.where(starts < ends[N_GROUPS - 1], gid, gid[jnp.maximum(n_used[0] - 1, 0)])
    shape3 = (n_tiles, 1, tm)
    return gid, n_used, src.reshape(shape3), dst.reshape(shape3)


def _rotary_table(positions):
    half = ROT_DIM // 2
    per_row = LANES // half
    inv_freq = ROPE_THETA ** (-jnp.arange(0, ROT_DIM, 2, dtype=F32) / ROT_DIM)
    B, S = positions.shape
    pos = jnp.repeat(positions.astype(F32).reshape(B, S // per_row, per_row), half, axis=-1)
    ang = pos * jnp.tile(inv_freq, per_row)
    c = jnp.cos(ang).reshape(B, S, half)
    s = jnp.sin(ang).reshape(B, S, half)
    return jnp.concatenate([c, s], axis=-1)


def _rotary_placement():
    half = ROT_DIM // 2
    e = np.zeros((2 * half, 3 * LANES), np.float32)
    one = np.zeros((1, LANES), np.float32)
    for lane in range(LANES):
        d = lane % HEAD_DIM
        if d < half:
            e[d, lane] = 1.0
            e[half + d, LANES + lane] = -1.0
        elif d < ROT_DIM:
            e[d - half, lane] = 1.0
            e[d, 2 * LANES + lane] = 1.0
        else:
            one[0, lane] = 1.0
    return jnp.asarray(e, BF16), jnp.asarray(one, F32)


def kernel(x, p, positions, attn_norm_g, w_in, pool_w, pool_scale, diff_qn_g, diff_kn_g, diff_lambda, diff_subln_g, fox_qn_g, fox_kn_g, fox_forget_b, w_branch, w_out, ffn_norm_g, w_route_group, b_route_group, w_route_expert, b_route_expert, moe_w_gate, moe_w_up, moe_w_down, ple_norm_g, w_ple_gate, w_ple):
    B, S, D = x.shape
    T = B * S
    depth = w_in.shape[0]
    two = lambda a: jnp.concatenate([a, a], axis=-1)
    pad_lanes = lambda a: jnp.pad(a, [(0, 0)] * (a.ndim - 1) + [(0, LANES - a.shape[-1])])
    row = lambda a: a[:, None, :]

    tab = _rotary_table(positions)
    rot_e, rot_one = _rotary_placement()
    w_in_b = w_in.astype(BF16)
    qkg = jnp.stack([two(diff_qn_g), two(diff_kn_g), two(fox_qn_g), two(fox_kn_g)], axis=1)
    fb = row(pad_lanes(fox_forget_b))
    w_r = jnp.concatenate([w_route_group, w_route_expert], axis=2)
    w_r_hi = w_r.astype(BF16)
    wrh = pad_lanes(w_r_hi)
    wrl = pad_lanes((w_r - w_r_hi.astype(F32)).astype(BF16))
    brt = row(pad_lanes(jnp.concatenate([b_route_group, b_route_expert], axis=1)))
    pool_w_b, w_branch_b, w_out_b = pool_w.astype(BF16), w_branch.astype(BF16), w_out.astype(BF16)
    w_ple_gate_b, w_ple_b = w_ple_gate.astype(BF16), w_ple.astype(BF16)
    p_flat = p.reshape(depth, T, PLE_DIM)

    for l in range(depth):
        lam_init = 0.8 - 0.6 * math.exp(-0.3 * l)
        u, dq, dk, dv, fq, fk, fv, f, ft, gates = _inproj(
            l, x, row(attn_norm_g), w_in_b, tab, rot_e, rot_one, qkg, fb)
        y_diff = _diff_attention(l, dq, dk, dv, diff_lambda, row(diff_subln_g), lam_init)
        y_fox = _fox_attention(fq, fk, fv, f, ft)
        x1, hf, info, counts = _merge(
            l, x, u, y_diff, y_fox, gates, pool_w_b, row(pool_scale), w_branch_b, w_out_b,
            row(ffn_norm_g), wrh, wrl, brt)
        gid, n_used, src, dst = _routing_tables(info, counts)
        y = _moe(l, gid, n_used, src, dst, hf.reshape(T * SLAB, LANES), moe_w_gate, moe_w_up, moe_w_down,
                 wrh, wrl, brt)
        x = _ple(l, x1.reshape(T, D), y, p_flat, row(ple_norm_g), w_ple_gate_b, w_ple_b).reshape(B, S, D)
    return x
```

```python
import functools
import math

import jax
import jax.numpy as jnp
import numpy as np
from jax import lax
from jax.experimental import pallas as pl
from jax.experimental.pallas import tpu as pltpu

F32 = jnp.float32
BF16 = jnp.bfloat16

D_MODEL = 1024
HEAD_DIM = 64
LANES = 128
POOL_WINDOWS = (2, 4, 8, 16)
POOL_HALO = 16
BRANCH_WIDTH = 512
N_BRANCH = 3
FOX_HEADS = 8
ROT_DIM = HEAD_DIM // 4
ROPE_THETA = 500000.0
PLE_DIM = 256
N_GROUPS = 4
EXPERTS_PER_GROUP = 4
N_EXPERTS = 16
D_EXPERT = 256
RMS_EPS = 1e-6
NEG = -1e30
LOG2E = math.log2(math.e)

C_POOL, C_DQ, C_DK, C_DV, C_FQ, C_FK, C_FV, C_FZ, C_GZ = (0, 512, 1024, 1536, 2048, 2560, 3072, 3584, 3712)
IN_PACKED = C_GZ + N_BRANCH * D_MODEL

TM_IN = 512
TM_MERGE = 512
TM_MOE = 512
TM_PLE = 1024
GROUP_HIDDEN = EXPERTS_PER_GROUP * D_EXPERT
TQ = 256
ATTN_TILES_PER_STEP = 2
VMEM_LIMIT = 56 * 1024 * 1024


def _const_spec(shape):
    zeros = (0,) * len(shape)
    return pl.BlockSpec(shape, lambda *_: zeros, pipeline_mode=pl.Buffered(1))


def _layer_spec(layer, shape):
    zeros = (0,) * len(shape)
    return pl.BlockSpec((None,) + tuple(shape), lambda *_: (layer,) + zeros, pipeline_mode=pl.Buffered(1))


def _rms(x, g):
    return x * lax.rsqrt(jnp.mean(x * x, axis=-1, keepdims=True) + RMS_EPS) * g


SLAB = D_MODEL // LANES


def _rows_to_slabs(dst, lead, x, row0=0):
    n = x.shape[0]
    for c in range(SLAB):
        dst[lead + (pl.ds(row0 * SLAB + c, n, stride=SLAB), slice(None))] = x[:, c * LANES:(c + 1) * LANES]


def _slabs_to_rows(src, lead, n):
    return jnp.concatenate([src[lead + (pl.ds(c, n, stride=SLAB), slice(None))] for c in range(SLAB)], axis=1)


def _router_logits(hf, wrh_ref, wrl_ref, br_ref):
    hi = hf.astype(BF16)
    lo = (hf - hi.astype(F32)).astype(BF16)
    w_both = jnp.concatenate([wrh_ref[...], wrl_ref[...]], axis=1)
    hi_both = jnp.dot(hi, w_both, preferred_element_type=F32)
    return (hi_both[:, 0:LANES]
            + jnp.dot(lo, wrh_ref[...], preferred_element_type=F32)
            + hi_both[:, LANES:2 * LANES]
            + br_ref[...])


def _inproj_kernel(x_ref, g_ref, wa_ref, wb_ref, tab_ref, rot_e_ref, rot_one_ref, qkg_ref, fb_ref,
                   u_ref, dq_ref, dk_ref, dv_ref, fq_ref, fk_ref, fv_ref, f_ref, ft_ref, gates_ref,
                   carry_ref, wg_scr):
    i = pl.program_id(1)
    tm = x_ref.shape[1]

    @pl.when(i == 0)
    def _():
        carry_ref[...] = jnp.zeros_like(carry_ref)

    @pl.when((pl.program_id(0) == 0) & (i == 0))
    def _():
        wg_scr[...] = wb_ref[:, FOX_HEADS:FOX_HEADS + N_BRANCH * D_MODEL]

    h = _rms(x_ref[0], g_ref[...]).astype(BF16)

    def proj(c0, width):
        if c0 < C_FZ:
            w = wa_ref[:, c0:c0 + width]
        elif c0 == C_FZ:
            w = wb_ref[:, 0:width]
        else:
            w = wg_scr[:, c0 - C_GZ:c0 - C_GZ + width]
        return jnp.dot(h, w, preferred_element_type=F32)

    lo = lax.broadcasted_iota(jnp.int32, (tm, LANES), 1) < HEAD_DIM
    tab = tab_ref[0]
    tab_hi = tab.astype(BF16)
    tab_lo = (tab - tab_hi.astype(F32)).astype(BF16)
    placed = (jnp.dot(tab_hi, rot_e_ref[...], preferred_element_type=F32)
              + jnp.dot(tab_lo, rot_e_ref[...], preferred_element_type=F32))
    cos = placed[:, 0:LANES] + rot_one_ref[...]
    sa = placed[:, LANES:2 * LANES]
    sb = placed[:, 2 * LANES:3 * LANES]
    scale = HEAD_DIM ** -0.5 * LOG2E

    def head_norm(a, g):
        sq = a * a
        s_lo = jnp.sum(jnp.where(lo, sq, 0.0), axis=-1, keepdims=True)
        s_hi = jnp.sum(jnp.where(lo, 0.0, sq), axis=-1, keepdims=True)
        ss = jnp.where(lo, s_lo, s_hi)
        return a * lax.rsqrt(ss * (1.0 / HEAD_DIM) + RMS_EPS) * g

    def rotary(a):
        return a * cos + pltpu.roll(a, LANES - ROT_DIM // 2, 1) * sa + pltpu.roll(a, ROT_DIM // 2, 1) * sb

    def plain(out_ref):
        def epilogue(acc):
            out_ref[0] = acc.astype(BF16)
        return epilogue

    def qk(out_ref, g, rot, mult):
        def epilogue(acc):
            for j in range(4):
                a = head_norm(acc[:, j * LANES:(j + 1) * LANES], g)
                if rot:
                    a = rotary(a)
                if mult != 1.0:
                    a = a * mult
                out_ref[0, :, j * LANES:(j + 1) * LANES] = a.astype(BF16)
        return epilogue

    def forget(acc):
        z = acc + fb_ref[...]
        logf = jnp.minimum(z, 0.0) - jnp.log1p(jnp.exp(-jnp.abs(z)))
        row = lax.broadcasted_iota(jnp.int32, (tm, tm), 0)
        col = lax.broadcasted_iota(jnp.int32, (tm, tm), 1)
        tri = jnp.where(row >= col, 1.0, 0.0).astype(BF16)
        hi = logf.astype(BF16)
        lo_part = (logf - hi.astype(F32)).astype(BF16)
        both = jnp.dot(tri, jnp.concatenate([hi, lo_part], axis=1), preferred_element_type=F32)
        fcum = both[:, 0:LANES] + both[:, LANES:2 * LANES] + carry_ref[0:1, :]
        carry_ref[...] = jnp.broadcast_to(fcum[tm - 1:tm, :], carry_ref.shape)
        f_ref[0] = fcum
        ft_ref[0] = fcum.T[0:FOX_HEADS, :]

    def gate(c):
        def epilogue(acc):
            gates_ref[0, :, c * 512:(c + 1) * 512] = jax.nn.sigmoid(acc).astype(BF16)
        return epilogue

    chunks = [(C_POOL, 512, plain(u_ref)),
              (C_DQ, 512, qk(dq_ref, qkg_ref[0:1, :], True, scale)),
              (C_DK, 512, qk(dk_ref, qkg_ref[1:2, :], True, 1.0)),
              (C_DV, 512, plain(dv_ref)),
              (C_FQ, 512, qk(fq_ref, qkg_ref[2:3, :], False, scale)),
              (C_FK, 512, qk(fk_ref, qkg_ref[3:4, :], False, 1.0)),
              (C_FV, 512, plain(fv_ref)),
              (C_FZ, LANES, forget)]
    chunks += [(C_GZ + c * 512, 512, gate(c)) for c in range(N_BRANCH * D_MODEL // 512)]
    acc = proj(chunks[0][0], chunks[0][1])
    for k, (_, _, epilogue) in enumerate(chunks):
        nxt = proj(chunks[k + 1][0], chunks[k + 1][1]) if k + 1 < len(chunks) else None
        epilogue(acc)
        acc = nxt


def _inproj(layer, x, g, w, tab, rot_e, rot_one, qkg, fb):
    B, S, D = x.shape
    tm = TM_IN
    tok = lambda width: pl.BlockSpec((1, tm, width), lambda b, i: (b, i, 0))
    out_shape = (
        [jax.ShapeDtypeStruct((B, S, 512), BF16)] * 7
        + [jax.ShapeDtypeStruct((B, S, LANES), F32),
           jax.ShapeDtypeStruct((B, FOX_HEADS, S), F32),
           jax.ShapeDtypeStruct((B, S, N_BRANCH * D), BF16)]
    )
    out_specs = (
        [tok(512)] * 7
        + [tok(LANES),
           pl.BlockSpec((1, FOX_HEADS, tm), lambda b, i: (b, 0, i)),
           tok(N_BRANCH * D)]
    )
    return pl.pallas_call(
        _inproj_kernel,
        grid=(B, S // tm),
        in_specs=[tok(D), _layer_spec(layer, (1, D)),
                  _layer_spec(layer, (D, C_FZ)),
                  pl.BlockSpec((None, D, C_FZ), lambda *_: (layer, 0, 1), pipeline_mode=pl.Buffered(1)),
                  tok(2 * (ROT_DIM // 2)), _const_spec((2 * (ROT_DIM // 2), 3 * LANES)), _const_spec((1, LANES)),
                  _layer_spec(layer, (4, LANES)), _layer_spec(layer, (1, LANES))],
        out_specs=out_specs,
        out_shape=out_shape,
        scratch_shapes=[pltpu.VMEM((8, LANES), F32), pltpu.VMEM((D, N_BRANCH * D), BF16)],
        compiler_params=pltpu.CompilerParams(
            dimension_semantics=("arbitrary", "arbitrary"), vmem_limit_bytes=VMEM_LIMIT),
        name="inproj",
    )(x, g, w, w, tab, rot_e, rot_one, qkg, fb)


def _nt_dot(a, b):
    return lax.dot_general(a, b, (((1,), (1,)), ((), ())), preferred_element_type=F32)


def _score_rows(score_chunk, n_chunks, s_scr):
    tq = TQ
    mt = jnp.full((tq, LANES), NEG, F32)
    for c in range(n_chunks):
        s = score_chunk(c)
        s_scr[:, c * tq:(c + 1) * tq] = s
        for w in range(tq // LANES):
            mt = jnp.maximum(mt, s[:, w * LANES:(w + 1) * LANES])
    return jnp.max(mt, axis=1, keepdims=True)


def _exp_rows(n_chunks, m, s_scr, p_scr):
    tq = TQ
    for c in range(n_chunks):
        p_scr[:, c * tq:(c + 1) * tq] = jnp.exp2(s_scr[:, c * tq:(c + 1) * tq] - m).astype(BF16)


def _diff_kernel(lam_init, q_ref, k_ref, v_ref, lam_ref, g_ref, o_ref, s_scr, p_scr, v2_scr):
    S = q_ref.shape[1]
    tq = TQ
    n_t = q_ref.shape[2] // LANES
    lo = lax.broadcasted_iota(jnp.int32, (tq, LANES), 1) < HEAD_DIM
    causal = (lax.broadcasted_iota(jnp.int32, (tq, tq), 1) <= lax.broadcasted_iota(jnp.int32, (tq, tq), 0))
    lp = lam_ref[...]
    lam = (jnp.exp(jnp.sum(lp[0:1, :] * lp[1:2, :], axis=1, keepdims=True))
           - jnp.exp(jnp.sum(lp[2:3, :] * lp[3:4, :], axis=1, keepdims=True)) + lam_init)
    for t in range(n_t):
        v2_scr[t, :, 0:LANES] = v_ref[0, :, t * LANES:(t + 1) * LANES]
        v2_scr[t, :, LANES:2 * LANES] = jnp.ones((S, LANES), BF16)

    def s_buf(qi, t, comp):
        return s_scr.at[((qi % 2) * n_t + t) * 2 + comp]

    def scores(qi, t, comp):
        q = q_ref[0, qi * tq:(qi + 1) * tq, t * LANES:(t + 1) * LANES]
        zero = jnp.zeros_like(q)
        qm = jnp.where(lo, q, zero) if comp == 0 else jnp.where(lo, zero, q)

        def score_chunk(c):
            s = _nt_dot(qm, k_ref[0, c * tq:(c + 1) * tq, t * LANES:(t + 1) * LANES])
            return jnp.where(causal, s, NEG) if c == qi else s

        return _score_rows(score_chunk, qi + 1, s_buf(qi, t, comp))

    n_q = S // tq
    streams = [(t, comp) for t in range(n_t) for comp in range(2)]
    row_max = {(0,) + st: scores(0, *st) for st in streams}
    for qi in range(n_q):
        kend = (qi + 1) * tq
        parts = {}
        for t, comp in streams:
            _exp_rows(qi + 1, row_max[(qi, t, comp)], s_buf(qi, t, comp), p_scr.at[2 * t + comp])
            if qi + 1 < n_q:
                row_max[(qi + 1, t, comp)] = scores(qi + 1, t, comp)
            pv = jnp.dot(p_scr[2 * t + comp, :, 0:kend], v2_scr[t, 0:kend, :], preferred_element_type=F32)
            parts[(t, comp)] = pv[:, 0:LANES] / pv[:, LANES:2 * LANES]
        for t in range(n_t):
            o = parts[(t, 0)] - lam * parts[(t, 1)]
            o = _rms(o, g_ref[...]) * (1.0 - lam_init)
            o_ref[0, qi * tq:kend, t * LANES:(t + 1) * LANES] = o.astype(BF16)


def _diff_attention(layer, dq, dk, dv, lam_p, subln_g, lam_init):
    B, S, width = dq.shape
    n_t = ATTN_TILES_PER_STEP
    blk = pl.BlockSpec((1, S, n_t * LANES), lambda b, h: (b, 0, h))
    return pl.pallas_call(
        functools.partial(_diff_kernel, lam_init),
        grid=(B, width // (n_t * LANES)),
        in_specs=[blk, blk, blk, _layer_spec(layer, (4, HEAD_DIM)), _layer_spec(layer, (1, LANES))],
        out_specs=blk,
        out_shape=jax.ShapeDtypeStruct((B, S, width), BF16),
        scratch_shapes=[pltpu.VMEM((4 * n_t, TQ, S), F32), pltpu.VMEM((2 * n_t, TQ, S), BF16),
                        pltpu.VMEM((n_t, S, 2 * LANES), BF16)],
        compiler_params=pltpu.CompilerParams(
            dimension_semantics=("parallel", "parallel"), vmem_limit_bytes=VMEM_LIMIT),
        name="diff_attn",
    )(dq, dk, dv, lam_p, subln_g)


def _fox_kernel(q_ref, k_ref, v_ref, f_ref, ft_ref, o_ref, s_scr, p_scr, v2_scr):
    S = q_ref.shape[1]
    tq = TQ
    n_t = q_ref.shape[2] // LANES
    j = pl.program_id(1)
    lane = lax.broadcasted_iota(jnp.int32, (tq, LANES), 1)
    lo = lane < HEAD_DIM
    causal = (lax.broadcasted_iota(jnp.int32, (tq, tq), 1) <= lax.broadcasted_iota(jnp.int32, (tq, tq), 0))
    lo_s = lax.broadcasted_iota(jnp.int32, (S, LANES), 1) < HEAD_DIM
    for t in range(n_t):
        v = v_ref[0, :, t * LANES:(t + 1) * LANES]
        one = jnp.ones_like(v)
        v2_scr[2 * t] = jnp.where(lo_s, v, one)
        v2_scr[2 * t + 1] = jnp.where(lo_s, one, v)

    def s_buf(qi, t, hh):
        return s_scr.at[((qi % 2) * n_t + t) * 2 + hh]

    def scores(qi, t, hh):
        head = 2 * (n_t * j + t) + hh
        q = q_ref[0, qi * tq:(qi + 1) * tq, t * LANES:(t + 1) * LANES]
        zero = jnp.zeros_like(q)
        qm = jnp.where(lo, q, zero) if hh == 0 else jnp.where(lo, zero, q)
        f_tile = f_ref[0, qi * tq:(qi + 1) * tq, :] * LOG2E
        f_t = jnp.sum(jnp.where(lane == head, f_tile, 0.0), axis=1, keepdims=True)

        def score_chunk(c):
            f_s = ft_ref[0, pl.ds(head, 1), c * tq:(c + 1) * tq] * LOG2E
            s = _nt_dot(qm, k_ref[0, c * tq:(c + 1) * tq, t * LANES:(t + 1) * LANES]) + (f_t - f_s)
            return jnp.where(causal, s, NEG) if c == qi else s

        return _score_rows(score_chunk, qi + 1, s_buf(qi, t, hh))

    n_q = S // tq
    streams = [(t, hh) for t in range(n_t) for hh in range(2)]
    row_max = {(0,) + st: scores(0, *st) for st in streams}
    for qi in range(n_q):
        kend = (qi + 1) * tq
        pvs = {}
        for t, hh in streams:
            _exp_rows(qi + 1, row_max[(qi, t, hh)], s_buf(qi, t, hh), p_scr.at[2 * t + hh])
            if qi + 1 < n_q:
                row_max[(qi + 1, t, hh)] = scores(qi + 1, t, hh)
            pvs[(t, hh)] = jnp.dot(p_scr[2 * t + hh, :, 0:kend], v2_scr[2 * t + hh, 0:kend, :],
                                   preferred_element_type=F32)
        for t in range(n_t):
            a, b = pvs[(t, 0)], pvs[(t, 1)]
            o = jnp.where(lo, a / pltpu.roll(a, HEAD_DIM, 1), b / pltpu.roll(b, HEAD_DIM, 1))
            o_ref[0, qi * tq:kend, t * LANES:(t + 1) * LANES] = o.astype(BF16)


def _fox_attention(fq, fk, fv, f, ft):
    B, S, width = fq.shape
    n_t = ATTN_TILES_PER_STEP
    blk = pl.BlockSpec((1, S, n_t * LANES), lambda b, j: (b, 0, j))
    return pl.pallas_call(
        _fox_kernel,
        grid=(B, width // (n_t * LANES)),
        in_specs=[blk, blk, blk,
                  pl.BlockSpec((1, S, LANES), lambda b, j: (b, 0, 0)),
                  pl.BlockSpec((1, FOX_HEADS, S), lambda b, j: (b, 0, 0))],
        out_specs=blk,
        out_shape=jax.ShapeDtypeStruct((B, S, width), BF16),
        scratch_shapes=[pltpu.VMEM((4 * n_t, TQ, S), F32), pltpu.VMEM((2 * n_t, TQ, S), BF16),
                        pltpu.VMEM((2 * n_t, S, LANES), BF16)],
        compiler_params=pltpu.CompilerParams(
            dimension_semantics=("parallel", "parallel"), vmem_limit_bytes=VMEM_LIMIT),
        name="fox_attn",
    )(fq, fk, fv, f, ft)


def _merge_kernel(x_ref, u_ref, uh_ref, yd_ref, yf_ref, gates_ref, pw_ref, ps_ref, wb_ref, wo_ref,
                  ng_ref, wrh_ref, wrl_ref, br_ref,
                  x1_ref, hf_ref, info_ref, count_ref):
    i = pl.program_id(1)
    tm = x_ref.shape[1]

    @pl.when((pl.program_id(0) == 0) & (i == 0))
    def _():
        count_ref[...] = jnp.zeros_like(count_ref)

    hm = tm // 2
    half = D_MODEL // 2
    first_halo = uh_ref[0].astype(F32) * jnp.where(i > 0, 1.0, 0.0)

    def mix(h):
        r0 = h * hm
        halo = first_halo if h == 0 else u_ref[0, r0 - POOL_HALO:r0, :].astype(F32)
        ext = jnp.concatenate([halo, u_ref[0, r0:r0 + hm, :].astype(F32)], axis=0)
        t = i * tm + r0 + lax.broadcasted_iota(jnp.int32, (hm, 1), 0)
        ys, gated = [], {}
        for g, w in enumerate(POOL_WINDOWS):
            n, c0 = 1 + g // 2, (g % 2) * half
            y_att = (yd_ref if n == 1 else yf_ref)[0, r0:r0 + hm, :]
            gated[(n, g % 2)] = (gates_ref[0, r0:r0 + hm, n * D_MODEL + c0:n * D_MODEL + c0 + half].astype(F32)
                                 * jnp.dot(y_att, wb_ref[n, :, c0:c0 + half], preferred_element_type=F32))
            e = ext[:, g * LANES:(g + 1) * LANES]
            win = e
            step = 1
            while step < w:
                win = win + pltpu.roll(win, step, 0)
                step *= 2
            cnt = jnp.minimum(t + 1, w).astype(F32)
            d = win[POOL_HALO:, :] / cnt - e[POOL_HALO:, :]
            ys.append(jnp.dot(d.astype(BF16), pw_ref[g], preferred_element_type=F32))
        return jnp.concatenate(ys, axis=1) * ps_ref[...], gated

    def project(h, y_pool, gated):
        r0 = h * hm
        br_pool = jnp.dot(y_pool.astype(BF16), wb_ref[0], preferred_element_type=F32)
        att = [jnp.concatenate([gated[(n, 0)], gated[(n, 1)]], axis=1) for n in (1, 2)]
        merged = gates_ref[0, r0:r0 + hm, 0:D_MODEL].astype(F32) * br_pool + att[0] + att[1]
        x1 = x_ref[0, r0:r0 + hm, :] + jnp.dot(merged.astype(BF16), wo_ref[...], preferred_element_type=F32)
        x1_ref[0, r0:r0 + hm, :] = x1
        return x1

    def route(h, x1):
        hf = _rms(x1, ng_ref[...])
        _rows_to_slabs(hf_ref, (0,), hf, h * hm)
        return _router_logits(hf, wrh_ref, wrl_ref, br_ref)

    mixed = [mix(h) for h in range(2)]
    x1s = [project(h, *mixed[h]) for h in range(2)]
    lg = jnp.concatenate([route(h, x1s[h]) for h in range(2)], axis=0)

    lane = lax.broadcasted_iota(jnp.int32, (tm, LANES), 1)
    lane_f = lane.astype(F32)
    gmask = lane < N_GROUPS
    gl = jnp.where(gmask, lg, NEG)
    gmax = jnp.max(gl, axis=1, keepdims=True)
    gi = jnp.min(jnp.where(gmask & (gl == gmax), lane_f, float(LANES)), axis=1, keepdims=True)

    onehot = jnp.where(lane_f == gi, 1.0, 0.0)
    row = lax.broadcasted_iota(jnp.int32, (tm, tm), 0)
    col = lax.broadcasted_iota(jnp.int32, (tm, tm), 1)
    tri = jnp.where(row >= col, 1.0, 0.0).astype(BF16)
    incl = jnp.dot(tri, onehot.astype(BF16), preferred_element_type=F32)
    before = count_ref[0:1, :]
    rank = jnp.sum(onehot * (incl - onehot + before), axis=1, keepdims=True)
    count_ref[...] = jnp.broadcast_to(before + incl[tm - 1:tm, :], count_ref.shape)
    info_ref[0] = jnp.where(lane == 0, gi, jnp.where(lane == 1, rank, 0.0)).T[0:8, :]


def _merge(layer, x, u, yd, yf, gates, pw, ps, wb, wo, ng, wrh, wrl, brt):
    B, S, D = x.shape
    tm = TM_MERGE
    tok = lambda width: pl.BlockSpec((1, tm, width), lambda b, i: (b, i, 0))
    halo_blocks = tm // POOL_HALO
    halo = pl.BlockSpec((1, POOL_HALO, 512), lambda b, i: (b, jnp.maximum(i * halo_blocks - 1, 0), 0))
    return pl.pallas_call(
        _merge_kernel,
        grid=(B, S // tm),
        in_specs=[tok(D), tok(512), halo, tok(512), tok(512), tok(N_BRANCH * D),
                  _layer_spec(layer, (4, LANES, LANES)), _layer_spec(layer, (1, 512)),
                  _layer_spec(layer, (N_BRANCH, BRANCH_WIDTH, D)), _layer_spec(layer, (D, D)),
                  _layer_spec(layer, (1, D)), _layer_spec(layer, (D, LANES)), _layer_spec(layer, (D, LANES)),
                  _layer_spec(layer, (1, LANES))],
        out_specs=[tok(D), pl.BlockSpec((1, tm * SLAB, LANES), lambda b, i: (b, i, 0)),
                   pl.BlockSpec((1, 8, tm), lambda b, i: (b, 0, i)),
                   pl.BlockSpec((8, LANES), lambda b, i: (0, 0))],
        out_shape=[jax.ShapeDtypeStruct((B, S, D), F32),
                   jax.ShapeDtypeStruct((B, S * SLAB, LANES), F32),
                   jax.ShapeDtypeStruct((B, 8, S), F32),
                   jax.ShapeDtypeStruct((8, LANES), F32)],
        compiler_params=pltpu.CompilerParams(
            dimension_semantics=("arbitrary", "arbitrary"), vmem_limit_bytes=VMEM_LIMIT),
        name="merge",
    )(x, u, u, yd, yf, gates, pw, ps, wb, wo, ng, wrh, wrl, brt)


def _moe_kernel(gid_ref, nused_ref, src_ref, src_next_ref, dst_prev_ref,
                hf_hbm, wg_ref, wu_ref, wd_ref, wrh_ref, wrl_ref, br_ref,
                y_hbm,
                xbuf, ybuf, wgu_bf, wd_bf, sem_g, sem_s):
    i = pl.program_id(0)
    n_used = nused_ref[0]
    tm = TM_MOE
    n_tiles = pl.num_programs(0) - 1

    def slab(row):
        start = row * SLAB
        return pl.ds(start if isinstance(row, int) else pl.multiple_of(start, SLAB), SLAB)

    def gather_row(idx_ref, r, to_slot):
        return pltpu.make_async_copy(hf_hbm.at[slab(idx_ref[0, 0, r])], xbuf.at[to_slot, slab(r)],
                                     sem_g.at[to_slot])

    def scatter_row(r, from_slot):
        return pltpu.make_async_copy(ybuf.at[from_slot, slab(r)], y_hbm.at[slab(dst_prev_ref[0, 0, r])],
                                     sem_s.at[from_slot])

    def start_rows(make_copy, unrolled):
        if unrolled:
            for r in range(tm):
                make_copy(r).start()
        else:
            def body(r, c):
                make_copy(r).start()
                return c
            lax.fori_loop(0, tm, body, 0, unroll=8)

    def wait_gather(of_slot):
        pltpu.make_async_copy(hf_hbm.at[pl.ds(0, tm * SLAB)], xbuf.at[of_slot], sem_g.at[of_slot]).wait()

    def wait_scatter(of_slot):
        pltpu.make_async_copy(ybuf.at[of_slot], y_hbm.at[pl.ds(0, tm * SLAB)], sem_s.at[of_slot]).wait()

    def cast_weights():
        for e in range(EXPERTS_PER_GROUP):
            wgu_bf[:, e * D_EXPERT:(e + 1) * D_EXPERT] = wg_ref[e].astype(BF16)
            wgu_bf[:, GROUP_HIDDEN + e * D_EXPERT:GROUP_HIDDEN + (e + 1) * D_EXPERT] = wu_ref[e].astype(BF16)
            wd_bf[e * D_EXPERT:(e + 1) * D_EXPERT, :] = wd_ref[e].astype(BF16)

    def run_experts(slot, g):
        hf = _slabs_to_rows(xbuf, (slot,), tm)

        lg = _router_logits(hf, wrh_ref, wrl_ref, br_ref)
        lane = lax.broadcasted_iota(jnp.int32, (tm, LANES), 1)
        lane_f = lane.astype(F32)
        far = float(LANES)
        rmax = lambda a: jnp.max(a, axis=1, keepdims=True)
        rmin = lambda a: jnp.min(a, axis=1, keepdims=True)
        rsum = lambda a: jnp.sum(a, axis=1, keepdims=True)
        lg_g = rsum(jnp.where(lane == g, lg, 0.0))
        gp = 1.0 / rsum(jnp.where(lane < N_GROUPS, jnp.exp(lg - lg_g), 0.0))
        e_lo = N_GROUPS + EXPERTS_PER_GROUP * g
        emask = (lane >= e_lo) & (lane < e_lo + EXPERTS_PER_GROUP)
        el = jnp.where(emask, lg, NEG)
        m1 = rmax(el)
        i1 = rmin(jnp.where(emask & (el == m1), lane_f, far))
        rest = emask & (lane_f != i1)
        el2 = jnp.where(rest, lg, NEG)
        m2 = rmax(el2)
        i2 = rmin(jnp.where(rest & (el2 == m2), lane_f, far))
        r = jnp.exp(m2 - m1)
        comb = jnp.where(lane_f == i1, gp / (1.0 + r), jnp.where(lane_f == i2, gp * r / (1.0 + r), 0.0))
        cw = jnp.concatenate(
            [jnp.broadcast_to(rsum(jnp.where(lane == e_lo + e, comb, 0.0)), (tm, D_EXPERT))
             for e in range(EXPERTS_PER_GROUP)], axis=1)

        gu = jnp.dot(hf.astype(BF16), wgu_bf[...], preferred_element_type=F32)
        gate, up = gu[:, 0:GROUP_HIDDEN], gu[:, GROUP_HIDDEN:2 * GROUP_HIDDEN]
        act = gate * jax.nn.sigmoid(gate) * up * cw
        _rows_to_slabs(ybuf, (slot,), jnp.dot(act.astype(BF16), wd_bf[...], preferred_element_type=F32))

    tile = jnp.minimum(i, n_tiles - 1)
    g = gid_ref[tile]
    steady = (i >= 2) & (i + 1 < n_used)

    def steady_step(slot):
        wait_gather(slot)
        wait_scatter(slot)

        @pl.when(g != gid_ref[tile - 1])
        def _():
            cast_weights()

        start_rows(lambda r: gather_row(src_next_ref, r, 1 - slot), True)
        start_rows(lambda r: scatter_row(r, 1 - slot), True)
        run_experts(slot, g)

    for parity in range(2):
        @pl.when(steady & (lax.rem(i, 2) == parity))
        def _(parity=parity):
            steady_step(parity)

    @pl.when(jnp.logical_not(steady))
    def _():
        slot = lax.rem(i, 2)

        @pl.when(i == 0)
        def _():
            start_rows(lambda r: gather_row(src_ref, r, slot), False)
            ybuf[1] = jnp.zeros((tm * SLAB, LANES), F32)
            n_real = hf_hbm.shape[0]
            spare = [pltpu.make_async_copy(ybuf.at[1], y_hbm.at[pl.ds(n_real + k * tm * SLAB, tm * SLAB)], sem_s.at[1])
                     for k in range(N_GROUPS)]
            for cp in spare:
                cp.start()
            for cp in spare:
                cp.wait()

        @pl.when(i < n_used)
        def _():
            wait_gather(slot)

        @pl.when(i + 1 < n_used)
        def _():
            start_rows(lambda r: gather_row(src_next_ref, r, 1 - slot), False)

        @pl.when((i >= 1) & (i <= n_used))
        def _():
            start_rows(lambda r: scatter_row(r, 1 - slot), False)

        @pl.when(i < n_used)
        def _():
            @pl.when(i >= 2)
            def _():
                wait_scatter(slot)

            @pl.when((i == 0) | (g != gid_ref[jnp.maximum(tile - 1, 0)]))
            def _():
                cast_weights()

            run_experts(slot, g)

        @pl.when(i == n_used)
        def _():
            wait_scatter(1 - slot)

            @pl.when(n_used >= 2)
            def _():
                wait_scatter(slot)


def _moe(layer, gid, n_used, src, dst, hf, wg, wu, wd, wrh, wrl, brt):
    D = D_MODEL
    tm = TM_MOE
    n_tiles = src.shape[0]
    clamp = lambda t: jnp.clip(t, 0, n_tiles - 1)
    idx_spec = lambda shift: pl.BlockSpec(
        (1, 1, tm), lambda i, gid, nu: (clamp(i + shift), 0, 0), memory_space=pltpu.SMEM)
    grp = lambda shape: pl.BlockSpec((None,) + shape, lambda i, gid, nu: (layer, gid[clamp(i)], 0, 0))
    return pl.pallas_call(
        _moe_kernel,
        grid_spec=pltpu.PrefetchScalarGridSpec(
            num_scalar_prefetch=2,
            grid=(n_tiles + 1,),
            in_specs=[idx_spec(0), idx_spec(1), idx_spec(-1),
                      pl.BlockSpec(memory_space=pl.ANY),
                      grp((EXPERTS_PER_GROUP, D, D_EXPERT)), grp((EXPERTS_PER_GROUP, D, D_EXPERT)),
                      grp((EXPERTS_PER_GROUP, D_EXPERT, D)),
                      _layer_spec(layer, (D, LANES)), _layer_spec(layer, (D, LANES)), _layer_spec(layer, (1, LANES))],
            out_specs=pl.BlockSpec(memory_space=pl.ANY),
            scratch_shapes=[pltpu.VMEM((2, tm * SLAB, LANES), F32), pltpu.VMEM((2, tm * SLAB, LANES), F32),
                            pltpu.VMEM((D, 2 * GROUP_HIDDEN), BF16), pltpu.VMEM((GROUP_HIDDEN, D), BF16),
                            pltpu.SemaphoreType.DMA((2,)), pltpu.SemaphoreType.DMA((2,))]),
        out_shape=jax.ShapeDtypeStruct((n_tiles * tm * SLAB, LANES), F32),
        compiler_params=pltpu.CompilerParams(
            dimension_semantics=("arbitrary",), vmem_limit_bytes=VMEM_LIMIT),
        name="moe",
    )(gid, n_used, src, src, dst, hf, wg, wu, wd, wrh, wrl, brt)


def _ple_kernel(x1_ref, y_ref, p_ref, pg_ref, wpg_ref, wp_ref, o_ref):
    x2 = x1_ref[...] + _slabs_to_rows(y_ref, (), x1_ref.shape[0])
    hp = _rms(x2, pg_ref[...]).astype(BF16)
    gate = jax.nn.sigmoid(jnp.dot(hp, wpg_ref[...], preferred_element_type=F32))
    emb = jnp.dot(p_ref[...].astype(BF16), wp_ref[...], preferred_element_type=F32)
    o_ref[...] = x2 + gate * emb


def _ple(layer, x1, y, p, pg, wpg, wp):
    T, D = x1.shape
    tm = TM_PLE
    tok = lambda width: pl.BlockSpec((tm, width), lambda i: (i, 0))
    return pl.pallas_call(
        _ple_kernel,
        grid=(T // tm,),
        in_specs=[tok(D), pl.BlockSpec((tm * SLAB, LANES), lambda i: (i, 0)),
                  pl.BlockSpec((None, tm, PLE_DIM), lambda i: (layer, i, 0)),
                  _layer_spec(layer, (1, D)), _layer_spec(layer, (D, D)), _layer_spec(layer, (PLE_DIM, D))],
        out_specs=tok(D),
        out_shape=jax.ShapeDtypeStruct((T, D), F32),
        compiler_params=pltpu.CompilerParams(
            dimension_semantics=("parallel",), vmem_limit_bytes=VMEM_LIMIT),
        name="ple",
    )(x1, y, p, pg, wpg, wp)


def _routing_tables(info, counts):
    tm = TM_MOE
    T = info.shape[0] * info.shape[2]
    n_rows = T + N_GROUPS * tm
    n_tiles = n_rows // tm
    group = info[:, 0, :].reshape(T).astype(jnp.int32)
    rank = info[:, 1, :].reshape(T).astype(jnp.int32)
    cnt = counts[0, :N_GROUPS].astype(jnp.int32)
    padded = (cnt + tm - 1) // tm * tm
    ends = jnp.cumsum(padded)
    base = ends - padded
    pos = base[group] + rank
    tok_of_row = jnp.full((n_rows,), -1, jnp.int32).at[pos].set(jnp.arange(T, dtype=jnp.int32))
    real = tok_of_row >= 0
    src = jnp.where(real, tok_of_row, 0)
    pad_index = jnp.cumsum(jnp.where(real, 0, 1)) - 1
    dst = jnp.where(real, tok_of_row, T + pad_index)
    starts = jnp.arange(n_tiles, dtype=jnp.int32) * tm
    gid = jnp.minimum(jnp.sum(starts[:, None] >= ends[None, :], axis=1), N_GROUPS - 1).astype(jnp.int32)
    n_used = (ends[N_GROUPS - 1] // tm).astype(jnp.int32)[None]
    gid = jnp.where(starts < ends[N_GROUPS - 1], gid, gid[jnp.maximum(n_used[0] - 1, 0)])
    shape3 = (n_tiles, 1, tm)
    return gid, n_used, src.reshape(shape3), dst.reshape(shape3)


def _rotary_table(positions):
    half = ROT_DIM // 2
    per_row = LANES // half
    inv_freq = ROPE_THETA ** (-jnp.arange(0, ROT_DIM, 2, dtype=F32) / ROT_DIM)
    B, S = positions.shape
    pos = jnp.repeat(positions.astype(F32).reshape(B, S // per_row, per_row), half, axis=-1)
    ang = pos * jnp.tile(inv_freq, per_row)
    c = jnp.cos(ang).reshape(B, S, half)
    s = jnp.sin(ang).reshape(B, S, half)
    return jnp.concatenate([c, s], axis=-1)


def _rotary_placement():
    half = ROT_DIM // 2
    e = np.zeros((2 * half, 3 * LANES), np.float32)
    one = np.zeros((1, LANES), np.float32)
    for lane in range(LANES):
        d = lane % HEAD_DIM
        if d < half:
            e[d, lane] = 1.0
            e[half + d, LANES + lane] = -1.0
        elif d < ROT_DIM:
            e[d - half, lane] = 1.0
            e[d, 2 * LANES + lane] = 1.0
        else:
            one[0, lane] = 1.0
    return jnp.asarray(e, BF16), jnp.asarray(one, F32)


def kernel(x, p, positions, attn_norm_g, w_in, pool_w, pool_scale, diff_qn_g, diff_kn_g, diff_lambda, diff_subln_g, fox_qn_g, fox_kn_g, fox_forget_b, w_branch, w_out, ffn_norm_g, w_route_group, b_route_group, w_route_expert, b_route_expert, moe_w_gate, moe_w_up, moe_w_down, ple_norm_g, w_ple_gate, w_ple):
    B, S, D = x.shape
    T = B * S
    depth = w_in.shape[0]
    two = lambda a: jnp.concatenate([a, a], axis=-1)
    pad_lanes = lambda a: jnp.pad(a, [(0, 0)] * (a.ndim - 1) + [(0, LANES - a.shape[-1])])
    row = lambda a: a[:, None, :]

    tab = _rotary_table(positions)
    rot_e, rot_one = _rotary_placement()
    w_in_b = w_in.astype(BF16)
    qkg = jnp.stack([two(diff_qn_g), two(diff_kn_g), two(fox_qn_g), two(fox_kn_g)], axis=1)
    fb = row(pad_lanes(fox_forget_b))
    w_r = jnp.concatenate([w_route_group, w_route_expert], axis=2)
    w_r_hi = w_r.astype(BF16)
    wrh = pad_lanes(w_r_hi)
    wrl = pad_lanes((w_r - w_r_hi.astype(F32)).astype(BF16))
    brt = row(pad_lanes(jnp.concatenate([b_route_group, b_route_expert], axis=1)))
    pool_w_b, w_branch_b, w_out_b = pool_w.astype(BF16), w_branch.astype(BF16), w_out.astype(BF16)
    w_ple_gate_b, w_ple_b = w_ple_gate.astype(BF16), w_ple.astype(BF16)
    p_flat = p.reshape(depth, T, PLE_DIM)

    for l in range(depth):
        lam_init = 0.8 - 0.6 * math.exp(-0.3 * l)
        u, dq, dk, dv, fq, fk, fv, f, ft, gates = _inproj(
            l, x, row(attn_norm_g), w_in_b, tab, rot_e, rot_one, qkg, fb)
        y_diff = _diff_attention(l, dq, dk, dv, diff_lambda, row(diff_subln_g), lam_init)
        y_fox = _fox_attention(fq, fk, fv, f, ft)
        x1, hf, info, counts = _merge(
            l, x, u, y_diff, y_fox, gates, pool_w_b, row(pool_scale), w_branch_b, w_out_b,
            row(ffn_norm_g), wrh, wrl, brt)
        gid, n_used, src, dst = _routing_tables(info, counts)
        y = _moe(l, gid, n_used, src, dst, hf.reshape(T * SLAB, LANES), moe_w_gate, moe_w_up, moe_w_down,
                 wrh, wrl, brt)
        x = _ple(l, x1.reshape(T, D), y, p_flat, row(ple_norm_g), w_ple_gate_b, w_ple_b).reshape(B, S, D)
    return x
```
